```python
import math
import jax, jax.numpy as jnp
from jax import lax
import numpy as np

D_MODEL = 2048
BATCH = 4
SEQ = 2048
DEPTH = 1
DEC_BATCH = 128
DEC_SEQ = 4
PAST_LEN = 2048
PAGE_SIZE = 128

D_SSM = 1024
SSM_GROUP = 16
N_GROUPS = D_SSM // SSM_GROUP
SSM_STATE = 64
D_ATT = 1024
HEAD_DIM = 64
N_HEADS = D_ATT // (2 * HEAD_DIM)
D_FF = 4 * D_MODEL
N_IN = D_SSM + 3 * D_ATT + 2 * D_MODEL
BLOCK_Q = 128
LN_EPS = 1e-5
DEEPNORM_ALPHA = (2 * DEPTH) ** 0.25
DEEPNORM_BETA = (8 * DEPTH) ** -0.25

kernel_name = 'hybrid_s5_diffattn_alibi_deepnorm_step'


def layer_norm(x, g=None, b=None):
    xf = x.astype(jnp.float32)
    mu = jnp.mean(xf, axis=-1, keepdims=True)
    xc = xf - mu
    var = jnp.mean(xc * xc, axis=-1, keepdims=True)
    y = xc * lax.rsqrt(var + LN_EPS)
    if g is not None:
        y = y * g.astype(jnp.float32) + b.astype(jnp.float32)
    return y.astype(x.dtype)


def rms_norm(x, g):
    xf = x.astype(jnp.float32)
    y = xf * lax.rsqrt(jnp.mean(xf * xf, axis=-1, keepdims=True) + LN_EPS)
    return (y * g.astype(jnp.float32)).astype(x.dtype)


def alibi_slopes():
    return jnp.asarray(np.array([2.0 ** (-8.0 * (h + 1) / N_HEADS) for h in range(N_HEADS)], dtype=np.float32))


def _complex_affine_combine(e1, e2):
    a1r, a1i, b1r, b1i = e1
    a2r, a2i, b2r, b2i = e2
    return (a2r * a1r - a2i * a1i,
            a2r * a1i + a2i * a1r,
            a2r * b1r - a2i * b1i + b2r,
            a2r * b1i + a2i * b1r + b2i)


def s5_branch(u, s0_re, s0_im, a_re, a_im, log_dt, b_re, b_im, c_re, c_im, d_skip, w_glu):
    f32 = jnp.float32
    bsz, seq_len, _ = u.shape
    ug = u.reshape(bsz, seq_len, N_GROUPS, SSM_GROUP).astype(f32)
    a_re = a_re.astype(f32)
    a_im = a_im.astype(f32)
    dt = jnp.exp(log_dt.astype(f32))[:, None]
    mag = jnp.exp(dt * a_re)
    ab_re = mag * jnp.cos(dt * a_im)
    ab_im = mag * jnp.sin(dt * a_im)
    den = a_re * a_re + a_im * a_im
    f_re = ((ab_re - 1.0) * a_re + ab_im * a_im) / den
    f_im = (ab_im * a_re - (ab_re - 1.0) * a_im) / den
    br = b_re.astype(f32)
    bi = b_im.astype(f32)
    bb_re = f_re[..., None] * br - f_im[..., None] * bi
    bb_im = f_re[..., None] * bi + f_im[..., None] * br
    bu_re = jnp.einsum('gpc,blgc->blgp', bb_re, ug)
    bu_im = jnp.einsum('gpc,blgc->blgp', bb_im, ug)
    s0_re = s0_re.astype(f32)
    s0_im = s0_im.astype(f32)
    bu_re = bu_re.at[:, 0].add(ab_re * s0_re - ab_im * s0_im)
    bu_im = bu_im.at[:, 0].add(ab_re * s0_im + ab_im * s0_re)
    shape = bu_re.shape
    _, _, s_re, s_im = lax.associative_scan(
        _complex_affine_combine,
        (jnp.broadcast_to(ab_re, shape), jnp.broadcast_to(ab_im, shape), bu_re, bu_im),
        axis=1)
    y = (jnp.einsum('gcp,blgp->blgc', c_re.astype(f32), s_re)
         - jnp.einsum('gcp,blgp->blgc', c_im.astype(f32), s_im)
         + d_skip.reshape(N_GROUPS, SSM_GROUP).astype(f32) * ug)
    z = jax.nn.gelu(y.reshape(bsz, seq_len, D_SSM)).astype(u.dtype)
    out = z * jax.nn.sigmoid(z @ w_glu)
    return out, s_re[:, -1], s_im[:, -1]


def diff_attn_block(q1, q2, q_pos, k1, k2, v, k_pos, lam):
    scale = HEAD_DIM ** -0.5
    dist = (q_pos[:, None] - k_pos[None, :]).astype(jnp.float32)
    bias = jnp.where(dist[None] >= 0, -alibi_slopes()[:, None, None] * dist[None], -jnp.inf)
    s1 = jnp.einsum('bqhd,bkhd->bhqk', q1, k1).astype(jnp.float32) * scale + bias
    s2 = jnp.einsum('bqhd,bkhd->bhqk', q2, k2).astype(jnp.float32) * scale + bias
    p = jax.nn.softmax(s1, axis=-1) - lam * jax.nn.softmax(s2, axis=-1)
    return jnp.einsum('bhqk,bkhd->bqhd', p.astype(v.dtype), v)


def diff_attention(q1, q2, q_pos, k1, k2, v, k_pos, lam):
    bsz, tq = q1.shape[:2]
    if tq <= BLOCK_Q or tq % BLOCK_Q != 0:
        return diff_attn_block(q1, q2, q_pos, k1, k2, v, k_pos, lam)
    nb = tq // BLOCK_Q

    def blk(t):
        return jnp.moveaxis(t.reshape(bsz, nb, BLOCK_Q, *t.shape[2:]), 1, 0)

    out = lax.map(lambda xs: diff_attn_block(xs[0], xs[1], xs[2], k1, k2, v, k_pos, lam),
                  (blk(q1), blk(q2), q_pos.reshape(nb, BLOCK_Q)))
    return jnp.moveaxis(out, 0, 1).reshape(bsz, tq, N_HEADS, 2 * HEAD_DIM)


def trunk(x, c, s_re, s_im, cache_k, cache_v, page_table, weights):
    (w_ada, b_ada, w_in, ssm_a_re, ssm_a_im, ssm_log_dt, ssm_b_re, ssm_b_im, ssm_c_re, ssm_c_im,
     ssm_d, w_glu, w_up_ssm, lam_q1, lam_k1, lam_q2, lam_k2, subln_g, w_up_att, w_o,
     ln1_g, ln1_b, w_ff1, w_ff2, ln2_g, ln2_b) = weights
    bsz, seq_len, _ = x.shape
    splits = [D_SSM, D_SSM + D_ATT, D_SSM + 2 * D_ATT, D_SSM + 3 * D_ATT, D_SSM + 3 * D_ATT + D_MODEL]
    new_k, new_v, new_sr, new_si = [], [], [], []
    for l in range(DEPTH):
        lambda_init = 0.8 - 0.6 * math.exp(-0.3 * l)
        mod = jax.nn.silu(c) @ w_ada[l] + b_ada[l]
        sh1, sc1, g1, sh2, sc2, g2 = jnp.split(mod[:, None, :], 6, axis=-1)
        h = layer_norm(x) * (1.0 + sc1) + sh1
        proj = h @ w_in[l]
        u, q, k, v, gs, ga = jnp.split(proj, splits, axis=-1)
        if s_re is None:
            s0_re = jnp.zeros((bsz, N_GROUPS, SSM_STATE), jnp.float32)
            s0_im = jnp.zeros((bsz, N_GROUPS, SSM_STATE), jnp.float32)
        else:
            s0_re = s_re[l]
            s0_im = s_im[l]
        y_ssm, sr, si = s5_branch(u, s0_re, s0_im, ssm_a_re[l], ssm_a_im[l], ssm_log_dt[l],
                                  ssm_b_re[l], ssm_b_im[l], ssm_c_re[l], ssm_c_im[l], ssm_d[l], w_glu[l])
        y_ssm = y_ssm @ w_up_ssm[l]
        qh = q.reshape(bsz, seq_len, N_HEADS, 2, HEAD_DIM)
        q1 = qh[..., 0, :]
        q2 = qh[..., 1, :]
        k_rows = k.reshape(bsz, seq_len, N_HEADS, 2 * HEAD_DIM)
        v_rows = v.reshape(bsz, seq_len, N_HEADS, 2 * HEAD_DIM)
        lam = (jnp.exp(jnp.sum(lam_q1[l].astype(jnp.float32) * lam_k1[l].astype(jnp.float32)))
               - jnp.exp(jnp.sum(lam_q2[l].astype(jnp.float32) * lam_k2[l].astype(jnp.float32)))
               + lambda_init)
        if cache_k is None:
            past_len = 0
            k_all = k_rows
            v_all = v_rows
        else:
            kp = cache_k[l, page_table]
            vp = cache_v[l, page_table]
            past_len = kp.shape[1] * kp.shape[2]
            kp = kp.reshape(bsz, past_len, N_HEADS, 2 * HEAD_DIM)
            vp = vp.reshape(bsz, past_len, N_HEADS, 2 * HEAD_DIM)
            k_all = jnp.concatenate([kp, k_rows.astype(kp.dtype)], axis=1)
            v_all = jnp.concatenate([vp, v_rows.astype(vp.dtype)], axis=1)
        q_pos = past_len + jnp.arange(seq_len, dtype=jnp.int32)
        k_pos = jnp.arange(past_len + seq_len, dtype=jnp.int32)
        o = diff_attention(q1, q2, q_pos, k_all[..., :HEAD_DIM], k_all[..., HEAD_DIM:], v_all, k_pos, lam)
        o = rms_norm(o, subln_g[l]) * (1.0 - lambda_init)
        y_att = o.reshape(bsz, seq_len, D_ATT).astype(x.dtype) @ w_up_att[l]
        mix = (jax.nn.sigmoid(gs) * y_ssm + jax.nn.sigmoid(ga) * y_att) @ w_o[l]
        x = layer_norm(DEEPNORM_ALPHA * x + g1 * mix, ln1_g[l], ln1_b[l])
        h2 = layer_norm(x) * (1.0 + sc2) + sh2
        ff = jnp.square(jax.nn.relu(h2 @ w_ff1[l])) @ w_ff2[l]
        x = layer_norm(DEEPNORM_ALPHA * x + g2 * ff, ln2_g[l], ln2_b[l])
        new_k.append(k_rows)
        new_v.append(v_rows)
        new_sr.append(sr)
        new_si.append(si)
    return x, jnp.stack(new_k), jnp.stack(new_v), jnp.stack(new_sr), jnp.stack(new_si)


def setup_inputs(seed: int = 0) -> dict:
    key = jax.random.key(seed)
    ks = jax.random.split(key, 40)
    f32 = jnp.float32
    L = DEPTH

    def nrm(k, shape, scale):
        return jax.random.normal(k, shape, f32) * scale

    n_pages = PAST_LEN // PAGE_SIZE
    n_used = DEC_BATCH * n_pages
    n_pool = n_used + n_used // 4
    page_table = jax.random.permutation(ks[8], n_pool)[:n_used].reshape(DEC_BATCH, n_pages).astype(jnp.int32)
    col_scale = jnp.concatenate([jnp.ones((D_SSM + 2 * D_ATT,), f32),
                                 jnp.full((D_ATT,), DEEPNORM_BETA, f32),
                                 jnp.ones((2 * D_MODEL,), f32)])
    return {
        'x_prompt': nrm(ks[0], (BATCH, SEQ, D_MODEL), 1.0),
        'x_sample': nrm(ks[1], (DEC_BATCH, DEC_SEQ, D_MODEL), 1.0),
        'c_prompt': nrm(ks[2], (BATCH, D_MODEL), 1.0),
        'c_sample': nrm(ks[3], (DEC_BATCH, D_MODEL), 1.0),
        'cache_k': nrm(ks[4], (L, n_pool, PAGE_SIZE, N_HEADS, 2 * HEAD_DIM), 1.0),
        'cache_v': nrm(ks[5], (L, n_pool, PAGE_SIZE, N_HEADS, 2 * HEAD_DIM), 1.0),
        'state_ssm_re': nrm(ks[6], (L, DEC_BATCH, N_GROUPS, SSM_STATE), 0.5),
        'state_ssm_im': nrm(ks[7], (L, DEC_BATCH, N_GROUPS, SSM_STATE), 0.5),
        'page_table': page_table,
        'w_ada': nrm(ks[9], (L, D_MODEL, 6 * D_MODEL), 0.5 * D_MODEL ** -0.5),
        'b_ada': nrm(ks[10], (L, 6 * D_MODEL), 0.02),
        'w_in': nrm(ks[11], (L, D_MODEL, N_IN), D_MODEL ** -0.5) * col_scale,
        'ssm_a_re': -0.5 + nrm(ks[12], (L, N_GROUPS, SSM_STATE), 0.01),
        'ssm_a_im': jnp.pi * jnp.arange(SSM_STATE, dtype=f32)[None, None, :] + nrm(ks[13], (L, N_GROUPS, SSM_STATE), 0.01),
        'ssm_log_dt': jax.random.uniform(ks[14], (L, N_GROUPS), f32, math.log(1e-3), math.log(1e-1)),
        'ssm_b_re': nrm(ks[15], (L, N_GROUPS, SSM_STATE, SSM_GROUP), (2 * SSM_GROUP) ** -0.5),
        'ssm_b_im': nrm(ks[16], (L, N_GROUPS, SSM_STATE, SSM_GROUP), (2 * SSM_GROUP) ** -0.5),
        'ssm_c_re': nrm(ks[17], (L, N_GROUPS, SSM_GROUP, SSM_STATE), SSM_STATE ** -0.5),
        'ssm_c_im': nrm(ks[18], (L, N_GROUPS, SSM_GROUP, SSM_STATE), SSM_STATE ** -0.5),
        'ssm_d': nrm(ks[19], (L, D_SSM), 0.5),
        'w_glu': nrm(ks[20], (L, D_SSM, D_SSM), D_SSM ** -0.5),
        'w_up_ssm': nrm(ks[21], (L, D_SSM, D_MODEL), DEEPNORM_BETA * D_SSM ** -0.5),
        'lam_q1': nrm(ks[22], (L, HEAD_DIM), 0.1),
        'lam_k1': nrm(ks[23], (L, HEAD_DIM), 0.1),
        'lam_q2': nrm(ks[24], (L, HEAD_DIM), 0.1),
        'lam_k2': nrm(ks[25], (L, HEAD_DIM), 0.1),
        'subln_g': 1.0 + nrm(ks[26], (L, 2 * HEAD_DIM), 0.02),
        'w_up_att': nrm(ks[27], (L, D_ATT, D_MODEL), DEEPNORM_BETA * D_ATT ** -0.5),
        'w_o': nrm(ks[28], (L, D_MODEL, D_MODEL), DEEPNORM_BETA * D_MODEL ** -0.5),
        'ln1_g': 1.0 + nrm(ks[29], (L, D_MODEL), 0.02),
        'ln1_b': nrm(ks[30], (L, D_MODEL), 0.02),
        'w_ff1': nrm(ks[31], (L, D_MODEL, D_FF), DEEPNORM_BETA * D_MODEL ** -0.5),
        'w_ff2': nrm(ks[32], (L, D_FF, D_MODEL), DEEPNORM_BETA * D_FF ** -0.5),
        'ln2_g': 1.0 + nrm(ks[33], (L, D_MODEL), 0.02),
        'ln2_b': nrm(ks[34], (L, D_MODEL), 0.02),
    }


def reference(x_prompt, x_sample, c_prompt, c_sample, cache_k, cache_v, state_ssm_re, state_ssm_im,
              page_table, w_ada, b_ada, w_in, ssm_a_re, ssm_a_im, ssm_log_dt, ssm_b_re, ssm_b_im,
              ssm_c_re, ssm_c_im, ssm_d, w_glu, w_up_ssm, lam_q1, lam_k1, lam_q2, lam_k2, subln_g,
              w_up_att, w_o, ln1_g, ln1_b, w_ff1, w_ff2, ln2_g, ln2_b):
    weights = (w_ada, b_ada, w_in, ssm_a_re, ssm_a_im, ssm_log_dt, ssm_b_re, ssm_b_im, ssm_c_re, ssm_c_im,
               ssm_d, w_glu, w_up_ssm, lam_q1, lam_k1, lam_q2, lam_k2, subln_g, w_up_att, w_o,
               ln1_g, ln1_b, w_ff1, w_ff2, ln2_g, ln2_b)
    y_prompt, k_prompt, v_prompt, ssm_re_prompt, ssm_im_prompt = trunk(
        x_prompt, c_prompt, None, None, None, None, None, weights)
    y_sample, k_sample, v_sample, ssm_re_sample, ssm_im_sample = trunk(
        x_sample, c_sample, state_ssm_re, state_ssm_im, cache_k, cache_v, page_table, weights)
    return (y_prompt, y_sample, k_prompt, v_prompt, ssm_re_prompt, ssm_im_prompt,
            k_sample, v_sample, ssm_re_sample, ssm_im_sample)
```

```python
import functools
import math

import jax
import jax.numpy as jnp
from jax import lax
from jax.experimental import pallas as pl
from jax.experimental.pallas import tpu as pltpu

F32 = jnp.float32
BF16 = jnp.bfloat16

SSM_GROUP = 16
SSM_STATE = 64
HEAD_DIM = 64
BLOCK_W = 256
GROUPS_PER_BLOCK = BLOCK_W // SSM_GROUP
STATE_W = GROUPS_PER_BLOCK * SSM_STATE
PROMPT_CHUNK = 16
LN_EPS = 1e-5
NEG_BIG = -1e30
VMEM_LIMIT = 56 * 1024 * 1024


def _params(*sem):
    return pltpu.CompilerParams(dimension_semantics=sem, vmem_limit_bytes=VMEM_LIMIT)


def _ln(x):
    mu = jnp.mean(x, axis=-1, keepdims=True)
    xc = x - mu
    var = jnp.mean(xc * xc, axis=-1, keepdims=True)
    return xc * lax.rsqrt(var + LN_EPS)


def _dot(a, b):
    return jnp.dot(a, b, preferred_element_type=F32)


def _dot_nt(a, b):
    return lax.dot_general(a, b, (((1,), (1,)), ((), ())), preferred_element_type=F32)


def _dot_tn(a, b):
    return lax.dot_general(a, b, (((0,), (0,)), ((), ())), preferred_element_type=F32)


def _mod_spec(mod3, tm, rows_per_batch, col, ngrid):
    d = mod3.shape[2] // 6
    if mod3.shape[1] == 1:
        per = rows_per_batch // tm
        if ngrid == 1:
            return pl.BlockSpec((None, 1, d), lambda i: (i // per, 0, col))
        return pl.BlockSpec((None, 1, d), lambda i, j: (i // per, 0, col))
    if ngrid == 1:
        return pl.BlockSpec((None, tm, d), lambda i: (0, i, col))
    return pl.BlockSpec((None, tm, d), lambda i, j: (0, i, col))


def _ada_kernel(c_ref, w_ref, b_ref, o_ref):
    c = c_ref[...]
    a = (c * jax.nn.sigmoid(c)).astype(BF16)
    o_ref[...] = _dot(a, w_ref[...].astype(BF16)) + b_ref[...]


def _ada(c, w, b, tn):
    m, d = c.shape
    n = w.shape[1]
    return pl.pallas_call(
        _ada_kernel,
        grid=(n // tn,),
        in_specs=[pl.BlockSpec((m, d), lambda j: (0, 0)),
                  pl.BlockSpec((d, tn), lambda j: (0, j)),
                  pl.BlockSpec((1, tn), lambda j: (0, j))],
        out_specs=pl.BlockSpec((m, tn), lambda j: (0, j)),
        out_shape=jax.ShapeDtypeStruct((m, n), F32),
        compiler_params=_params("arbitrary"),
        name="ada_mod",
    )(c, w, b)


def _ln_mod_kernel(x_ref, sh_ref, sc_ref, h_ref):
    h_ref[...] = (_ln(x_ref[...]) * (1.0 + sc_ref[...]) + sh_ref[...]).astype(h_ref.dtype)


def _ln_mod(x, mod3, rows_per_batch, tm):
    n, d = x.shape
    return pl.pallas_call(
        _ln_mod_kernel,
        grid=(n // tm,),
        in_specs=[pl.BlockSpec((tm, d), lambda i: (i, 0)),
                  _mod_spec(mod3, tm, rows_per_batch, 0, 1),
                  _mod_spec(mod3, tm, rows_per_batch, 1, 1)],
        out_specs=pl.BlockSpec((tm, d), lambda i: (i, 0)),
        out_shape=jax.ShapeDtypeStruct((n, d), BF16),
        compiler_params=_params("arbitrary"),
        name="ln_mod",
    )(x, mod3, mod3)


def _linear_kernel(a_ref, w_ref, o_ref):
    o_ref[...] = _dot(a_ref[...], w_ref[...]).astype(o_ref.dtype)


def _linear(a, w, col0, ncols, tm, tn, out_dtype=F32):
    n, k = a.shape
    cb = col0 // tn
    return pl.pallas_call(
        _linear_kernel,
        grid=(ncols // tn, n // tm),
        in_specs=[pl.BlockSpec((tm, k), lambda j, i: (i, 0)),
                  pl.BlockSpec((k, tn), lambda j, i: (0, cb + j))],
        out_specs=pl.BlockSpec((tm, tn), lambda j, i: (i, j)),
        out_shape=jax.ShapeDtypeStruct((n, ncols), out_dtype),
        compiler_params=_params("arbitrary", "arbitrary"),
        name="linear",
    )(a, w)


def _ssm_prep_kernel(are_ref, aim_ref, ldt_ref, bre_ref, bim_ref, cre_ref, cim_ref, d_ref,
                     pwre_ref, pwim_ref, gqre_ref, gqim_ref, kt_ref, lpre_ref, lpim_ref, *, npow):
    c = SSM_GROUP
    a_re = are_ref[...]
    a_im = aim_ref[...]
    dt = jnp.exp(ldt_ref[...])
    mag = jnp.exp(dt * a_re)
    ab_re = mag * jnp.cos(dt * a_im)
    ab_im = mag * jnp.sin(dt * a_im)
    den = a_re * a_re + a_im * a_im
    f_re = ((ab_re - 1.0) * a_re + ab_im * a_im) / den
    f_im = (ab_im * a_re - (ab_re - 1.0) * a_im) / den
    b_re = bre_ref[...]
    b_im = bim_ref[...]
    bb_re = f_re[:, None, :] * b_re - f_im[:, None, :] * b_im
    bb_im = f_re[:, None, :] * b_im + f_im[:, None, :] * b_re
    c_re = cre_ref[...]
    c_im = cim_ref[...]
    p_re = jnp.ones_like(a_re)
    p_im = jnp.zeros_like(a_re)
    for n in range(npow):
        rows = slice(n * c, (n + 1) * c)
        pr = p_re[:, None, :]
        pi = p_im[:, None, :]
        pwre_ref[:, rows, :] = pr * bb_re - pi * bb_im
        pwim_ref[:, rows, :] = pr * bb_im + pi * bb_re
        gqre_ref[:, rows, :] = c_re * pr - c_im * pi
        gqim_ref[:, rows, :] = -(c_re * pi + c_im * pr)
        lpre_ref[n] = p_re
        lpim_ref[n] = p_im
        p_re, p_im = p_re * ab_re - p_im * ab_im, p_re * ab_im + p_im * ab_re
    dn = (((2,), (2,)), ((0,), (0,)))
    kt = (lax.dot_general(gqre_ref[...], bb_re, dn, precision=lax.Precision.HIGHEST, preferred_element_type=F32)
          + lax.dot_general(gqim_ref[...], bb_im, dn, precision=lax.Precision.HIGHEST, preferred_element_type=F32))
    kt_ref[...] = kt
    eye = (lax.broadcasted_iota(jnp.int32, (c, c), 0) == lax.broadcasted_iota(jnp.int32, (c, c), 1))
    kt_ref[:, 0:c, :] = kt[:, 0:c, :] + jnp.where(eye[None], d_ref[...], 0.0)


def _ssm_prep(a_re, a_im, log_dt, b_re, b_im, c_re, c_im, d_skip, npow):
    g, p = a_re.shape
    c = SSM_GROUP
    gt = 8
    bt_re = jnp.swapaxes(b_re, 1, 2)
    bt_im = jnp.swapaxes(b_im, 1, 2)
    gp = pl.BlockSpec((gt, p), lambda i: (i, 0))
    gcp = pl.BlockSpec((gt, c, p), lambda i: (i, 0, 0))
    big = pl.BlockSpec((gt, npow * c, p), lambda i: (i, 0, 0))
    outs = pl.pallas_call(
        functools.partial(_ssm_prep_kernel, npow=npow),
        grid=(g // gt,),
        in_specs=[gp, gp, pl.BlockSpec((gt, 1), lambda i: (i, 0)), gcp, gcp, gcp, gcp,
                  pl.BlockSpec((gt, c, 1), lambda i: (i, 0, 0))],
        out_specs=[big, big, big, big,
                   pl.BlockSpec((gt, npow * c, c), lambda i: (i, 0, 0)),
                   pl.BlockSpec((npow, gt, p), lambda i: (0, i, 0)),
                   pl.BlockSpec((npow, gt, p), lambda i: (0, i, 0))],
        out_shape=[jax.ShapeDtypeStruct((g, npow * c, p), F32)] * 4
        + [jax.ShapeDtypeStruct((g, npow * c, c), F32)]
        + [jax.ShapeDtypeStruct((npow, g, p), F32)] * 2,
        compiler_params=_params("arbitrary"),
        name="ssm_prep",
    )(a_re, a_im, log_dt.reshape(g, 1), bt_re, bt_im, c_re, c_im, d_skip.reshape(g, c, 1))
    return outs


def _ssm_tables(pw_re, pw_im, gq_re, gq_im, kt, npow):
    g, _, p = pw_re.shape
    c = SSM_GROUP
    gl = GROUPS_PER_BLOCK
    gb = g // gl
    eye = jnp.eye(gl, dtype=F32)
    kt6 = kt.reshape(gb, gl, npow, c, c)
    w_t = jnp.einsum('bgncd,gh->nbgdhc', kt6, eye).reshape(npow, gb, BLOCK_W, BLOCK_W)
    pw = jnp.stack([pw_re, pw_im]).reshape(2, gb, gl, npow, c, p)
    w_p = jnp.einsum('rbgncp,gh->nbgcrhp', pw, eye).reshape(npow, gb, BLOCK_W, 2 * STATE_W)
    gq = jnp.stack([gq_re, gq_im]).reshape(2, gb, gl, npow, c, p)
    w_q = jnp.einsum('rbgncp,gh->nbrgphc', gq, eye).reshape(npow, gb, 2, STATE_W, BLOCK_W)
    return w_t.astype(BF16), w_p.astype(BF16), w_q.astype(BF16)


def _ssm_p_kernel(u_ref, w_ref, vre_ref, vim_ref, acc_ref, *, t_chunk):
    t = pl.program_id(1)

    @pl.when(t == 0)
    def _():
        acc_ref[...] = jnp.zeros_like(acc_ref)

    acc_ref[...] += _dot(u_ref[...].astype(BF16), w_ref[...])

    @pl.when(t == t_chunk - 1)
    def _():
        vre_ref[...] = acc_ref[:, :STATE_W]
        vim_ref[...] = acc_ref[:, STATE_W:]


def _ssm_p(u_flat, w_p, t_chunk):
    r = u_flat.shape[0]
    gb = w_p.shape[1]
    out = jax.ShapeDtypeStruct((r, gb * STATE_W), F32)
    return pl.pallas_call(
        functools.partial(_ssm_p_kernel, t_chunk=t_chunk),
        grid=(gb, t_chunk),
        in_specs=[pl.BlockSpec((r, BLOCK_W), lambda b, t: (0, t * gb + b)),
                  pl.BlockSpec((None, None, BLOCK_W, 2 * STATE_W), lambda b, t: (t_chunk - 1 - t, b, 0, 0))],
        out_specs=[pl.BlockSpec((r, STATE_W), lambda b, t: (0, b))] * 2,
        out_shape=[out, out],
        scratch_shapes=[pltpu.VMEM((r, 2 * STATE_W), F32)],
        compiler_params=_params("arbitrary", "arbitrary"),
        name="ssm_chunk_state",
    )(u_flat, w_p)


def _ssm_scan_kernel(vre_ref, vim_ref, s0re_ref, s0im_ref, lre_ref, lim_ref,
                     sre_ref, sim_ref, fre_ref, fim_ref, *, nb, nc):
    ar = lre_ref[...]
    ai = lim_ref[...]
    if nc == 1:
        sr = s0re_ref[...]
        si = s0im_ref[...]
        sre_ref[...] = sr
        sim_ref[...] = si
        fre_ref[...] = ar * sr - ai * si + vre_ref[...]
        fim_ref[...] = ar * si + ai * sr + vim_ref[...]
        return

    def body(k, carry):
        new = []
        for b in range(nb):
            sr, si = carry[2 * b], carry[2 * b + 1]
            row = pl.ds(b * nc + k, 1)
            sre_ref[row, :] = sr
            sim_ref[row, :] = si
            new.append(ar * sr - ai * si + vre_ref[row, :])
            new.append(ar * si + ai * sr + vim_ref[row, :])
        return tuple(new)

    init = []
    for b in range(nb):
        init += [s0re_ref[b:b + 1, :], s0im_ref[b:b + 1, :]]
    fin = lax.fori_loop(0, nc, body, tuple(init))
    for b in range(nb):
        fre_ref[b:b + 1, :] = fin[2 * b]
        fim_ref[b:b + 1, :] = fin[2 * b + 1]


def _ssm_scan(v_re, v_im, s0_re, s0_im, lam_re, lam_im, nb, nc):
    r, w = v_re.shape
    tw = STATE_W
    rows = pl.BlockSpec((r, tw), lambda j: (0, j))
    bat = pl.BlockSpec((nb, tw), lambda j: (0, j))
    one = pl.BlockSpec((1, tw), lambda j: (0, j))
    return pl.pallas_call(
        functools.partial(_ssm_scan_kernel, nb=nb, nc=nc),
        grid=(w // tw,),
        in_specs=[rows, rows, bat, bat, one, one],
        out_specs=[rows, rows, bat, bat],
        out_shape=[jax.ShapeDtypeStruct((r, w), F32)] * 2 + [jax.ShapeDtypeStruct((nb, w), F32)] * 2,
        compiler_params=_params("arbitrary"),
        name="ssm_chunk_scan",
    )(v_re, v_im, s0_re, s0_im, lam_re, lam_im)


def _ssm_y_kernel(*refs, t_chunk):
    u_refs = refs[:t_chunk]
    wt_ref, wq_ref, sre_ref, sim_ref, y_ref = refs[t_chunk:]
    t = pl.program_id(1)
    y_ref[...] = (_dot(sre_ref[...].astype(BF16), wq_ref[0]) + _dot(sim_ref[...].astype(BF16), wq_ref[1]))
    for j in range(t_chunk):
        @pl.when(j <= t)
        def _(j=j):
            y_ref[...] += _dot(u_refs[j][...].astype(BF16), wt_ref[t - j])


def _ssm_y(u_flat, w_t, w_q, s_re, s_im, t_chunk):
    r = u_flat.shape[0]
    gb = w_t.shape[1]
    u_specs = [pl.BlockSpec((r, BLOCK_W), lambda b, t, j=j: (0, j * gb + b)) for j in range(t_chunk)]
    return pl.pallas_call(
        functools.partial(_ssm_y_kernel, t_chunk=t_chunk),
        grid=(gb, t_chunk),
        in_specs=u_specs + [
            pl.BlockSpec((t_chunk, None, BLOCK_W, BLOCK_W), lambda b, t: (0, b, 0, 0)),
            pl.BlockSpec((None, None, 2, STATE_W, BLOCK_W), lambda b, t: (t + 1, b, 0, 0, 0)),
            pl.BlockSpec((r, STATE_W), lambda b, t: (0, b)),
            pl.BlockSpec((r, STATE_W), lambda b, t: (0, b))],
        out_specs=pl.BlockSpec((r, BLOCK_W), lambda b, t: (0, t * gb + b)),
        out_shape=jax.ShapeDtypeStruct(u_flat.shape, F32),
        compiler_params=_params("arbitrary", "arbitrary"),
        name="ssm_chunk_out",
    )(*([u_flat] * t_chunk), w_t, w_q, s_re, s_im)


def _lambda(lq1_ref, lk1_ref, lq2_ref, lk2_ref, lambda_init):
    return (jnp.exp(jnp.sum(lq1_ref[...] * lk1_ref[...], axis=-1, keepdims=True))
            - jnp.exp(jnp.sum(lq2_ref[...] * lk2_ref[...], axis=-1, keepdims=True)) + lambda_init)


def _sub_norm(o, g, lambda_init):
    return o * lax.rsqrt(jnp.mean(o * o, axis=-1, keepdims=True) + LN_EPS) * g * (1.0 - lambda_init)


def _attn_kernel(slope_ref, q_ref, k_ref, v_ref, lq1_ref, lk1_ref, lq2_ref, lk2_ref, g_ref, o_ref,
                 m1_ref, l1_ref, a1_ref, m2_ref, l2_ref, a2_ref, *, tq, lambda_init):
    h = pl.program_id(1)
    qi = pl.program_id(2)
    ki = pl.program_id(3)
    hd = HEAD_DIM

    @pl.when(ki == 0)
    def _():
        for m_ref, l_ref, a_ref in ((m1_ref, l1_ref, a1_ref), (m2_ref, l2_ref, a2_ref)):
            m_ref[...] = jnp.full_like(m_ref, NEG_BIG)
            l_ref[...] = jnp.zeros_like(l_ref)
            a_ref[...] = jnp.zeros_like(a_ref)

    @pl.when(ki <= qi)
    def _():
        q = q_ref[...] * (hd ** -0.5)
        k = k_ref[...].astype(BF16)
        v = v_ref[...].astype(BF16)
        dist = ((qi * tq + lax.broadcasted_iota(jnp.int32, (tq, tq), 0))
                - (ki * tq + lax.broadcasted_iota(jnp.int32, (tq, tq), 1))).astype(F32)
        bias = jnp.where(dist >= 0, -slope_ref[h] * dist, -jnp.inf)
        for mp, (m_ref, l_ref, a_ref) in enumerate(((m1_ref, l1_ref, a1_ref), (m2_ref, l2_ref, a2_ref))):
            s = _dot_nt(q[:, mp * hd:(mp + 1) * hd].astype(BF16), k[:, mp * hd:(mp + 1) * hd]) + bias
            m_old = m_ref[...]
            m_new = jnp.maximum(m_old, jnp.max(s, axis=-1, keepdims=True))
            alpha = jnp.exp(m_old - m_new)
            p = jnp.exp(s - m_new)
            l_ref[...] = alpha * l_ref[...] + jnp.sum(p, axis=-1, keepdims=True)
            a_ref[...] = alpha * a_ref[...] + _dot(p.astype(BF16), v)
            m_ref[...] = m_new

    @pl.when(ki == qi)
    def _():
        lam = _lambda(lq1_ref, lk1_ref, lq2_ref, lk2_ref, lambda_init)
        o = a1_ref[...] / l1_ref[...] - lam * (a2_ref[...] / l2_ref[...])
        o_ref[...] = _sub_norm(o, g_ref[...], lambda_init).astype(o_ref.dtype)


def _attention(q, k, v, slopes, lam_vecs, subln_g, nbatch, seq, lambda_init, tq):
    n, da = q.shape
    nh = da // (2 * HEAD_DIM)
    nq = seq // tq
    hw = 2 * HEAD_DIM
    vec = pl.BlockSpec((1, HEAD_DIM), lambda b, h, i, j: (0, 0))
    return pl.pallas_call(
        functools.partial(_attn_kernel, tq=tq, lambda_init=lambda_init),
        grid=(nbatch, nh, nq, nq),
        in_specs=[pl.BlockSpec(memory_space=pltpu.SMEM),
                  pl.BlockSpec((tq, hw), lambda b, h, i, j: (b * nq + i, h)),
                  pl.BlockSpec((tq, hw), lambda b, h, i, j: (b * nq + jnp.minimum(i, j), h)),
                  pl.BlockSpec((tq, hw), lambda b, h, i, j: (b * nq + jnp.minimum(i, j), h)),
                  vec, vec, vec, vec,
                  pl.BlockSpec((1, hw), lambda b, h, i, j: (0, 0))],
        out_specs=pl.BlockSpec((tq, hw), lambda b, h, i, j: (b * nq + i, h)),
        out_shape=jax.ShapeDtypeStruct((n, da), BF16),
        scratch_shapes=[pltpu.VMEM((tq, 1), F32), pltpu.VMEM((tq, 1), F32), pltpu.VMEM((tq, hw), F32),
                        pltpu.VMEM((tq, 1), F32), pltpu.VMEM((tq, 1), F32), pltpu.VMEM((tq, hw), F32)],
        compiler_params=_params("arbitrary", "arbitrary", "arbitrary", "arbitrary"),
        name="prompt_attention",
    )(slopes, q, k, v, *lam_vecs, subln_g)


def _dec_attn_kernel(pt_ref, q_ref, kn_ref, vn_ref, *rest, n_pages, page, tdec, lambda_init):
    del pt_ref
    k_refs = rest[:n_pages] + (kn_ref,)
    v_refs = rest[n_pages:2 * n_pages] + (vn_ref,)
    lq1_ref, lk1_ref, lq2_ref, lk2_ref, g_ref, o_ref = rest[2 * n_pages:]
    hd = HEAD_DIM
    da = q_ref.shape[-1]
    nh = da // (2 * hd)
    ncol = 2 * nh * tdec
    past = n_pages * page
    q = q_ref[...] * (hd ** -0.5)
    rid = lax.broadcasted_iota(jnp.int32, (ncol, da), 0)
    cid = lax.broadcasted_iota(jnp.int32, (ncol, da), 1)
    qt = jnp.zeros((ncol, da), F32)
    for i in range(tdec):
        qt = jnp.where(rid % tdec == i, jnp.broadcast_to(q[i:i + 1, :], (ncol, da)), qt)
    qt = jnp.where(rid // tdec == cid // hd, qt, 0.0).astype(BF16)
    col = lax.broadcasted_iota(jnp.int32, (1, ncol), 1)
    qidx = col % tdec
    slope = jnp.exp2(((col // (2 * tdec)) + 1).astype(F32) * (-8.0 / nh))
    row = lax.broadcasted_iota(jnp.int32, (page, 1), 0)
    s_list = []
    for j in range(n_pages + 1):
        s = _dot_nt(k_refs[j][...].astype(BF16), qt)
        if j < n_pages:
            s = s - slope * (past - j * page + qidx - row).astype(F32)
        else:
            s = jnp.where(qidx >= row, s - slope * (qidx - row).astype(F32), -jnp.inf)
        s_list.append(s)
    m = jnp.max(s_list[0], axis=0, keepdims=True)
    for s in s_list[1:]:
        m = jnp.maximum(m, jnp.max(s, axis=0, keepdims=True))
    p_list = [jnp.exp(s - m) for s in s_list]
    l = jnp.sum(p_list[0], axis=0, keepdims=True)
    for p in p_list[1:]:
        l = l + jnp.sum(p, axis=0, keepdims=True)
    inv = 1.0 / l
    acc = jnp.zeros((ncol, da), F32)
    for j in range(n_pages + 1):
        acc = acc + _dot_tn((p_list[j] * inv).astype(BF16), v_refs[j][...].astype(BF16))
    lam = _lambda(lq1_ref, lk1_ref, lq2_ref, lk2_ref, lambda_init)
    g = g_ref[...]
    for h in range(nh):
        blk = acc[h * 2 * tdec:(h + 1) * 2 * tdec, h * 2 * hd:(h + 1) * 2 * hd]
        o = blk[:tdec] - lam * blk[tdec:]
        o_ref[:, h * 2 * hd:(h + 1) * 2 * hd] = _sub_norm(o, g, lambda_init).astype(o_ref.dtype)


def _decode_attention(q3, k3, v3, cache_k, cache_v, page_table, lam_vecs, subln_g, lambda_init):
    nb, tdec, da = q3.shape
    n_pages = page_table.shape[1]
    page = cache_k.shape[1]
    kn = jnp.pad(k3, ((0, 0), (0, page - tdec), (0, 0)))
    vn = jnp.pad(v3, ((0, 0), (0, page - tdec), (0, 0)))
    new = pl.BlockSpec((None, tdec, da), lambda b, pt: (b, 0, 0))
    newp = pl.BlockSpec((None, page, da), lambda b, pt: (b, 0, 0))
    pages = [pl.BlockSpec((None, page, da), lambda b, pt, j=j: (pt[b, j], 0, 0)) for j in range(n_pages)]
    vec = pl.BlockSpec((1, HEAD_DIM), lambda b, pt: (0, 0))
    return pl.pallas_call(
        functools.partial(_dec_attn_kernel, n_pages=n_pages, page=page, tdec=tdec, lambda_init=lambda_init),
        grid_spec=pltpu.PrefetchScalarGridSpec(
            num_scalar_prefetch=1,
            grid=(nb,),
            in_specs=[new, newp, newp] + pages + pages + [vec, vec, vec, vec,
                                                          pl.BlockSpec((1, 2 * HEAD_DIM), lambda b, pt: (0, 0))],
            out_specs=new),
        out_shape=jax.ShapeDtypeStruct((nb, tdec, da), BF16),
        compiler_params=_params("arbitrary"),
        name="decode_attention",
    )(page_table, q3, kn, vn, *([cache_k] * n_pages), *([cache_v] * n_pages), *lam_vecs, subln_g)


def _mix_in_kernel(y_ref, o_ref, gs_ref, ga_ref, wglu_ref, wus_ref, wua_ref, out_ref):
    z = jax.nn.gelu(y_ref[...])
    zz = z * jax.nn.sigmoid(_dot(z.astype(BF16), wglu_ref[...]))
    y_ssm = _dot(zz.astype(BF16), wus_ref[...])
    y_att = _dot(o_ref[...], wua_ref[...])
    out_ref[...] = (jax.nn.sigmoid(gs_ref[...]) * y_ssm + jax.nn.sigmoid(ga_ref[...]) * y_att).astype(out_ref.dtype)


def _mix_in(y, o, gates, w_glu, w_up_ssm, w_up_att, tm):
    n, ds = y.shape
    da = o.shape[1]
    d = w_up_ssm.shape[1]
    full = lambda a: pl.BlockSpec(a.shape, lambda i: (0, 0))
    return pl.pallas_call(
        _mix_in_kernel,
        grid=(n // tm,),
        in_specs=[pl.BlockSpec((tm, ds), lambda i: (i, 0)),
                  pl.BlockSpec((tm, da), lambda i: (i, 0)),
                  pl.BlockSpec((tm, d), lambda i: (i, 0)),
                  pl.BlockSpec((tm, d), lambda i: (i, 1)),
                  full(w_glu), full(w_up_ssm), full(w_up_att)],
        out_specs=pl.BlockSpec((tm, d), lambda i: (i, 0)),
        out_shape=jax.ShapeDtypeStruct((n, d), BF16),
        compiler_params=_params("arbitrary"),
        name="branch_mix",
    )(y, o, gates, gates, w_glu, w_up_ssm, w_up_att)


def _out_proj_kernel(mix_ref, x_ref, g1_ref, wo_ref, lg_ref, lb_ref, out_ref, *, alpha):
    r = alpha * x_ref[...] + g1_ref[...] * _dot(mix_ref[...], wo_ref[...])
    out_ref[...] = _ln(r) * lg_ref[...] + lb_ref[...]


def _out_proj(mix, x, mod3, rows_per_batch, w_o, ln_g, ln_b, alpha, tm):
    n, d = x.shape
    row = pl.BlockSpec((tm, d), lambda i: (i, 0))
    vec = pl.BlockSpec((1, d), lambda i: (0, 0))
    return pl.pallas_call(
        functools.partial(_out_proj_kernel, alpha=alpha),
        grid=(n // tm,),
        in_specs=[row, row, _mod_spec(mod3, tm, rows_per_batch, 2, 1),
                  pl.BlockSpec(w_o.shape, lambda i: (0, 0)), vec, vec],
        out_specs=row,
        out_shape=jax.ShapeDtypeStruct((n, d), F32),
        compiler_params=_params("arbitrary"),
        name="out_proj_ln1",
    )(mix, x, mod3, w_o, ln_g, ln_b)


def _ffn_kernel(x_ref, sh_ref, sc_ref, g2_ref, w1_ref, w2_ref, lg_ref, lb_ref, out_ref, h_ref, acc_ref,
                *, alpha, nf):
    f = pl.program_id(1)

    @pl.when(f == 0)
    def _():
        h_ref[...] = (_ln(x_ref[...]) * (1.0 + sc_ref[...]) + sh_ref[...]).astype(h_ref.dtype)
        acc_ref[...] = jnp.zeros_like(acc_ref)

    a = jnp.maximum(_dot(h_ref[...], w1_ref[...]), 0.0)
    acc_ref[...] += _dot((a * a).astype(BF16), w2_ref[...])

    @pl.when(f == nf - 1)
    def _():
        r = alpha * x_ref[...] + g2_ref[...] * acc_ref[...]
        out_ref[...] = _ln(r) * lg_ref[...] + lb_ref[...]


def _ffn(x, mod3, rows_per_batch, w1, w2, ln_g, ln_b, alpha, tm, tf):
    n, d = x.shape
    dff = w1.shape[1]
    nf = dff // tf
    row = pl.BlockSpec((tm, d), lambda i, f: (i, 0))
    vec = pl.BlockSpec((1, d), lambda i, f: (0, 0))
    return pl.pallas_call(
        functools.partial(_ffn_kernel, alpha=alpha, nf=nf),
        grid=(n // tm, nf),
        in_specs=[row,
                  _mod_spec(mod3, tm, rows_per_batch, 3, 2),
                  _mod_spec(mod3, tm, rows_per_batch, 4, 2),
                  _mod_spec(mod3, tm, rows_per_batch, 5, 2),
                  pl.BlockSpec((d, tf), lambda i, f: (0, f)),
                  pl.BlockSpec((tf, d), lambda i, f: (f, 0)),
                  vec, vec],
        out_specs=row,
        out_shape=jax.ShapeDtypeStruct((n, d), F32),
        scratch_shapes=[pltpu.VMEM((tm, d), BF16), pltpu.VMEM((tm, d), F32)],
        compiler_params=_params("arbitrary", "arbitrary"),
        name="ffn_ln2",
    )(x, mod3, mod3, mod3, w1, w2, ln_g, ln_b)


def _trunk_layer(x3, mod3, s0, cache, page_table, wts, tabs, lambda_init, alpha):
    nbatch, seq, d = x3.shape
    n = nbatch * seq
    x = x3.reshape(n, d)
    ds = wts["w_glu"].shape[0]
    da = wts["w_up_att"].shape[0]
    tm = min(512, seq) if mod3.shape[1] == 1 else min(512, n)
    tn = min(1024, ds)
    h = _ln_mod(x, mod3, seq, tm)
    w_in = wts["w_in"]
    u = _linear(h, w_in, 0, ds, tm, tn)
    q = _linear(h, w_in, ds, da, tm, tn)
    k = _linear(h, w_in, ds + da, da, tm, tn)
    v = _linear(h, w_in, ds + 2 * da, da, tm, tn)
    gates = _linear(h, w_in, ds + 3 * da, 2 * d, tm, tn)

    w_t, w_p, w_q, lp_re, lp_im = tabs
    t_chunk = PROMPT_CHUNK if seq % PROMPT_CHUNK == 0 else seq
    nc = seq // t_chunk
    u_flat = u.reshape(n // t_chunk, t_chunk * ds)
    v_re, v_im = _ssm_p(u_flat, w_p, t_chunk)
    gp = lp_re.shape[1] * lp_re.shape[2]
    s_re, s_im, f_re, f_im = _ssm_scan(v_re, v_im, s0[0], s0[1], lp_re[t_chunk].reshape(1, gp),
                                       lp_im[t_chunk].reshape(1, gp), nbatch, nc)
    y = _ssm_y(u_flat, w_t, w_q, s_re, s_im, t_chunk).reshape(n, ds)

    lam_vecs = wts["lam_vecs"]
    if cache is None:
        nh = da // (2 * HEAD_DIM)
        slopes = jnp.asarray([2.0 ** (-8.0 * (i + 1) / nh) for i in range(nh)], F32)
        o = _attention(q, k, v, slopes, lam_vecs, wts["subln_g"], nbatch, seq, lambda_init, min(512, seq))
    else:
        o = _decode_attention(q.reshape(nbatch, seq, da), k.reshape(nbatch, seq, da), v.reshape(nbatch, seq, da),
                              cache[0], cache[1], page_table, lam_vecs, wts["subln_g"], lambda_init)
        o = o.reshape(n, da)

    mix = _mix_in(y, o, gates, wts["w_glu"], wts["w_up_ssm"], wts["w_up_att"], min(256, tm))
    x1 = _out_proj(mix, x, mod3, seq, wts["w_o"], wts["ln1_g"], wts["ln1_b"], alpha, tm)
    x2 = _ffn(x1, mod3, seq, wts["w_ff1"], wts["w_ff2"], wts["ln2_g"], wts["ln2_b"], alpha, tm,
              min(1024, wts["w_ff1"].shape[1]))
    nh = da // (2 * HEAD_DIM)
    return (x2.reshape(nbatch, seq, d), k.reshape(nbatch, seq, nh, 2 * HEAD_DIM),
            v.reshape(nbatch, seq, nh, 2 * HEAD_DIM), f_re, f_im)


def kernel(x_prompt, x_sample, c_prompt, c_sample, cache_k, cache_v, state_ssm_re, state_ssm_im, page_table,
           w_ada, b_ada, w_in, ssm_a_re, ssm_a_im, ssm_log_dt, ssm_b_re, ssm_b_im, ssm_c_re, ssm_c_im, ssm_d,
           w_glu, w_up_ssm, lam_q1, lam_k1, lam_q2, lam_k2, subln_g, w_up_att, w_o, ln1_g, ln1_b, w_ff1, w_ff2,
           ln2_g, ln2_b):
    depth = w_in.shape[0]
    bp, seq_p, d = x_prompt.shape
    bs, seq_s, _ = x_sample.shape
    g, p = ssm_a_re.shape[1:]
    alpha = (2 * depth) ** 0.25
    npow = PROMPT_CHUNK + 1
    xp, xs = x_prompt, x_sample
    outs_p, outs_s = [], []
    c_rows = jnp.concatenate([c_prompt, jnp.repeat(c_sample, seq_s, axis=0)], axis=0)
    n_rows = c_rows.shape[0]
    c_rows = jnp.pad(c_rows, ((0, (-n_rows) % 8), (0, 0)))
    for l in range(depth):
        lambda_init = 0.8 - 0.6 * math.exp(-0.3 * l)
        mod = _ada(c_rows, w_ada[l], b_ada[l].reshape(1, 6 * d), min(1024, d))
        mod_p = mod[:bp].reshape(bp, 1, 6 * d)
        mod_s = mod[bp:bp + bs * seq_s].reshape(1, bs * seq_s, 6 * d)
        wts = {
            "w_in": w_in[l].astype(BF16), "w_glu": w_glu[l].astype(BF16),
            "w_up_ssm": w_up_ssm[l].astype(BF16), "w_up_att": w_up_att[l].astype(BF16),
            "w_o": w_o[l].astype(BF16), "w_ff1": w_ff1[l].astype(BF16), "w_ff2": w_ff2[l].astype(BF16),
            "ln1_g": ln1_g[l].reshape(1, d), "ln1_b": ln1_b[l].reshape(1, d),
            "ln2_g": ln2_g[l].reshape(1, d), "ln2_b": ln2_b[l].reshape(1, d),
            "subln_g": subln_g[l].reshape(1, 2 * HEAD_DIM),
            "lam_vecs": tuple(a[l].reshape(1, HEAD_DIM) for a in (lam_q1, lam_k1, lam_q2, lam_k2)),
        }
        pw_re, pw_im, gq_re, gq_im, kt, lp_re, lp_im = _ssm_prep(
            ssm_a_re[l], ssm_a_im[l], ssm_log_dt[l], ssm_b_re[l], ssm_b_im[l], ssm_c_re[l], ssm_c_im[l],
            ssm_d[l], npow)
        tabs = _ssm_tables(pw_re, pw_im, gq_re, gq_im, kt, npow) + (lp_re, lp_im)
        zeros = jnp.zeros((bp, g * p), F32)
        res_p = _trunk_layer(xp, mod_p, (zeros, zeros), None, None, wts, tabs, lambda_init, alpha)
        n_pool, page = cache_k.shape[1:3]
        cache = (cache_k[l].reshape(n_pool, page, -1), cache_v[l].reshape(n_pool, page, -1))
        s0 = (state_ssm_re[l].reshape(bs, g * p), state_ssm_im[l].reshape(bs, g * p))
        res_s = _trunk_layer(xs, mod_s, s0, cache, page_table, wts, tabs, lambda_init, alpha)
        xp, xs = res_p[0], res_s[0]
        outs_p.append(res_p[1:])
        outs_s.append(res_s[1:])

    def stack(outs, i, shape=None):
        a = jnp.stack([o[i] for o in outs])
        return a if shape is None else a.reshape((depth,) + shape)

    return (xp, xs,
            stack(outs_p, 0), stack(outs_p, 1), stack(outs_p, 2, (bp, g, p)), stack(outs_p, 3, (bp, g, p)),
            stack(outs_s, 0), stack(outs_s, 1), stack(outs_s, 2, (bs, g, p)), stack(outs_s, 3, (bs, g, p)))
```

```python
import functools
import math

import jax
import jax.numpy as jnp
from jax import lax
from jax.experimental import pallas as pl
from jax.experimental.pallas import tpu as pltpu

F32 = jnp.float32
BF16 = jnp.bfloat16

SSM_GROUP = 16
SSM_STATE = 64
HEAD_DIM = 64
LANES = 128
BLOCK_W = 256
GROUPS_PER_BLOCK = BLOCK_W // SSM_GROUP
STATE_W = GROUPS_PER_BLOCK * SSM_STATE
PROMPT_CHUNK = 16
LN_EPS = 1e-5
NEG_BIG = -1e30
LOG2E = 1.4426950408889634
VMEM_LIMIT = 56 * 1024 * 1024


def _params(*sem):
    return pltpu.CompilerParams(dimension_semantics=sem, vmem_limit_bytes=VMEM_LIMIT)


def _ln(x):
    mu = jnp.mean(x, axis=-1, keepdims=True)
    xc = x - mu
    var = jnp.mean(xc * xc, axis=-1, keepdims=True)
    return xc * lax.rsqrt(var + LN_EPS)


def _dot(a, b):
    return jnp.dot(a, b, preferred_element_type=F32)


def _dot_nt(a, b):
    return lax.dot_general(a, b, (((1,), (1,)), ((), ())), preferred_element_type=F32)


def _dot_tn(a, b):
    return lax.dot_general(a, b, (((0,), (0,)), ((), ())), preferred_element_type=F32)


def _mod_spec(mod3, tm, rows_per_batch, col, ngrid):
    d = mod3.shape[2] // 6
    if mod3.shape[1] == 1:
        per = rows_per_batch // tm
        if ngrid == 1:
            return pl.BlockSpec((None, 1, d), lambda i: (i // per, 0, col))
        return pl.BlockSpec((None, 1, d), lambda i, j: (i // per, 0, col))
    if ngrid == 1:
        return pl.BlockSpec((None, tm, d), lambda i: (0, i, col))
    return pl.BlockSpec((None, tm, d), lambda i, j: (0, i, col))


def _ada_kernel(c_ref, w_ref, b_ref, o_ref):
    c = c_ref[...]
    a = (c * jax.nn.sigmoid(c)).astype(BF16)
    o_ref[...] = _dot(a, w_ref[...].astype(BF16)) + b_ref[...]


def _ada(c, w, b, tn):
    m, d = c.shape
    n = w.shape[1]
    return pl.pallas_call(
        _ada_kernel,
        grid=(n // tn,),
        in_specs=[pl.BlockSpec((m, d), lambda j: (0, 0)),
                  pl.BlockSpec((d, tn), lambda j: (0, j)),
                  pl.BlockSpec((1, tn), lambda j: (0, j))],
        out_specs=pl.BlockSpec((m, tn), lambda j: (0, j)),
        out_shape=jax.ShapeDtypeStruct((m, n), F32),
        compiler_params=_params("arbitrary"),
        name="ada_mod",
    )(c, w, b)


def _ln_mod_kernel(x_ref, sh_ref, sc_ref, h_ref):
    h_ref[...] = (_ln(x_ref[...]) * (1.0 + sc_ref[...]) + sh_ref[...]).astype(h_ref.dtype)


def _ln_mod(x, mod3, rows_per_batch, tm):
    n, d = x.shape
    return pl.pallas_call(
        _ln_mod_kernel,
        grid=(n // tm,),
        in_specs=[pl.BlockSpec((tm, d), lambda i: (i, 0)),
                  _mod_spec(mod3, tm, rows_per_batch, 0, 1),
                  _mod_spec(mod3, tm, rows_per_batch, 1, 1)],
        out_specs=pl.BlockSpec((tm, d), lambda i: (i, 0)),
        out_shape=jax.ShapeDtypeStruct((n, d), BF16),
        compiler_params=_params("arbitrary"),
        name="ln_mod",
    )(x, mod3, mod3)


def _linear_kernel(a_ref, w_ref, o_ref):
    o_ref[...] = _dot(a_ref[...], w_ref[...]).astype(o_ref.dtype)


def _linear(a, w, col0, ncols, tm, tn, out_dtype=F32):
    n, k = a.shape
    cb = col0 // tn
    return pl.pallas_call(
        _linear_kernel,
        grid=(ncols // tn, n // tm),
        in_specs=[pl.BlockSpec((tm, k), lambda j, i: (i, 0)),
                  pl.BlockSpec((k, tn), lambda j, i: (0, cb + j))],
        out_specs=pl.BlockSpec((tm, tn), lambda j, i: (i, j)),
        out_shape=jax.ShapeDtypeStruct((n, ncols), out_dtype),
        compiler_params=_params("arbitrary", "arbitrary"),
        name="linear",
    )(a, w)


def _ssm_prep_kernel(are_ref, aim_ref, ldt_ref, bre_ref, bim_ref, cre_ref, cim_ref, d_ref,
                     pd_ref, qre_ref, qim_ref, kt_ref, lpre_ref, lpim_ref, *, npow, gt):
    c = SSM_GROUP
    p = SSM_STATE
    a_re = are_ref[...]
    a_im = aim_ref[...]
    dt = jnp.exp(ldt_ref[...])
    mag = jnp.exp(dt * a_re)
    ab_re = mag * jnp.cos(dt * a_im)
    ab_im = mag * jnp.sin(dt * a_im)
    den = a_re * a_re + a_im * a_im
    f_re = ((ab_re - 1.0) * a_re + ab_im * a_im) / den
    f_im = (ab_im * a_re - (ab_re - 1.0) * a_im) / den
    b_re = bre_ref[...]
    b_im = bim_ref[...]
    bb_re = f_re[:, None, :] * b_re - f_im[:, None, :] * b_im
    bb_im = f_re[:, None, :] * b_im + f_im[:, None, :] * b_re
    c_re = cre_ref[...]
    c_im = cim_ref[...]
    low = lax.broadcasted_iota(jnp.int32, (1, 1, 2 * p), 2) < p
    p_re = jnp.ones_like(a_re)
    p_im = jnp.zeros_like(a_re)
    g_re_all, g_im_all = [], []
    for n in range(npow):
        pr = p_re[:, None, :]
        pi = p_im[:, None, :]
        pw_re = pr * bb_re - pi * bb_im
        pw_im = pr * bb_im + pi * bb_re
        pd_ref[n] = jnp.concatenate([pw_re, pw_im], axis=-1).reshape(gt * c, 4 * p)
        g_re = c_re * pr - c_im * pi
        g_im = -(c_re * pi + c_im * pr)
        g_re_all.append(g_re)
        g_im_all.append(g_im)
        lpre_ref[n] = p_re
        lpim_ref[n] = p_im
        p_re, p_im = p_re * ab_re - p_im * ab_im, p_re * ab_im + p_im * ab_re
    g_re_all = jnp.concatenate(g_re_all, axis=1)
    g_im_all = jnp.concatenate(g_im_all, axis=1)
    for gg in range(0, gt, 2):
        sl = slice(gg // 2 * 2 * p, (gg // 2 + 1) * 2 * p)
        qre_ref[:, sl] = jnp.where(low[0], g_re_all[gg], g_re_all[gg + 1])
        qim_ref[:, sl] = jnp.where(low[0], g_im_all[gg], g_im_all[gg + 1])
    dn = (((2,), (2,)), ((0,), (0,)))
    kt = (lax.dot_general(g_re_all, jnp.where(low, bb_re, 0.0), dn, precision=lax.Precision.HIGHEST,
                          preferred_element_type=F32)
          + lax.dot_general(g_im_all, jnp.where(low, bb_im, 0.0), dn, precision=lax.Precision.HIGHEST,
                            preferred_element_type=F32))
    kt_ref[...] = kt
    eye = (lax.broadcasted_iota(jnp.int32, (c, c), 0) == lax.broadcasted_iota(jnp.int32, (c, c), 1))
    kt_ref[:, 0:c, :] = kt[:, 0:c, :] + jnp.where(eye[None], d_ref[...], 0.0)


def _ssm_prep(a_re, a_im, log_dt, b_re, b_im, c_re, c_im, d_skip, npow):
    g, p = a_re.shape
    c = SSM_GROUP
    gt = 8
    dup = lambda a: jnp.concatenate([a, a], axis=-1)
    gp = pl.BlockSpec((gt, 2 * p), lambda i: (i, 0))
    gcp = pl.BlockSpec((gt, c, 2 * p), lambda i: (i, 0, 0))
    qspec = pl.BlockSpec((npow * c, gt * p), lambda i: (0, i))
    lspec = pl.BlockSpec((npow, gt, 2 * p), lambda i: (0, i, 0))
    return pl.pallas_call(
        functools.partial(_ssm_prep_kernel, npow=npow, gt=gt),
        grid=(g // gt,),
        in_specs=[gp, gp, pl.BlockSpec((gt, 1), lambda i: (i, 0)), gcp, gcp, gcp, gcp,
                  pl.BlockSpec((gt, c, 1), lambda i: (i, 0, 0))],
        out_specs=[pl.BlockSpec((npow, gt * c, 4 * p), lambda i: (0, i, 0)), qspec, qspec,
                   pl.BlockSpec((gt, npow * c, c), lambda i: (i, 0, 0)), lspec, lspec],
        out_shape=[jax.ShapeDtypeStruct((npow, g * c, 4 * p), F32),
                   jax.ShapeDtypeStruct((npow * c, g * p), F32),
                   jax.ShapeDtypeStruct((npow * c, g * p), F32),
                   jax.ShapeDtypeStruct((g, npow * c, c), F32),
                   jax.ShapeDtypeStruct((npow, g, 2 * p), F32),
                   jax.ShapeDtypeStruct((npow, g, 2 * p), F32)],
        compiler_params=_params("arbitrary"),
        name="ssm_prep",
    )(dup(a_re), dup(a_im), log_dt.reshape(g, 1), dup(jnp.swapaxes(b_re, 1, 2)), dup(jnp.swapaxes(b_im, 1, 2)),
      dup(c_re), dup(c_im), d_skip.reshape(g, c, 1))


def _toeplitz_tiles(kt, npow, t_chunk):
    g = kt.shape[0]
    c = SSM_GROUP
    gl = GROUPS_PER_BLOCK
    gb = g // gl
    eye = jnp.eye(gl, dtype=F32)
    kt6 = kt.reshape(gb, gl, npow, c, c)[:, :, :t_chunk]
    w_t = jnp.einsum('bgncd,gh->bngdhc', kt6, eye).reshape(gb, t_chunk, BLOCK_W, BLOCK_W)
    w_t = jnp.concatenate([w_t[:, ::-1], jnp.zeros((gb, t_chunk - 1, BLOCK_W, BLOCK_W), F32)], axis=1)
    return w_t.reshape(gb, (2 * t_chunk - 1) * BLOCK_W, BLOCK_W).astype(BF16)


def _group_mask(rows, cols, row_div, col_div):
    return (lax.broadcasted_iota(jnp.int32, (rows, cols), 0) // row_div
            == lax.broadcasted_iota(jnp.int32, (rows, cols), 1) // col_div)


def _ssm_p_kernel(u_ref, pd_ref, vre_ref, vim_ref, acc_ref, *, t_chunk):
    t = pl.program_id(1)

    @pl.when(t == 0)
    def _():
        acc_ref[...] = jnp.zeros_like(acc_ref)

    blk = pd_ref[...]
    mask = _group_mask(BLOCK_W, STATE_W, SSM_GROUP, SSM_STATE)
    reps = STATE_W // LANES
    w_re = jnp.where(mask, jnp.concatenate([blk[:, :LANES]] * reps, axis=1), 0.0)
    w_im = jnp.where(mask, jnp.concatenate([blk[:, LANES:]] * reps, axis=1), 0.0)
    w = jnp.concatenate([w_re, w_im], axis=1).astype(BF16)
    acc_ref[...] += _dot(u_ref[...].astype(BF16), w)

    @pl.when(t == t_chunk - 1)
    def _():
        vre_ref[...] = acc_ref[:, :STATE_W]
        vim_ref[...] = acc_ref[:, STATE_W:]


def _ssm_p(u_flat, pd, t_chunk):
    r = u_flat.shape[0]
    gb = pd.shape[1] // BLOCK_W
    out = jax.ShapeDtypeStruct((r, gb * STATE_W), F32)
    return pl.pallas_call(
        functools.partial(_ssm_p_kernel, t_chunk=t_chunk),
        grid=(gb, t_chunk),
        in_specs=[pl.BlockSpec((r, BLOCK_W), lambda b, t: (0, t * gb + b)),
                  pl.BlockSpec((None, BLOCK_W, 2 * LANES), lambda b, t: (t_chunk - 1 - t, b, 0))],
        out_specs=[pl.BlockSpec((r, STATE_W), lambda b, t: (0, b))] * 2,
        out_shape=[out, out],
        scratch_shapes=[pltpu.VMEM((r, 2 * STATE_W), F32)],
        compiler_params=_params("arbitrary", "arbitrary"),
        name="ssm_chunk_state",
    )(u_flat, pd)


def _ssm_scan_kernel(vre_ref, vim_ref, s0re_ref, s0im_ref, lre_ref, lim_ref,
                     sre_ref, sim_ref, fre_ref, fim_ref, *, nb, nc):
    ar = lre_ref[...]
    ai = lim_ref[...]
    if nc == 1:
        sr = s0re_ref[...]
        si = s0im_ref[...]
        sre_ref[...] = sr
        sim_ref[...] = si
        fre_ref[...] = ar * sr - ai * si + vre_ref[...]
        fim_ref[...] = ar * si + ai * sr + vim_ref[...]
        return

    def body(k, carry):
        new = []
        for b in range(nb):
            sr, si = carry[2 * b], carry[2 * b + 1]
            row = pl.ds(b * nc + k, 1)
            sre_ref[row, :] = sr
            sim_ref[row, :] = si
            new.append(ar * sr - ai * si + vre_ref[row, :])
            new.append(ar * si + ai * sr + vim_ref[row, :])
        return tuple(new)

    init = []
    for b in range(nb):
        init += [s0re_ref[b:b + 1, :], s0im_ref[b:b + 1, :]]
    fin = lax.fori_loop(0, nc, body, tuple(init))
    for b in range(nb):
        fre_ref[b:b + 1, :] = fin[2 * b]
        fim_ref[b:b + 1, :] = fin[2 * b + 1]


def _ssm_scan(v_re, v_im, s0_re, s0_im, lam_re, lam_im, nb, nc):
    r, w = v_re.shape
    tw = STATE_W
    rows = pl.BlockSpec((r, tw), lambda j: (0, j))
    bat = pl.BlockSpec((nb, tw), lambda j: (0, j))
    one = pl.BlockSpec((1, tw), lambda j: (0, j))
    return pl.pallas_call(
        functools.partial(_ssm_scan_kernel, nb=nb, nc=nc),
        grid=(w // tw,),
        in_specs=[rows, rows, bat, bat, one, one],
        out_specs=[rows, rows, bat, bat],
        out_shape=[jax.ShapeDtypeStruct((r, w), F32)] * 2 + [jax.ShapeDtypeStruct((nb, w), F32)] * 2,
        compiler_params=_params("arbitrary"),
        name="ssm_chunk_scan",
    )(v_re, v_im, s0_re, s0_im, lam_re, lam_im)


def _ssm_y_kernel(*refs, t_chunk):
    u_refs = refs[:t_chunk]
    wt_ref, qre_ref, qim_ref, sre_ref, sim_ref, y_ref, ucat_ref = refs[t_chunk:]
    t = pl.program_id(1)

    @pl.when(t == 0)
    def _():
        for j in range(t_chunk):
            ucat_ref[:, j * BLOCK_W:(j + 1) * BLOCK_W] = u_refs[j][...].astype(BF16)

    start = pl.multiple_of((t_chunk - 1 - t) * BLOCK_W, BLOCK_W)
    y = _dot(ucat_ref[...], wt_ref[pl.ds(start, t_chunk * BLOCK_W), :])
    mask = _group_mask(BLOCK_W, STATE_W, SSM_GROUP, SSM_STATE)
    q_re = jnp.where(mask, jnp.concatenate([qre_ref[...]] * GROUPS_PER_BLOCK, axis=0), 0.0).astype(BF16)
    q_im = jnp.where(mask, jnp.concatenate([qim_ref[...]] * GROUPS_PER_BLOCK, axis=0), 0.0).astype(BF16)
    y += _dot_nt(sre_ref[...].astype(BF16), q_re) + _dot_nt(sim_ref[...].astype(BF16), q_im)
    y_ref[...] = y


def _ssm_y(u_flat, w_t, q_re, q_im, s_re, s_im, t_chunk):
    r = u_flat.shape[0]
    gb = w_t.shape[0]
    c = SSM_GROUP
    u_specs = [pl.BlockSpec((r, BLOCK_W), lambda b, t, j=j: (0, j * gb + b)) for j in range(t_chunk)]
    qspec = pl.BlockSpec((c, STATE_W), lambda b, t: (t + 1, b))
    sspec = pl.BlockSpec((r, STATE_W), lambda b, t: (0, b))
    return pl.pallas_call(
        functools.partial(_ssm_y_kernel, t_chunk=t_chunk),
        grid=(gb, t_chunk),
        in_specs=u_specs + [
            pl.BlockSpec((None, (2 * t_chunk - 1) * BLOCK_W, BLOCK_W), lambda b, t: (b, 0, 0)),
            qspec, qspec, sspec, sspec],
        out_specs=pl.BlockSpec((r, BLOCK_W), lambda b, t: (0, t * gb + b)),
        out_shape=jax.ShapeDtypeStruct(u_flat.shape, F32),
        scratch_shapes=[pltpu.VMEM((r, t_chunk * BLOCK_W), BF16)],
        compiler_params=_params("arbitrary", "arbitrary"),
        name="ssm_chunk_out",
    )(*([u_flat] * t_chunk), w_t, q_re, q_im, s_re, s_im)


def _lambda(lq1_ref, lk1_ref, lq2_ref, lk2_ref, lambda_init):
    return (jnp.exp(jnp.sum(lq1_ref[...] * lk1_ref[...], axis=-1, keepdims=True))
            - jnp.exp(jnp.sum(lq2_ref[...] * lk2_ref[...], axis=-1, keepdims=True)) + lambda_init)


def _attn_kernel(slope_ref, q_ref, k_ref, v_ref, lq1_ref, lk1_ref, lq2_ref, lk2_ref, g_ref, o_ref,
                 b0_ref, m_ref, l_ref, acc_ref, *, tq, tk, lambda_init):
    h = pl.program_id(1)
    qi = pl.program_id(2)
    hd = HEAD_DIM
    slope2 = slope_ref[h] * LOG2E
    rel = (lax.broadcasted_iota(jnp.int32, (tk, tq), 1) - lax.broadcasted_iota(jnp.int32, (tk, tq), 0)).astype(F32)
    b0_ref[...] = -slope2 * rel
    q = q_ref[...] * (hd ** -0.5 * LOG2E)
    first = lax.broadcasted_iota(jnp.int32, (tq, 2 * hd), 1) < hd
    qm = (jnp.where(first, q, 0.0).astype(BF16), jnp.where(first, 0.0, q).astype(BF16))
    m_ref[...] = jnp.full_like(m_ref, NEG_BIG)
    l_ref[...] = jnp.zeros_like(l_ref)
    acc_ref[...] = jnp.zeros_like(acc_ref)

    def chunk(kj, mask_from):
        row0 = pl.multiple_of(kj * tk, tk)
        k = k_ref[pl.ds(row0, tk), :].astype(BF16)
        v = v_ref[pl.ds(row0, tk), :].astype(BF16)
        c = -slope2 * (qi * tq - kj * tk).astype(F32)
        b0 = b0_ref[...]
        ts = [_dot_nt(k, qm[mp]) + b0 for mp in range(2)]
        if mask_from is not None:
            keep = b0 <= -slope2 * mask_from
            ts = [jnp.where(keep, t, -jnp.inf) for t in ts]
        m_old = [m_ref[mp] for mp in range(2)]
        m_new = [jnp.maximum(m_old[mp], jnp.max(ts[mp], axis=0, keepdims=True) + c) for mp in range(2)]
        ps = [jnp.exp2(ts[mp] - (m_new[mp] - c)) for mp in range(2)]
        for mp in range(2):
            alpha = jnp.exp2(m_old[mp] - m_new[mp])
            l_ref[mp] = alpha * l_ref[mp] + jnp.sum(ps[mp], axis=0, keepdims=True)
            acc_ref[mp] = alpha * acc_ref[mp] + _dot_tn(v, ps[mp].astype(BF16))
            m_ref[mp] = m_new[mp]

    ratio = tq // tk

    def body(kj, carry):
        chunk(kj, None)
        return carry

    lax.fori_loop(0, qi * ratio, body, 0)
    for d in range(ratio):
        chunk(qi * ratio + d, float(d * tk))

    lam = _lambda(lq1_ref, lk1_ref, lq2_ref, lk2_ref, lambda_init)
    o = acc_ref[0] / l_ref[0] - lam * (acc_ref[1] / l_ref[1])
    o = o * lax.rsqrt(jnp.mean(o * o, axis=0, keepdims=True) + LN_EPS) * g_ref[...] * (1.0 - lambda_init)
    o_ref[...] = o.T.astype(o_ref.dtype)


def _attention(q, k, v, slopes, lam_vecs, subln_g, nbatch, seq, lambda_init, tq, tk):
    n, da = q.shape
    nh = da // (2 * HEAD_DIM)
    nq = seq // tq
    hw = 2 * HEAD_DIM
    vec = pl.BlockSpec((1, HEAD_DIM), lambda b, h, i: (0, 0))
    return pl.pallas_call(
        functools.partial(_attn_kernel, tq=tq, tk=tk, lambda_init=lambda_init),
        grid=(nbatch, nh, nq),
        in_specs=[pl.BlockSpec(memory_space=pltpu.SMEM),
                  pl.BlockSpec((tq, hw), lambda b, h, i: (b * nq + i, h)),
                  pl.BlockSpec((seq, hw), lambda b, h, i: (b, h)),
                  pl.BlockSpec((seq, hw), lambda b, h, i: (b, h)),
                  vec, vec, vec, vec,
                  pl.BlockSpec((hw, 1), lambda b, h, i: (0, 0))],
        out_specs=pl.BlockSpec((tq, hw), lambda b, h, i: (b * nq + i, h)),
        out_shape=jax.ShapeDtypeStruct((n, da), BF16),
        scratch_shapes=[pltpu.VMEM((tk, tq), F32), pltpu.VMEM((2, 1, tq), F32), pltpu.VMEM((2, 1, tq), F32),
                        pltpu.VMEM((2, hw, tq), F32)],
        compiler_params=_params("arbitrary", "arbitrary", "arbitrary"),
        name="prompt_attention",
    )(slopes, q, k, v, *lam_vecs, subln_g.reshape(hw, 1))


def _dec_attn_kernel(pt_ref, q_ref, kn_ref, vn_ref, *rest, n_pages, page, tdec, nh, lambda_init):
    del pt_ref
    k_refs = rest[:n_pages] + (kn_ref,)
    v_refs = rest[n_pages:2 * n_pages] + (vn_ref,)
    lq1_ref, lk1_ref, lq2_ref, lk2_ref, g_ref, o_ref = rest[2 * n_pages:]
    hd = HEAD_DIM
    hw = 2 * hd
    ncol = 2 * nh * tdec
    past = n_pages * page
    q = q_ref[...] * (hd ** -0.5)
    rid = lax.broadcasted_iota(jnp.int32, (ncol, hw), 0)
    cid = lax.broadcasted_iota(jnp.int32, (ncol, hw), 1)
    qt = jnp.zeros((ncol, hw), F32)
    for h in range(nh):
        for i in range(tdec):
            sel = (rid // (2 * tdec) == h) & (rid % tdec == i)
            qt = jnp.where(sel, jnp.broadcast_to(q[i:i + 1, h * hw:(h + 1) * hw], (ncol, hw)), qt)
    qt = jnp.where((rid // tdec) % 2 == cid // hd, qt, 0.0).astype(BF16)
    col = lax.broadcasted_iota(jnp.int32, (1, ncol), 1)
    qidx = col % tdec
    hcol = col // (2 * tdec)
    slope = jnp.exp2((hcol + 1).astype(F32) * (-8.0 / nh))

    def rows(n):
        r = lax.broadcasted_iota(jnp.int32, (n, 1), 0)
        return r // nh, r % nh

    tok, hrow = rows(page * nh)
    base = jnp.where(hrow == hcol, -slope * (past + qidx - tok).astype(F32), -jnp.inf)
    tokn, hrown = rows(tdec * nh)
    basen = jnp.where((hrown == hcol) & (qidx >= tokn), -slope * (qidx - tokn).astype(F32), -jnp.inf)
    t_list, c_list = [], []
    m = jnp.full((1, ncol), NEG_BIG, F32)
    for j in range(n_pages + 1):
        raw = _dot_nt(k_refs[j][...].astype(BF16), qt)
        t = raw + (base if j < n_pages else basen)
        c = slope * float(j * page) if j < n_pages else jnp.zeros_like(slope)
        m = jnp.maximum(m, jnp.max(t, axis=0, keepdims=True) + c)
        t_list.append(t)
        c_list.append(c)
    l = jnp.zeros((1, ncol), F32)
    acc = jnp.zeros((ncol, hw), F32)
    for j in range(n_pages + 1):
        p = jnp.exp(t_list[j] - (m - c_list[j]))
        l = l + jnp.sum(p, axis=0, keepdims=True)
        p = p.astype(BF16)
        v = v_refs[j][...].astype(BF16)
        if p.shape[0] < LANES:
            pad = LANES - p.shape[0]
            p = jnp.concatenate([p, jnp.zeros((pad, ncol), BF16)], axis=0)
            v = jnp.concatenate([v, jnp.zeros((pad, hw), BF16)], axis=0)
        acc = acc + _dot_tn(p, v)
    eye = lax.broadcasted_iota(jnp.int32, (ncol, ncol), 0) == lax.broadcasted_iota(jnp.int32, (ncol, ncol), 1)
    lcol = jnp.sum(jnp.where(eye, jnp.broadcast_to(l, (ncol, ncol)), 0.0), axis=1, keepdims=True)
    acc = acc / lcol
    lam = _lambda(lq1_ref, lk1_ref, lq2_ref, lk2_ref, lambda_init)
    g = g_ref[...]
    for h in range(nh):
        blk = acc[h * 2 * tdec:(h + 1) * 2 * tdec]
        o = blk[:tdec] - lam * blk[tdec:]
        o = o * lax.rsqrt(jnp.mean(o * o, axis=-1, keepdims=True) + LN_EPS) * g * (1.0 - lambda_init)
        o_ref[:, h * hw:(h + 1) * hw] = o.astype(o_ref.dtype)


def _decode_attention(q3, k3, v3, cache_k, cache_v, page_table, lam_vecs, subln_g, lambda_init):
    nb, tdec, da = q3.shape
    n_pages = page_table.shape[1]
    rows, hw = cache_k.shape[1:]
    nh = da // hw
    page = rows // nh
    new = pl.BlockSpec((None, tdec, da), lambda b, pt: (b, 0, 0))
    newp = pl.BlockSpec((None, tdec * nh, hw), lambda b, pt: (b, 0, 0))
    pages = [pl.BlockSpec((None, rows, hw), lambda b, pt, j=j: (pt[b, j], 0, 0)) for j in range(n_pages)]
    vec = pl.BlockSpec((1, HEAD_DIM), lambda b, pt: (0, 0))
    return pl.pallas_call(
        functools.partial(_dec_attn_kernel, n_pages=n_pages, page=page, tdec=tdec, nh=nh, lambda_init=lambda_init),
        grid_spec=pltpu.PrefetchScalarGridSpec(
            num_scalar_prefetch=1,
            grid=(nb,),
            in_specs=[new, newp, newp] + pages + pages + [vec, vec, vec, vec,
                                                          pl.BlockSpec((1, hw), lambda b, pt: (0, 0))],
            out_specs=new),
        out_shape=jax.ShapeDtypeStruct((nb, tdec, da), BF16),
        compiler_params=_params("arbitrary"),
        name="decode_attention",
    )(page_table, q3, k3.reshape(nb, tdec * nh, hw), v3.reshape(nb, tdec * nh, hw),
      *([cache_k] * n_pages), *([cache_v] * n_pages), *lam_vecs, subln_g)


def _mix_in_kernel(y_ref, o_ref, gs_ref, ga_ref, wglu_ref, wus_ref, wua_ref, out_ref):
    z = jax.nn.gelu(y_ref[...])
    zz = z * jax.nn.sigmoid(_dot(z.astype(BF16), wglu_ref[...]))
    y_ssm = _dot(zz.astype(BF16), wus_ref[...])
    y_att = _dot(o_ref[...], wua_ref[...])
    out_ref[...] = (jax.nn.sigmoid(gs_ref[...]) * y_ssm + jax.nn.sigmoid(ga_ref[...]) * y_att).astype(out_ref.dtype)


def _mix_in(y, o, gates, w_glu, w_up_ssm, w_up_att, tm):
    n, ds = y.shape
    da = o.shape[1]
    d = w_up_ssm.shape[1]
    full = lambda a: pl.BlockSpec(a.shape, lambda i: (0, 0))
    return pl.pallas_call(
        _mix_in_kernel,
        grid=(n // tm,),
        in_specs=[pl.BlockSpec((tm, ds), lambda i: (i, 0)),
                  pl.BlockSpec((tm, da), lambda i: (i, 0)),
                  pl.BlockSpec((tm, d), lambda i: (i, 0)),
                  pl.BlockSpec((tm, d), lambda i: (i, 1)),
                  full(w_glu), full(w_up_ssm), full(w_up_att)],
        out_specs=pl.BlockSpec((tm, d), lambda i: (i, 0)),
        out_shape=jax.ShapeDtypeStruct((n, d), BF16),
        compiler_params=_params("arbitrary"),
        name="branch_mix",
    )(y, o, gates, gates, w_glu, w_up_ssm, w_up_att)


def _out_proj_kernel(mix_ref, x_ref, g1_ref, wo_ref, lg_ref, lb_ref, out_ref, *, alpha):
    r = alpha * x_ref[...] + g1_ref[...] * _dot(mix_ref[...], wo_ref[...])
    out_ref[...] = _ln(r) * lg_ref[...] + lb_ref[...]


def _out_proj(mix, x, mod3, rows_per_batch, w_o, ln_g, ln_b, alpha, tm):
    n, d = x.shape
    row = pl.BlockSpec((tm, d), lambda i: (i, 0))
    vec = pl.BlockSpec((1, d), lambda i: (0, 0))
    return pl.pallas_call(
        functools.partial(_out_proj_kernel, alpha=alpha),
        grid=(n // tm,),
        in_specs=[row, row, _mod_spec(mod3, tm, rows_per_batch, 2, 1),
                  pl.BlockSpec(w_o.shape, lambda i: (0, 0)), vec, vec],
        out_specs=row,
        out_shape=jax.ShapeDtypeStruct((n, d), F32),
        compiler_params=_params("arbitrary"),
        name="out_proj_ln1",
    )(mix, x, mod3, w_o, ln_g, ln_b)


def _ffn_kernel(x_ref, sh_ref, sc_ref, g2_ref, w1_ref, w2_ref, lg_ref, lb_ref, out_ref, h_ref, acc_ref,
                *, alpha, nf):
    f = pl.program_id(1)

    @pl.when(f == 0)
    def _():
        h_ref[...] = (_ln(x_ref[...]) * (1.0 + sc_ref[...]) + sh_ref[...]).astype(h_ref.dtype)
        acc_ref[...] = jnp.zeros_like(acc_ref)

    a = jnp.maximum(_dot(h_ref[...], w1_ref[...]), 0.0)
    acc_ref[...] += _dot((a * a).astype(BF16), w2_ref[...])

    @pl.when(f == nf - 1)
    def _():
        r = alpha * x_ref[...] + g2_ref[...] * acc_ref[...]
        out_ref[...] = _ln(r) * lg_ref[...] + lb_ref[...]


def _ffn(x, mod3, rows_per_batch, w1, w2, ln_g, ln_b, alpha, tm, tf):
    n, d = x.shape
    dff = w1.shape[1]
    nf = dff // tf
    row = pl.BlockSpec((tm, d), lambda i, f: (i, 0))
    vec = pl.BlockSpec((1, d), lambda i, f: (0, 0))
    return pl.pallas_call(
        functools.partial(_ffn_kernel, alpha=alpha, nf=nf),
        grid=(n // tm, nf),
        in_specs=[row,
                  _mod_spec(mod3, tm, rows_per_batch, 3, 2),
                  _mod_spec(mod3, tm, rows_per_batch, 4, 2),
                  _mod_spec(mod3, tm, rows_per_batch, 5, 2),
                  pl.BlockSpec((d, tf), lambda i, f: (0, f)),
                  pl.BlockSpec((tf, d), lambda i, f: (f, 0)),
                  vec, vec],
        out_specs=row,
        out_shape=jax.ShapeDtypeStruct((n, d), F32),
        scratch_shapes=[pltpu.VMEM((tm, d), BF16), pltpu.VMEM((tm, d), F32)],
        compiler_params=_params("arbitrary", "arbitrary"),
        name="ffn_ln2",
    )(x, mod3, mod3, mod3, w1, w2, ln_g, ln_b)


def _trunk_layer(x3, mod3, s0, cache, page_table, wts, tabs, lambda_init, alpha):
    nbatch, seq, d = x3.shape
    n = nbatch * seq
    x = x3.reshape(n, d)
    ds = wts["w_glu"].shape[0]
    da = wts["w_up_att"].shape[0]
    nh = da // (2 * HEAD_DIM)
    tm = min(512, seq) if mod3.shape[1] == 1 else min(512, n)
    tn = min(1024, ds)
    h = _ln_mod(x, mod3, seq, tm)
    w_in = wts["w_in"]
    u = _linear(h, w_in, 0, ds, tm, tn)
    q = _linear(h, w_in, ds, da, tm, tn)
    k = _linear(h, w_in, ds + da, da, tm, tn)
    v = _linear(h, w_in, ds + 2 * da, da, tm, tn)
    gates = _linear(h, w_in, ds + 3 * da, 2 * d, tm, tn)

    pd, q_re, q_im, kt, lp_re, lp_im = tabs
    npow = lp_re.shape[0]
    t_chunk = PROMPT_CHUNK if seq % PROMPT_CHUNK == 0 else seq
    nc = seq // t_chunk
    u_flat = u.reshape(n // t_chunk, t_chunk * ds)
    v_re, v_im = _ssm_p(u_flat, pd, t_chunk)
    gp = lp_re.shape[1] * SSM_STATE
    lam_re = lp_re[t_chunk, :, :SSM_STATE].reshape(1, gp)
    lam_im = lp_im[t_chunk, :, :SSM_STATE].reshape(1, gp)
    s_re, s_im, f_re, f_im = _ssm_scan(v_re, v_im, s0[0], s0[1], lam_re, lam_im, nbatch, nc)
    w_t = _toeplitz_tiles(kt, npow, t_chunk)
    y = _ssm_y(u_flat, w_t, q_re, q_im, s_re, s_im, t_chunk).reshape(n, ds)

    lam_vecs = wts["lam_vecs"]
    if cache is None:
        slopes = jnp.asarray([2.0 ** (-8.0 * (i + 1) / nh) for i in range(nh)], F32)
        tq = min(512, seq)
        o = _attention(q, k, v, slopes, lam_vecs, wts["subln_g"], nbatch, seq, lambda_init, tq, tq)
    else:
        o = _decode_attention(q.reshape(nbatch, seq, da), k.reshape(nbatch, seq, da), v.reshape(nbatch, seq, da),
                              cache[0], cache[1], page_table, lam_vecs, wts["subln_g"], lambda_init)
        o = o.reshape(n, da)

    mix = _mix_in(y, o, gates, wts["w_glu"], wts["w_up_ssm"], wts["w_up_att"], min(256, tm))
    x1 = _out_proj(mix, x, mod3, seq, wts["w_o"], wts["ln1_g"], wts["ln1_b"], alpha, tm)
    x2 = _ffn(x1, mod3, seq, wts["w_ff1"], wts["w_ff2"], wts["ln2_g"], wts["ln2_b"], alpha, tm,
              min(1024, wts["w_ff1"].shape[1]))
    return (x2.reshape(nbatch, seq, d), k.reshape(nbatch, seq, nh, 2 * HEAD_DIM),
            v.reshape(nbatch, seq, nh, 2 * HEAD_DIM), f_re, f_im)


def kernel(x_prompt, x_sample, c_prompt, c_sample, cache_k, cache_v, state_ssm_re, state_ssm_im, page_table,
           w_ada, b_ada, w_in, ssm_a_re, ssm_a_im, ssm_log_dt, ssm_b_re, ssm_b_im, ssm_c_re, ssm_c_im, ssm_d,
           w_glu, w_up_ssm, lam_q1, lam_k1, lam_q2, lam_k2, subln_g, w_up_att, w_o, ln1_g, ln1_b, w_ff1, w_ff2,
           ln2_g, ln2_b):
    depth = w_in.shape[0]
    bp, seq_p, d = x_prompt.shape
    bs, seq_s, _ = x_sample.shape
    g, p = ssm_a_re.shape[1:]
    alpha = (2 * depth) ** 0.25
    npow = PROMPT_CHUNK + 1
    xp, xs = x_prompt, x_sample
    outs_p, outs_s = [], []
    c_rows = jnp.concatenate([c_prompt, c_sample], axis=0)
    c_rows = jnp.pad(c_rows, ((0, (-(bp + bs)) % 8), (0, 0)))
    n_pool, page, nh, hw = cache_k.shape[1:]
    for l in range(depth):
        lambda_init = 0.8 - 0.6 * math.exp(-0.3 * l)
        mod = _ada(c_rows, w_ada[l], b_ada[l].reshape(1, 6 * d), min(1024, d))
        mod_p = mod[:bp].reshape(bp, 1, 6 * d)
        mod_s = jnp.repeat(mod[bp:bp + bs], seq_s, axis=0).reshape(1, bs * seq_s, 6 * d)
        wts = {
            "w_in": w_in[l].astype(BF16), "w_glu": w_glu[l].astype(BF16),
            "w_up_ssm": w_up_ssm[l].astype(BF16), "w_up_att": w_up_att[l].astype(BF16),
            "w_o": w_o[l].astype(BF16), "w_ff1": w_ff1[l].astype(BF16), "w_ff2": w_ff2[l].astype(BF16),
            "ln1_g": ln1_g[l].reshape(1, d), "ln1_b": ln1_b[l].reshape(1, d),
            "ln2_g": ln2_g[l].reshape(1, d), "ln2_b": ln2_b[l].reshape(1, d),
            "subln_g": subln_g[l].reshape(1, 2 * HEAD_DIM),
            "lam_vecs": tuple(a[l].reshape(1, HEAD_DIM) for a in (lam_q1, lam_k1, lam_q2, lam_k2)),
        }
        tabs = _ssm_prep(ssm_a_re[l], ssm_a_im[l], ssm_log_dt[l], ssm_b_re[l], ssm_b_im[l], ssm_c_re[l],
                         ssm_c_im[l], ssm_d[l], npow)
        zeros = jnp.zeros((bp, g * p), F32)
        res_p = _trunk_layer(xp, mod_p, (zeros, zeros), None, None, wts, tabs, lambda_init, alpha)
        cache = (cache_k[l].reshape(n_pool, page * nh, hw), cache_v[l].reshape(n_pool, page * nh, hw))
        s0 = (state_ssm_re[l].reshape(bs, g * p), state_ssm_im[l].reshape(bs, g * p))
        res_s = _trunk_layer(xs, mod_s, s0, cache, page_table, wts, tabs, lambda_init, alpha)
        xp, xs = res_p[0], res_s[0]
        outs_p.append(res_p[1:])
        outs_s.append(res_s[1:])

    def stack(outs, i, shape=None):
        a = jnp.stack([o[i] for o in outs])
        return a if shape is None else a.reshape((depth,) + shape)

    return (xp, xs,
            stack(outs_p, 0), stack(outs_p, 1), stack(outs_p, 2, (bp, g, p)), stack(outs_p, 3, (bp, g, p)),
            stack(outs_s, 0), stack(outs_s, 1), stack(outs_s, 2, (bs, g, p)), stack(outs_s, 3, (bs, g, p)))
```

```python
import functools
import math

import jax
import jax.numpy as jnp
from jax import lax
from jax.experimental import pallas as pl
from jax.experimental.pallas import tpu as pltpu

F32 = jnp.float32
BF16 = jnp.bfloat16

SSM_GROUP = 16
SSM_STATE = 64
HEAD_DIM = 64
LANES = 128
BLOCK_W = 256
GROUPS_PER_BLOCK = BLOCK_W // SSM_GROUP
STATE_W = GROUPS_PER_BLOCK * SSM_STATE
PROMPT_CHUNK = 16
LN_EPS = 1e-5
NEG_BIG = -1e30
LOG2E = 1.4426950408889634
VMEM_LIMIT = 56 * 1024 * 1024


def _params(*sem):
    return pltpu.CompilerParams(dimension_semantics=sem, vmem_limit_bytes=VMEM_LIMIT)


def _ln(x):
    mu = jnp.mean(x, axis=-1, keepdims=True)
    xc = x - mu
    var = jnp.mean(xc * xc, axis=-1, keepdims=True)
    return xc * lax.rsqrt(var + LN_EPS)


def _dot(a, b):
    return jnp.dot(a, b, preferred_element_type=F32)


def _dot_nt(a, b):
    return lax.dot_general(a, b, (((1,), (1,)), ((), ())), preferred_element_type=F32)


def _dot_tn(a, b):
    return lax.dot_general(a, b, (((0,), (0,)), ((), ())), preferred_element_type=F32)


def _mod_spec(mod3, tm, rows_per_batch, col, ngrid):
    d = mod3.shape[2] // 6
    if mod3.shape[1] == 1:
        per = rows_per_batch // tm
        if ngrid == 1:
            return pl.BlockSpec((None, 1, d), lambda i: (i // per, 0, col))
        return pl.BlockSpec((None, 1, d), lambda i, j: (i // per, 0, col))
    if ngrid == 1:
        return pl.BlockSpec((None, tm, d), lambda i: (0, i, col))
    return pl.BlockSpec((None, tm, d), lambda i, j: (0, i, col))


def _ada_kernel(c_ref, w_ref, b_ref, o_ref):
    c = c_ref[...]
    a = (c * jax.nn.sigmoid(c)).astype(BF16)
    o_ref[...] = _dot(a, w_ref[...].astype(BF16)) + b_ref[...]


def _ada(c, w, b, tn):
    m, d = c.shape
    n = w.shape[1]
    return pl.pallas_call(
        _ada_kernel,
        grid=(n // tn,),
        in_specs=[pl.BlockSpec((m, d), lambda j: (0, 0)),
                  pl.BlockSpec((d, tn), lambda j: (0, j)),
                  pl.BlockSpec((1, tn), lambda j: (0, j))],
        out_specs=pl.BlockSpec((m, tn), lambda j: (0, j)),
        out_shape=jax.ShapeDtypeStruct((m, n), F32),
        compiler_params=_params("arbitrary"),
        name="ada_mod",
    )(c, w, b)


def _ln_mod_kernel(x_ref, sh_ref, sc_ref, h_ref):
    h_ref[...] = (_ln(x_ref[...]) * (1.0 + sc_ref[...]) + sh_ref[...]).astype(h_ref.dtype)


def _ln_mod(x, mod3, rows_per_batch, tm):
    n, d = x.shape
    return pl.pallas_call(
        _ln_mod_kernel,
        grid=(n // tm,),
        in_specs=[pl.BlockSpec((tm, d), lambda i: (i, 0)),
                  _mod_spec(mod3, tm, rows_per_batch, 0, 1),
                  _mod_spec(mod3, tm, rows_per_batch, 1, 1)],
        out_specs=pl.BlockSpec((tm, d), lambda i: (i, 0)),
        out_shape=jax.ShapeDtypeStruct((n, d), BF16),
        compiler_params=_params("arbitrary"),
        name="ln_mod",
    )(x, mod3, mod3)


def _linear_kernel(a_ref, w_ref, o_ref):
    o_ref[...] = _dot(a_ref[...], w_ref[...]).astype(o_ref.dtype)


def _linear(a, w, col0, ncols, tm, tn, out_dtype=F32):
    n, k = a.shape
    cb = col0 // tn
    return pl.pallas_call(
        _linear_kernel,
        grid=(ncols // tn, n // tm),
        in_specs=[pl.BlockSpec((tm, k), lambda j, i: (i, 0)),
                  pl.BlockSpec((k, tn), lambda j, i: (0, cb + j))],
        out_specs=pl.BlockSpec((tm, tn), lambda j, i: (i, j)),
        out_shape=jax.ShapeDtypeStruct((n, ncols), out_dtype),
        compiler_params=_params("arbitrary", "arbitrary"),
        name="linear",
    )(a, w)


def _ssm_prep_kernel(are_ref, aim_ref, ldt_ref, bre_ref, bim_ref, cre_ref, cim_ref, d_ref,
                     pd_ref, qre_ref, qim_ref, kt_ref, lpre_ref, lpim_ref, *, npow, gt):
    c = SSM_GROUP
    p = SSM_STATE
    a_re = are_ref[...]
    a_im = aim_ref[...]
    dt = jnp.exp(ldt_ref[...])
    mag = jnp.exp(dt * a_re)
    ab_re = mag * jnp.cos(dt * a_im)
    ab_im = mag * jnp.sin(dt * a_im)
    den = a_re * a_re + a_im * a_im
    f_re = ((ab_re - 1.0) * a_re + ab_im * a_im) / den
    f_im = (ab_im * a_re - (ab_re - 1.0) * a_im) / den
    b_re = bre_ref[...]
    b_im = bim_ref[...]
    bb_re = f_re[:, None, :] * b_re - f_im[:, None, :] * b_im
    bb_im = f_re[:, None, :] * b_im + f_im[:, None, :] * b_re
    c_re = cre_ref[...]
    c_im = cim_ref[...]
    low = lax.broadcasted_iota(jnp.int32, (1, 1, 2 * p), 2) < p
    p_re = jnp.ones_like(a_re)
    p_im = jnp.zeros_like(a_re)
    g_re_all, g_im_all = [], []
    for n in range(npow):
        pr = p_re[:, None, :]
        pi = p_im[:, None, :]
        pw_re = pr * bb_re - pi * bb_im
        pw_im = pr * bb_im + pi * bb_re
        pd_ref[n] = jnp.concatenate([pw_re, pw_im], axis=-1).reshape(gt * c, 4 * p)
        g_re = c_re * pr - c_im * pi
        g_im = -(c_re * pi + c_im * pr)
        g_re_all.append(g_re)
        g_im_all.append(g_im)
        lpre_ref[n] = p_re
        lpim_ref[n] = p_im
        p_re, p_im = p_re * ab_re - p_im * ab_im, p_re * ab_im + p_im * ab_re
    g_re_all = jnp.concatenate(g_re_all, axis=1)
    g_im_all = jnp.concatenate(g_im_all, axis=1)
    for gg in range(0, gt, 2):
        sl = slice(gg // 2 * 2 * p, (gg // 2 + 1) * 2 * p)
        qre_ref[:, sl] = jnp.where(low[0], g_re_all[gg], g_re_all[gg + 1])
        qim_ref[:, sl] = jnp.where(low[0], g_im_all[gg], g_im_all[gg + 1])
    rows = kt_ref.shape[2]
    zpad = jnp.zeros((gt, rows - npow * c, 2 * p), F32)
    dn = (((2,), (2,)), ((0,), (0,)))
    kt = (lax.dot_general(jnp.where(low, bb_re, 0.0), jnp.concatenate([g_re_all, zpad], axis=1), dn,
                          precision=lax.Precision.HIGHEST, preferred_element_type=F32)
          + lax.dot_general(jnp.where(low, bb_im, 0.0), jnp.concatenate([g_im_all, zpad], axis=1), dn,
                            precision=lax.Precision.HIGHEST, preferred_element_type=F32))
    eye = (lax.broadcasted_iota(jnp.int32, (c, rows), 0) == lax.broadcasted_iota(jnp.int32, (c, rows), 1))
    kt_ref[...] = kt + jnp.where(eye[None], d_ref[...], 0.0)


def _ssm_prep(a_re, a_im, log_dt, b_re, b_im, c_re, c_im, d_skip, npow):
    g, p = a_re.shape
    c = SSM_GROUP
    gt = 8
    kt_w = -(-npow * c // LANES) * LANES
    dup = lambda a: jnp.concatenate([a, a], axis=-1)
    gp = pl.BlockSpec((gt, 2 * p), lambda i: (i, 0))
    gcp = pl.BlockSpec((gt, c, 2 * p), lambda i: (i, 0, 0))
    qspec = pl.BlockSpec((npow * c, gt * p), lambda i: (0, i))
    lspec = pl.BlockSpec((npow, gt, 2 * p), lambda i: (0, i, 0))
    return pl.pallas_call(
        functools.partial(_ssm_prep_kernel, npow=npow, gt=gt),
        grid=(g // gt,),
        in_specs=[gp, gp, pl.BlockSpec((gt, 1), lambda i: (i, 0)), gcp, gcp, gcp, gcp,
                  pl.BlockSpec((gt, c, 1), lambda i: (i, 0, 0))],
        out_specs=[pl.BlockSpec((npow, gt * c, 4 * p), lambda i: (0, i, 0)), qspec, qspec,
                   pl.BlockSpec((gt, c, kt_w), lambda i: (i, 0, 0)), lspec, lspec],
        out_shape=[jax.ShapeDtypeStruct((npow, g * c, 4 * p), F32),
                   jax.ShapeDtypeStruct((npow * c, g * p), F32),
                   jax.ShapeDtypeStruct((npow * c, g * p), F32),
                   jax.ShapeDtypeStruct((g, c, kt_w), F32),
                   jax.ShapeDtypeStruct((npow, g, 2 * p), F32),
                   jax.ShapeDtypeStruct((npow, g, 2 * p), F32)],
        compiler_params=_params("arbitrary"),
        name="ssm_prep",
    )(dup(a_re), dup(a_im), log_dt.reshape(g, 1), dup(jnp.swapaxes(b_re, 1, 2)), dup(jnp.swapaxes(b_im, 1, 2)),
      dup(c_re), dup(c_im), d_skip.reshape(g, c, 1))


def _toeplitz_kernel(kt_ref, w_ref, *, t_max):
    c = SSM_GROUP
    kw = kt_ref.shape[2]
    x = kt_ref[...].reshape(BLOCK_W, kw).astype(BF16)
    mask = _group_mask(BLOCK_W, BLOCK_W, c, c)
    sel_r = lax.broadcasted_iota(jnp.int32, (kw, BLOCK_W), 0)
    sel_c = lax.broadcasted_iota(jnp.int32, (kw, BLOCK_W), 1)
    for n in range(t_max):
        e = jnp.where((sel_r // c == n) & (sel_r % c == sel_c % c), 1.0, 0.0).astype(BF16)
        tile = jnp.where(mask, _dot(x, e), 0.0)
        w_ref[pl.ds((t_max - 1 - n) * BLOCK_W, BLOCK_W), :] = tile.astype(w_ref.dtype)
    w_ref[pl.ds(t_max * BLOCK_W, (t_max - 1) * BLOCK_W), :] = jnp.zeros(((t_max - 1) * BLOCK_W, BLOCK_W), w_ref.dtype)


def _toeplitz_tiles(kt, t_max):
    g, c, kw = kt.shape
    gb = g // GROUPS_PER_BLOCK
    rows = (2 * t_max - 1) * BLOCK_W
    return pl.pallas_call(
        functools.partial(_toeplitz_kernel, t_max=t_max),
        grid=(gb,),
        in_specs=[pl.BlockSpec((GROUPS_PER_BLOCK, c, kw), lambda b: (b, 0, 0))],
        out_specs=pl.BlockSpec((None, rows, BLOCK_W), lambda b: (b, 0, 0)),
        out_shape=jax.ShapeDtypeStruct((gb, rows, BLOCK_W), BF16),
        compiler_params=_params("arbitrary"),
        name="ssm_toeplitz",
    )(kt)


def _group_mask(rows, cols, row_div, col_div):
    return (lax.broadcasted_iota(jnp.int32, (rows, cols), 0) // row_div
            == lax.broadcasted_iota(jnp.int32, (rows, cols), 1) // col_div)


def _ssm_p_kernel(u_ref, pd_ref, vre_ref, vim_ref, acc_ref, *, t_chunk):
    t = pl.program_id(1)

    @pl.when(t == 0)
    def _():
        acc_ref[...] = jnp.zeros_like(acc_ref)

    blk = pd_ref[...]
    mask = _group_mask(BLOCK_W, STATE_W, SSM_GROUP, SSM_STATE)
    reps = STATE_W // LANES
    w_re = jnp.where(mask, jnp.concatenate([blk[:, :LANES]] * reps, axis=1), 0.0)
    w_im = jnp.where(mask, jnp.concatenate([blk[:, LANES:]] * reps, axis=1), 0.0)
    w = jnp.concatenate([w_re, w_im], axis=1).astype(BF16)
    acc_ref[...] += _dot(u_ref[...].astype(BF16), w)

    @pl.when(t == t_chunk - 1)
    def _():
        vre_ref[...] = acc_ref[:, :STATE_W]
        vim_ref[...] = acc_ref[:, STATE_W:]


def _ssm_p(u_flat, pd, t_chunk):
    r = u_flat.shape[0]
    gb = pd.shape[1] // BLOCK_W
    out = jax.ShapeDtypeStruct((r, gb * STATE_W), F32)
    return pl.pallas_call(
        functools.partial(_ssm_p_kernel, t_chunk=t_chunk),
        grid=(gb, t_chunk),
        in_specs=[pl.BlockSpec((r, BLOCK_W), lambda b, t: (0, t * gb + b)),
                  pl.BlockSpec((None, BLOCK_W, 2 * LANES), lambda b, t: (t_chunk - 1 - t, b, 0))],
        out_specs=[pl.BlockSpec((r, STATE_W), lambda b, t: (0, b))] * 2,
        out_shape=[out, out],
        scratch_shapes=[pltpu.VMEM((r, 2 * STATE_W), F32)],
        compiler_params=_params("arbitrary", "arbitrary"),
        name="ssm_chunk_state",
    )(u_flat, pd)


def _ssm_scan_kernel(vre_ref, vim_ref, s0re_ref, s0im_ref, lre_ref, lim_ref,
                     sre_ref, sim_ref, fre_ref, fim_ref, *, nb, nc):
    ar = lre_ref[...]
    ai = lim_ref[...]
    if nc == 1:
        sr = s0re_ref[...]
        si = s0im_ref[...]
        sre_ref[...] = sr
        sim_ref[...] = si
        fre_ref[...] = ar * sr - ai * si + vre_ref[...]
        fim_ref[...] = ar * si + ai * sr + vim_ref[...]
        return

    def body(k, carry):
        new = []
        for b in range(nb):
            sr, si = carry[2 * b], carry[2 * b + 1]
            row = pl.ds(b * nc + k, 1)
            sre_ref[row, :] = sr
            sim_ref[row, :] = si
            new.append(ar * sr - ai * si + vre_ref[row, :])
            new.append(ar * si + ai * sr + vim_ref[row, :])
        return tuple(new)

    init = []
    for b in range(nb):
        init += [s0re_ref[b:b + 1, :], s0im_ref[b:b + 1, :]]
    fin = lax.fori_loop(0, nc, body, tuple(init))
    for b in range(nb):
        fre_ref[b:b + 1, :] = fin[2 * b]
        fim_ref[b:b + 1, :] = fin[2 * b + 1]


def _ssm_scan(v_re, v_im, s0_re, s0_im, lam_re, lam_im, nb, nc):
    r, w = v_re.shape
    tw = STATE_W
    rows = pl.BlockSpec((r, tw), lambda j: (0, j))
    bat = pl.BlockSpec((nb, tw), lambda j: (0, j))
    one = pl.BlockSpec((1, tw), lambda j: (0, j))
    return pl.pallas_call(
        functools.partial(_ssm_scan_kernel, nb=nb, nc=nc),
        grid=(w // tw,),
        in_specs=[rows, rows, bat, bat, one, one],
        out_specs=[rows, rows, bat, bat],
        out_shape=[jax.ShapeDtypeStruct((r, w), F32)] * 2 + [jax.ShapeDtypeStruct((nb, w), F32)] * 2,
        compiler_params=_params("arbitrary"),
        name="ssm_chunk_scan",
    )(v_re, v_im, s0_re, s0_im, lam_re, lam_im)


def _ssm_y_kernel(*refs, t_chunk):
    u_refs = refs[:t_chunk]
    wt_ref, qre_ref, qim_ref, sre_ref, sim_ref, y_ref, ucat_ref, sbf_ref = refs[t_chunk:]
    t = pl.program_id(1)
    t_max = (wt_ref.shape[0] // BLOCK_W + 1) // 2

    @pl.when(t == 0)
    def _():
        for j in range(t_chunk):
            ucat_ref[:, j * BLOCK_W:(j + 1) * BLOCK_W] = u_refs[j][...].astype(BF16)
        sbf_ref[0] = sre_ref[...].astype(BF16)
        sbf_ref[1] = sim_ref[...].astype(BF16)

    mask = _group_mask(BLOCK_W, STATE_W, SSM_GROUP, SSM_STATE)
    q_re = jnp.where(mask, jnp.concatenate([qre_ref[...]] * GROUPS_PER_BLOCK, axis=0), 0.0).astype(BF16)
    q_im = jnp.where(mask, jnp.concatenate([qim_ref[...]] * GROUPS_PER_BLOCK, axis=0), 0.0).astype(BF16)
    y_state = _dot_nt(sbf_ref[0], q_re) + _dot_nt(sbf_ref[1], q_im)
    start = pl.multiple_of((t_max - 1 - t) * BLOCK_W, BLOCK_W)

    def emit(n_tok):
        y_ref[...] = y_state + _dot(ucat_ref[:, :n_tok * BLOCK_W], wt_ref[pl.ds(start, n_tok * BLOCK_W), :])

    half = t_chunk // 2
    if half >= 2:
        pl.when(t < half)(lambda: emit(half))
        pl.when(t >= half)(lambda: emit(t_chunk))
    else:
        emit(t_chunk)


def _ssm_y(u_flat, w_t, q_re, q_im, s_re, s_im, t_chunk):
    r = u_flat.shape[0]
    gb = w_t.shape[0]
    c = SSM_GROUP
    u_specs = [pl.BlockSpec((r, BLOCK_W), lambda b, t, j=j: (0, j * gb + b)) for j in range(t_chunk)]
    qspec = pl.BlockSpec((c, STATE_W), lambda b, t: (t + 1, b))
    sspec = pl.BlockSpec((r, STATE_W), lambda b, t: (0, b))
    return pl.pallas_call(
        functools.partial(_ssm_y_kernel, t_chunk=t_chunk),
        grid=(gb, t_chunk),
        in_specs=u_specs + [
            pl.BlockSpec((None,) + w_t.shape[1:], lambda b, t: (b, 0, 0)),
            qspec, qspec, sspec, sspec],
        out_specs=pl.BlockSpec((r, BLOCK_W), lambda b, t: (0, t * gb + b)),
        out_shape=jax.ShapeDtypeStruct(u_flat.shape, F32),
        scratch_shapes=[pltpu.VMEM((r, t_chunk * BLOCK_W), BF16), pltpu.VMEM((2, r, STATE_W), BF16)],
        compiler_params=_params("arbitrary", "arbitrary"),
        name="ssm_chunk_out",
    )(*([u_flat] * t_chunk), w_t, q_re, q_im, s_re, s_im)


def _lambda(lq1_ref, lk1_ref, lq2_ref, lk2_ref, lambda_init):
    return (jnp.exp(jnp.sum(lq1_ref[...] * lk1_ref[...], axis=-1, keepdims=True))
            - jnp.exp(jnp.sum(lq2_ref[...] * lk2_ref[...], axis=-1, keepdims=True)) + lambda_init)


def _bias_columns(n, off, pieces, key_side):
    hw = 2 * HEAD_DIM
    lane = lax.broadcasted_iota(jnp.int32, (1, hw), 1) - off
    valid = (lane >= 0) & (lane < 4 * len(pieces))
    a = lane // 4
    kap = lane % 4
    s_lane = pieces[-1]
    for i in range(len(pieces) - 2, -1, -1):
        s_lane = jnp.where(a == i, pieces[i], s_lane)
    pos = lax.broadcasted_iota(jnp.int32, (n, 1), 0)
    hi = (pos // 64 * 64).astype(F32)
    lo = (pos % 64).astype(F32)
    if key_side:
        val = jnp.where(kap < 2, s_lane, jnp.where(kap == 2, hi, lo))
    else:
        val = jnp.where(kap == 0, -hi, jnp.where(kap == 1, -lo, s_lane))
    return jnp.where(valid, val, 0.0)


def _attn_kernel(slope_ref, q_ref, k_ref, v_ref, lq1_ref, lk1_ref, lq2_ref, lk2_ref, g_ref, o_ref,
                 rel_ref, ka_ref, qb_ref, m_ref, l_ref, acc_ref, *, tq, tk, nq, lambda_init):
    h = pl.program_id(1)
    qi = pl.program_id(2)
    hd = HEAD_DIM
    hw = 2 * hd
    slope2 = slope_ref[h] * LOG2E
    first = lax.broadcasted_iota(jnp.int32, (1, hw), 1) < hd

    @pl.when(qi == 0)
    def _():
        sv = jnp.full((1, hw), slope2, F32)
        s1 = sv.astype(BF16).astype(F32)
        s2 = (sv - s1).astype(BF16).astype(F32)
        s3 = (sv - s1 - s2).astype(BF16).astype(F32)
        pieces = (s1, s2, s3)
        qb_ref[0] = _bias_columns(tq, hd, pieces, False)
        qb_ref[1] = _bias_columns(tq, 0, pieces, False)
        ka_ref[0] = _bias_columns(tk, hd, pieces, True)
        ka_ref[1] = _bias_columns(tk, 0, pieces, True)
        rel_ref[...] = (lax.broadcasted_iota(jnp.int32, (tk, tq), 1)
                        - lax.broadcasted_iota(jnp.int32, (tk, tq), 0)).astype(F32)

    q = q_ref[...] * (hd ** -0.5 * LOG2E)
    qm = (jnp.where(first, q, qb_ref[0]).astype(BF16), jnp.where(first, qb_ref[1], q).astype(BF16))
    ones = jnp.ones((16, tk), BF16)
    m_ref[...] = jnp.full_like(m_ref, NEG_BIG)
    l_ref[...] = jnp.zeros_like(l_ref)
    acc_ref[...] = jnp.zeros_like(acc_ref)

    def scores(kj, q_blk, masked):
        k = k_ref[pl.ds(kj * tk, tk), :]
        km = (jnp.where(first, k, ka_ref[0]).astype(BF16), jnp.where(first, ka_ref[1], k).astype(BF16))
        ts = [_dot_nt(km[mp], qm[mp]) for mp in range(2)]
        if masked:
            keep = rel_ref[...] >= 0.0
            ts = [jnp.where(keep, t, -jnp.inf) for t in ts]
        return ts, -slope2 * float(q_blk * tq - kj * tk)

    def update(kj, ts, c):
        v = v_ref[pl.ds(kj * tk, tk), :].astype(BF16)
        m_old = [m_ref[mp] for mp in range(2)]
        m_new = [jnp.maximum(m_old[mp], jnp.max(ts[mp], axis=0, keepdims=True) + c) for mp in range(2)]
        ps = [jnp.exp2(ts[mp] - (m_new[mp] - c)).astype(BF16) for mp in range(2)]
        for mp in range(2):
            alpha = jnp.exp2(m_old[mp] - m_new[mp])
            l_ref[mp] = alpha * l_ref[mp] + _dot(ones, ps[mp])[0:1]
            acc_ref[mp] = alpha * acc_ref[mp] + _dot_tn(v, ps[mp])
            m_ref[mp] = m_new[mp]

    for q_blk in range(nq):
        @pl.when(qi == q_blk)
        def _(q_blk=q_blk):
            nxt = scores(0, q_blk, q_blk == 0)
            for kj in range(q_blk + 1):
                cur = nxt
                if kj < q_blk:
                    nxt = scores(kj + 1, q_blk, kj + 1 == q_blk)
                update(kj, *cur)

    lam = _lambda(lq1_ref, lk1_ref, lq2_ref, lk2_ref, lambda_init)
    o = acc_ref[0] / l_ref[0] - lam * (acc_ref[1] / l_ref[1])
    o = o * lax.rsqrt(jnp.mean(o * o, axis=0, keepdims=True) + LN_EPS) * g_ref[...] * (1.0 - lambda_init)
    o_ref[...] = o.T.astype(o_ref.dtype)


def _attention(q, k, v, slopes, lam_vecs, subln_g, nbatch, seq, lambda_init, tq, tk):
    n, da = q.shape
    nh = da // (2 * HEAD_DIM)
    nq = seq // tq
    hw = 2 * HEAD_DIM
    vec = pl.BlockSpec((1, HEAD_DIM), lambda b, h, i: (0, 0))
    return pl.pallas_call(
        functools.partial(_attn_kernel, tq=tq, tk=tk, nq=nq, lambda_init=lambda_init),
        grid=(nbatch, nh, nq),
        in_specs=[pl.BlockSpec(memory_space=pltpu.SMEM),
                  pl.BlockSpec((tq, hw), lambda b, h, i: (b * nq + i, h)),
                  pl.BlockSpec((seq, hw), lambda b, h, i: (b, h)),
                  pl.BlockSpec((seq, hw), lambda b, h, i: (b, h)),
                  vec, vec, vec, vec,
                  pl.BlockSpec((hw, 1), lambda b, h, i: (0, 0))],
        out_specs=pl.BlockSpec((tq, hw), lambda b, h, i: (b * nq + i, h)),
        out_shape=jax.ShapeDtypeStruct((n, da), BF16),
        scratch_shapes=[pltpu.VMEM((tk, tq), F32), pltpu.VMEM((2, tk, hw), F32), pltpu.VMEM((2, tq, hw), F32),
                        pltpu.VMEM((2, 1, tq), F32),
                        pltpu.VMEM((2, 1, tq), F32), pltpu.VMEM((2, hw, tq), F32)],
        compiler_params=_params("arbitrary", "arbitrary", "arbitrary"),
        name="prompt_attention",
    )(slopes, q, k, v, *lam_vecs, subln_g.reshape(hw, 1))


def _dec_attn_kernel(pt_ref, q_ref, kn_ref, vn_ref, *rest, n_pages, page, tdec, nh, lambda_init):
    del pt_ref
    k_refs = rest[:n_pages] + (kn_ref,)
    v_refs = rest[n_pages:2 * n_pages] + (vn_ref,)
    lq1_ref, lk1_ref, lq2_ref, lk2_ref, g_ref, o_ref = rest[2 * n_pages:]
    hd = HEAD_DIM
    hw = 2 * hd
    ncol = 2 * nh * tdec
    past = n_pages * page
    q = q_ref[...] * (hd ** -0.5 * LOG2E)
    rid = lax.broadcasted_iota(jnp.int32, (ncol, hw), 0)
    cid = lax.broadcasted_iota(jnp.int32, (ncol, hw), 1)
    qt = jnp.zeros((ncol, hw), F32)
    for h in range(nh):
        for i in range(tdec):
            sel = (rid // (2 * tdec) == h) & (rid % tdec == i)
            qt = jnp.where(sel, jnp.broadcast_to(q[i:i + 1, h * hw:(h + 1) * hw], (ncol, hw)), qt)
    qt = jnp.where((rid // tdec) % 2 == cid // hd, qt, 0.0)
    zq = jnp.zeros_like(qt)
    qt2 = jnp.concatenate([jnp.concatenate([qt, zq], axis=1), jnp.concatenate([zq, qt], axis=1)],
                          axis=0).astype(BF16)
    col2 = lax.broadcasted_iota(jnp.int32, (1, 2 * ncol), 1)
    second = col2 // ncol
    col = col2 % ncol
    qidx = col % tdec
    hcol = col // (2 * tdec)
    slope = jnp.exp2((hcol + 1).astype(F32) * (-8.0 / nh)) * LOG2E

    def rows(n):
        r = lax.broadcasted_iota(jnp.int32, (n, 1), 0)
        return r // nh, r % nh

    tok, hrow = rows(page * nh)
    base = jnp.where(hrow == hcol, -slope * (past + qidx - tok - second * page).astype(F32), -jnp.inf)
    tokn, hrown = rows(tdec * nh)
    basen = jnp.where((hrown == hcol) & (qidx >= tokn) & (second == 0),
                      -slope * (qidx - tokn).astype(F32), -jnp.inf)
    blocks = [(j, j + 1, base, slope * float(j * page)) for j in range(0, n_pages - 1, 2)]
    if n_pages % 2:
        blocks.append((n_pages - 1, None, jnp.where(second == 0, base, -jnp.inf), slope * float((n_pages - 1) * page)))
    blocks.append((n_pages, None, basen, jnp.zeros_like(slope)))

    def pair(refs, a, b):
        xa = refs[a][...]
        xb = refs[b][...] if b is not None else jnp.zeros_like(xa)
        return jnp.concatenate([xa, xb], axis=1).astype(BF16)

    t_list = []
    m2 = jnp.full((1, 2 * ncol), NEG_BIG, F32)
    for a, b, bias, c in blocks:
        t = _dot_nt(pair(k_refs, a, b), qt2) + bias
        m2 = jnp.maximum(m2, jnp.max(t, axis=0, keepdims=True) + c)
        t_list.append(t)
    m = jnp.maximum(m2[:, :ncol], m2[:, ncol:])
    m2 = jnp.concatenate([m, m], axis=1)
    l2 = jnp.zeros((1, 2 * ncol), F32)
    acc2 = jnp.zeros((2 * ncol, 2 * hw), F32)
    for (a, b, bias, c), t in zip(blocks, t_list):
        p = jnp.exp2(t - (m2 - c))
        l2 = l2 + jnp.sum(p, axis=0, keepdims=True)
        p = p.astype(BF16)
        v = pair(v_refs, a, b)
        if p.shape[0] < LANES:
            pad = LANES - p.shape[0]
            p = jnp.concatenate([p, jnp.zeros((pad, 2 * ncol), BF16)], axis=0)
            v = jnp.concatenate([v, jnp.zeros((pad, 2 * hw), BF16)], axis=0)
        acc2 = acc2 + _dot_tn(p, v)
    l = l2[:, :ncol] + l2[:, ncol:]
    acc = acc2[:ncol, :hw] + acc2[ncol:, hw:]
    eye = lax.broadcasted_iota(jnp.int32, (ncol, ncol), 0) == lax.broadcasted_iota(jnp.int32, (ncol, ncol), 1)
    lcol = jnp.sum(jnp.where(eye, jnp.broadcast_to(l, (ncol, ncol)), 0.0), axis=1, keepdims=True)
    acc = acc / lcol
    lam = _lambda(lq1_ref, lk1_ref, lq2_ref, lk2_ref, lambda_init)
    g = g_ref[...]
    for h in range(nh):
        blk = acc[h * 2 * tdec:(h + 1) * 2 * tdec]
        o = blk[:tdec] - lam * blk[tdec:]
        o = o * lax.rsqrt(jnp.mean(o * o, axis=-1, keepdims=True) + LN_EPS) * g * (1.0 - lambda_init)
        o_ref[:, h * hw:(h + 1) * hw] = o.astype(o_ref.dtype)


def _decode_attention(q3, k3, v3, cache_k, cache_v, page_table, lam_vecs, subln_g, lambda_init):
    nb, tdec, da = q3.shape
    n_pages = page_table.shape[1]
    rows, hw = cache_k.shape[1:]
    nh = da // hw
    page = rows // nh
    new = pl.BlockSpec((None, tdec, da), lambda b, pt: (b, 0, 0))
    newp = pl.BlockSpec((None, tdec * nh, hw), lambda b, pt: (b, 0, 0))
    pages = [pl.BlockSpec((None, rows, hw), lambda b, pt, j=j: (pt[b, j], 0, 0)) for j in range(n_pages)]
    vec = pl.BlockSpec((1, HEAD_DIM), lambda b, pt: (0, 0))
    return pl.pallas_call(
        functools.partial(_dec_attn_kernel, n_pages=n_pages, page=page, tdec=tdec, nh=nh, lambda_init=lambda_init),
        grid_spec=pltpu.PrefetchScalarGridSpec(
            num_scalar_prefetch=1,
            grid=(nb,),
            in_specs=[new, newp, newp] + pages + pages + [vec, vec, vec, vec,
                                                          pl.BlockSpec((1, hw), lambda b, pt: (0, 0))],
            out_specs=new),
        out_shape=jax.ShapeDtypeStruct((nb, tdec, da), BF16),
        compiler_params=_params("arbitrary"),
        name="decode_attention",
    )(page_table, q3, k3.reshape(nb, tdec * nh, hw), v3.reshape(nb, tdec * nh, hw),
      *([cache_k] * n_pages), *([cache_v] * n_pages), *lam_vecs, subln_g)


def _mix_in_kernel(y_ref, o_ref, gs_ref, ga_ref, wglu_ref, wus_ref, wua_ref, out_ref):
    z = jax.nn.gelu(y_ref[...])
    zz = z * jax.nn.sigmoid(_dot(z.astype(BF16), wglu_ref[...]))
    y_ssm = _dot(zz.astype(BF16), wus_ref[...])
    y_att = _dot(o_ref[...], wua_ref[...])
    out_ref[...] = (jax.nn.sigmoid(gs_ref[...]) * y_ssm + jax.nn.sigmoid(ga_ref[...]) * y_att).astype(out_ref.dtype)


def _mix_in(y, o, gates, w_glu, w_up_ssm, w_up_att, tm):
    n, ds = y.shape
    da = o.shape[1]
    d = w_up_ssm.shape[1]
    full = lambda a: pl.BlockSpec(a.shape, lambda i: (0, 0))
    return pl.pallas_call(
        _mix_in_kernel,
        grid=(n // tm,),
        in_specs=[pl.BlockSpec((tm, ds), lambda i: (i, 0)),
                  pl.BlockSpec((tm, da), lambda i: (i, 0)),
                  pl.BlockSpec((tm, d), lambda i: (i, 0)),
                  pl.BlockSpec((tm, d), lambda i: (i, 1)),
                  full(w_glu), full(w_up_ssm), full(w_up_att)],
        out_specs=pl.BlockSpec((tm, d), lambda i: (i, 0)),
        out_shape=jax.ShapeDtypeStruct((n, d), BF16),
        compiler_params=_params("arbitrary"),
        name="branch_mix",
    )(y, o, gates, gates, w_glu, w_up_ssm, w_up_att)


def _out_proj_kernel(mix_ref, x_ref, g1_ref, wo_ref, lg_ref, lb_ref, out_ref, *, alpha):
    r = alpha * x_ref[...] + g1_ref[...] * _dot(mix_ref[...], wo_ref[...])
    out_ref[...] = _ln(r) * lg_ref[...] + lb_ref[...]


def _out_proj(mix, x, mod3, rows_per_batch, w_o, ln_g, ln_b, alpha, tm):
    n, d = x.shape
    row = pl.BlockSpec((tm, d), lambda i: (i, 0))
    vec = pl.BlockSpec((1, d), lambda i: (0, 0))
    return pl.pallas_call(
        functools.partial(_out_proj_kernel, alpha=alpha),
        grid=(n // tm,),
        in_specs=[row, row, _mod_spec(mod3, tm, rows_per_batch, 2, 1),
                  pl.BlockSpec(w_o.shape, lambda i: (0, 0)), vec, vec],
        out_specs=row,
        out_shape=jax.ShapeDtypeStruct((n, d), F32),
        compiler_params=_params("arbitrary"),
        name="out_proj_ln1",
    )(mix, x, mod3, w_o, ln_g, ln_b)


def _ffn_kernel(x_ref, sh_ref, sc_ref, g2_ref, w1_ref, w2_ref, lg_ref, lb_ref, out_ref, h_ref, acc_ref,
                *, alpha, nf):
    f = pl.program_id(1)

    @pl.when(f == 0)
    def _():
        h_ref[...] = (_ln(x_ref[...]) * (1.0 + sc_ref[...]) + sh_ref[...]).astype(h_ref.dtype)
        acc_ref[...] = jnp.zeros_like(acc_ref)

    a = jnp.maximum(_dot(h_ref[...], w1_ref[...]), 0.0)
    acc_ref[...] += _dot((a * a).astype(BF16), w2_ref[...])

    @pl.when(f == nf - 1)
    def _():
        r = alpha * x_ref[...] + g2_ref[...] * acc_ref[...]
        out_ref[...] = _ln(r) * lg_ref[...] + lb_ref[...]


def _ffn(x, mod3, rows_per_batch, w1, w2, ln_g, ln_b, alpha, tm, tf):
    n, d = x.shape
    dff = w1.shape[1]
    nf = dff // tf
    row = pl.BlockSpec((tm, d), lambda i, f: (i, 0))
    vec = pl.BlockSpec((1, d), lambda i, f: (0, 0))
    return pl.pallas_call(
        functools.partial(_ffn_kernel, alpha=alpha, nf=nf),
        grid=(n // tm, nf),
        in_specs=[row,
                  _mod_spec(mod3, tm, rows_per_batch, 3, 2),
                  _mod_spec(mod3, tm, rows_per_batch, 4, 2),
                  _mod_spec(mod3, tm, rows_per_batch, 5, 2),
                  pl.BlockSpec((d, tf), lambda i, f: (0, f)),
                  pl.BlockSpec((tf, d), lambda i, f: (f, 0)),
                  vec, vec],
        out_specs=row,
        out_shape=jax.ShapeDtypeStruct((n, d), F32),
        scratch_shapes=[pltpu.VMEM((tm, d), BF16), pltpu.VMEM((tm, d), F32)],
        compiler_params=_params("arbitrary", "arbitrary"),
        name="ffn_ln2",
    )(x, mod3, mod3, mod3, w1, w2, ln_g, ln_b)


def _trunk_layer(x3, mod3, s0, cache, page_table, wts, tabs, lambda_init, alpha):
    nbatch, seq, d = x3.shape
    n = nbatch * seq
    x = x3.reshape(n, d)
    ds = wts["w_glu"].shape[0]
    da = wts["w_up_att"].shape[0]
    nh = da // (2 * HEAD_DIM)
    tm = min(512, seq) if mod3.shape[1] == 1 else min(512, n)
    tn = min(1024, ds)
    h = _ln_mod(x, mod3, seq, tm)
    w_in = wts["w_in"]
    u = _linear(h, w_in, 0, ds, tm, tn)
    q = _linear(h, w_in, ds, da, tm, tn)
    k = _linear(h, w_in, ds + da, da, tm, tn)
    v = _linear(h, w_in, ds + 2 * da, da, tm, tn)
    gates = _linear(h, w_in, ds + 3 * da, 2 * d, tm, tn)

    pd, q_re, q_im, w_t, lp_re, lp_im = tabs
    t_chunk = PROMPT_CHUNK if seq % PROMPT_CHUNK == 0 else seq
    nc = seq // t_chunk
    u_flat = u.reshape(n // t_chunk, t_chunk * ds)
    v_re, v_im = _ssm_p(u_flat, pd, t_chunk)
    gp = lp_re.shape[1] * SSM_STATE
    lam_re = lp_re[t_chunk, :, :SSM_STATE].reshape(1, gp)
    lam_im = lp_im[t_chunk, :, :SSM_STATE].reshape(1, gp)
    s_re, s_im, f_re, f_im = _ssm_scan(v_re, v_im, s0[0], s0[1], lam_re, lam_im, nbatch, nc)
    y = _ssm_y(u_flat, w_t, q_re, q_im, s_re, s_im, t_chunk).reshape(n, ds)

    lam_vecs = wts["lam_vecs"]
    if cache is None:
        slopes = jnp.asarray([2.0 ** (-8.0 * (i + 1) / nh) for i in range(nh)], F32)
        tq = min(512, seq)
        o = _attention(q, k, v, slopes, lam_vecs, wts["subln_g"], nbatch, seq, lambda_init, tq, tq)
    else:
        o = _decode_attention(q.reshape(nbatch, seq, da), k.reshape(nbatch, seq, da), v.reshape(nbatch, seq, da),
                              cache[0], cache[1], page_table, lam_vecs, wts["subln_g"], lambda_init)
        o = o.reshape(n, da)

    mix = _mix_in(y, o, gates, wts["w_glu"], wts["w_up_ssm"], wts["w_up_att"], min(256, tm))
    x1 = _out_proj(mix, x, mod3, seq, wts["w_o"], wts["ln1_g"], wts["ln1_b"], alpha, tm)
    x2 = _ffn(x1, mod3, seq, wts["w_ff1"], wts["w_ff2"], wts["ln2_g"], wts["ln2_b"], alpha, tm,
              min(1024, wts["w_ff1"].shape[1]))
    return (x2.reshape(nbatch, seq, d), k.reshape(nbatch, seq, nh, 2 * HEAD_DIM),
            v.reshape(nbatch, seq, nh, 2 * HEAD_DIM), f_re, f_im)


def kernel(x_prompt, x_sample, c_prompt, c_sample, cache_k, cache_v, state_ssm_re, state_ssm_im, page_table,
           w_ada, b_ada, w_in, ssm_a_re, ssm_a_im, ssm_log_dt, ssm_b_re, ssm_b_im, ssm_c_re, ssm_c_im, ssm_d,
           w_glu, w_up_ssm, lam_q1, lam_k1, lam_q2, lam_k2, subln_g, w_up_att, w_o, ln1_g, ln1_b, w_ff1, w_ff2,
           ln2_g, ln2_b):
    depth = w_in.shape[0]
    bp, seq_p, d = x_prompt.shape
    bs, seq_s, _ = x_sample.shape
    g, p = ssm_a_re.shape[1:]
    alpha = (2 * depth) ** 0.25
    npow = PROMPT_CHUNK + 1
    xp, xs = x_prompt, x_sample
    outs_p, outs_s = [], []
    c_rows = jnp.concatenate([c_prompt, c_sample], axis=0)
    c_rows = jnp.pad(c_rows, ((0, (-(bp + bs)) % 8), (0, 0)))
    n_pool, page, nh, hw = cache_k.shape[1:]
    for l in range(depth):
        lambda_init = 0.8 - 0.6 * math.exp(-0.3 * l)
        mod = _ada(c_rows, w_ada[l], b_ada[l].reshape(1, 6 * d), min(1024, d))
        mod_p = mod[:bp].reshape(bp, 1, 6 * d)
        mod_s = jnp.repeat(mod[bp:bp + bs], seq_s, axis=0).reshape(1, bs * seq_s, 6 * d)
        wts = {
            "w_in": w_in[l].astype(BF16), "w_glu": w_glu[l].astype(BF16),
            "w_up_ssm": w_up_ssm[l].astype(BF16), "w_up_att": w_up_att[l].astype(BF16),
            "w_o": w_o[l].astype(BF16), "w_ff1": w_ff1[l].astype(BF16), "w_ff2": w_ff2[l].astype(BF16),
            "ln1_g": ln1_g[l].reshape(1, d), "ln1_b": ln1_b[l].reshape(1, d),
            "ln2_g": ln2_g[l].reshape(1, d), "ln2_b": ln2_b[l].reshape(1, d),
            "subln_g": subln_g[l].reshape(1, 2 * HEAD_DIM),
            "lam_vecs": tuple(a[l].reshape(1, HEAD_DIM) for a in (lam_q1, lam_k1, lam_q2, lam_k2)),
        }
        tabs = _ssm_prep(ssm_a_re[l], ssm_a_im[l], ssm_log_dt[l], ssm_b_re[l], ssm_b_im[l], ssm_c_re[l],
                         ssm_c_im[l], ssm_d[l], npow)
        tabs = list(tabs)
        tabs[3] = _toeplitz_tiles(tabs[3], PROMPT_CHUNK)
        zeros = jnp.zeros((bp, g * p), F32)
        res_p = _trunk_layer(xp, mod_p, (zeros, zeros), None, None, wts, tabs, lambda_init, alpha)
        cache = (cache_k[l].reshape(n_pool, page * nh, hw), cache_v[l].reshape(n_pool, page * nh, hw))
        s0 = (state_ssm_re[l].reshape(bs, g * p), state_ssm_im[l].reshape(bs, g * p))
        res_s = _trunk_layer(xs, mod_s, s0, cache, page_table, wts, tabs, lambda_init, alpha)
        xp, xs = res_p[0], res_s[0]
        outs_p.append(res_p[1:])
        outs_s.append(res_s[1:])

    def stack(outs, i, shape=None):
        a = jnp.stack([o[i] for o in outs])
        return a if shape is None else a.reshape((depth,) + shape)

    return (xp, xs,
            stack(outs_p, 0), stack(outs_p, 1), stack(outs_p, 2, (bp, g, p)), stack(outs_p, 3, (bp, g, p)),
            stack(outs_s, 0), stack(outs_s, 1), stack(outs_s, 2, (bs, g, p)), stack(outs_s, 3, (bs, g, p)))
```

```python
import functools
import math

import jax
import jax.numpy as jnp
from jax import lax
from jax.experimental import pallas as pl
from jax.experimental.pallas import tpu as pltpu

F32 = jnp.float32
BF16 = jnp.bfloat16

SSM_GROUP = 16
SSM_STATE = 64
HEAD_DIM = 64
LANES = 128
BLOCK_W = 256
GROUPS_PER_BLOCK = BLOCK_W // SSM_GROUP
STATE_W = GROUPS_PER_BLOCK * SSM_STATE
PROMPT_CHUNK = 16
LN_EPS = 1e-5
NEG_BIG = -1e30
LOG2E = 1.4426950408889634
VMEM_LIMIT = 56 * 1024 * 1024


def _params(*sem):
    return pltpu.CompilerParams(dimension_semantics=sem, vmem_limit_bytes=VMEM_LIMIT)


def _ln(x):
    mu = jnp.mean(x, axis=-1, keepdims=True)
    xc = x - mu
    var = jnp.mean(xc * xc, axis=-1, keepdims=True)
    return xc * lax.rsqrt(var + LN_EPS)


def _dot(a, b):
    return jnp.dot(a, b, preferred_element_type=F32)


def _dot_nt(a, b):
    return lax.dot_general(a, b, (((1,), (1,)), ((), ())), preferred_element_type=F32)


def _dot_tn(a, b):
    return lax.dot_general(a, b, (((0,), (0,)), ((), ())), preferred_element_type=F32)


def _mod_spec(mod3, tm, rows_per_batch, col, ngrid):
    d = mod3.shape[2] // 6
    if mod3.shape[1] == 1:
        per = rows_per_batch // tm
        if ngrid == 1:
            return pl.BlockSpec((None, 1, d), lambda i: (i // per, 0, col))
        return pl.BlockSpec((None, 1, d), lambda i, j: (i // per, 0, col))
    if ngrid == 1:
        return pl.BlockSpec((None, tm, d), lambda i: (0, i, col))
    return pl.BlockSpec((None, tm, d), lambda i, j: (0, i, col))


def _ada_kernel(c_ref, w_ref, b_ref, o_ref):
    c = c_ref[...]
    a = (c * jax.nn.sigmoid(c)).astype(BF16)
    o_ref[...] = _dot(a, w_ref[...].astype(BF16)) + b_ref[...]


def _ada(c, w, b, tn):
    m, d = c.shape
    n = w.shape[1]
    return pl.pallas_call(
        _ada_kernel,
        grid=(n // tn,),
        in_specs=[pl.BlockSpec((m, d), lambda j: (0, 0)),
                  pl.BlockSpec((d, tn), lambda j: (0, j)),
                  pl.BlockSpec((1, tn), lambda j: (0, j))],
        out_specs=pl.BlockSpec((m, tn), lambda j: (0, j)),
        out_shape=jax.ShapeDtypeStruct((m, n), F32),
        compiler_params=_params("arbitrary"),
        name="ada_mod",
    )(c, w, b)


def _ln_mod_kernel(x_ref, sh_ref, sc_ref, h_ref):
    h_ref[...] = (_ln(x_ref[...]) * (1.0 + sc_ref[...]) + sh_ref[...]).astype(h_ref.dtype)


def _ln_mod(x, mod3, rows_per_batch, tm):
    n, d = x.shape
    return pl.pallas_call(
        _ln_mod_kernel,
        grid=(n // tm,),
        in_specs=[pl.BlockSpec((tm, d), lambda i: (i, 0)),
                  _mod_spec(mod3, tm, rows_per_batch, 0, 1),
                  _mod_spec(mod3, tm, rows_per_batch, 1, 1)],
        out_specs=pl.BlockSpec((tm, d), lambda i: (i, 0)),
        out_shape=jax.ShapeDtypeStruct((n, d), BF16),
        compiler_params=_params("arbitrary"),
        name="ln_mod",
    )(x, mod3, mod3)


def _linear_kernel(a_ref, w_ref, o_ref):
    o_ref[...] = _dot(a_ref[...], w_ref[...]).astype(o_ref.dtype)


def _linear_slab_kernel(a_ref, w_ref, o_ref):
    acc = _dot(a_ref[...], w_ref[...])
    for s in range(o_ref.shape[0]):
        o_ref[s] = acc[:, s * LANES:(s + 1) * LANES].astype(o_ref.dtype)


def _linear(a, w, col0, ncols, tm, tn, out_dtype=F32, slabs=False):
    n, k = a.shape
    cb = col0 // tn
    if slabs:
        per = tn // LANES
        out_spec = pl.BlockSpec((per, tm, LANES), lambda j, i: (j, i, 0))
        out_shape = jax.ShapeDtypeStruct((ncols // LANES, n, LANES), out_dtype)
    else:
        out_spec = pl.BlockSpec((tm, tn), lambda j, i: (i, j))
        out_shape = jax.ShapeDtypeStruct((n, ncols), out_dtype)
    return pl.pallas_call(
        _linear_slab_kernel if slabs else _linear_kernel,
        grid=(ncols // tn, n // tm),
        in_specs=[pl.BlockSpec((tm, k), lambda j, i: (i, 0)),
                  pl.BlockSpec((k, tn), lambda j, i: (0, cb + j))],
        out_specs=out_spec,
        out_shape=out_shape,
        compiler_params=_params("arbitrary", "arbitrary"),
        name="linear",
    )(a, w)


def _ssm_prep_kernel(are_ref, aim_ref, ldt_ref, bre_ref, bim_ref, cre_ref, cim_ref, d_ref,
                     pd_ref, qre_ref, qim_ref, kt_ref, lpre_ref, lpim_ref, *, npow, gt):
    c = SSM_GROUP
    p = SSM_STATE
    a_re = are_ref[...]
    a_im = aim_ref[...]
    dt = jnp.exp(ldt_ref[...])
    mag = jnp.exp(dt * a_re)
    ab_re = mag * jnp.cos(dt * a_im)
    ab_im = mag * jnp.sin(dt * a_im)
    den = a_re * a_re + a_im * a_im
    f_re = ((ab_re - 1.0) * a_re + ab_im * a_im) / den
    f_im = (ab_im * a_re - (ab_re - 1.0) * a_im) / den
    b_re = bre_ref[...]
    b_im = bim_ref[...]
    bb_re = f_re[:, None, :] * b_re - f_im[:, None, :] * b_im
    bb_im = f_re[:, None, :] * b_im + f_im[:, None, :] * b_re
    c_re = cre_ref[...]
    c_im = cim_ref[...]
    low = lax.broadcasted_iota(jnp.int32, (1, 1, 2 * p), 2) < p
    p_re = jnp.ones_like(a_re)
    p_im = jnp.zeros_like(a_re)
    g_re_all, g_im_all = [], []
    for n in range(npow):
        pr = p_re[:, None, :]
        pi = p_im[:, None, :]
        pw_re = pr * bb_re - pi * bb_im
        pw_im = pr * bb_im + pi * bb_re
        pd_ref[n] = jnp.concatenate([pw_re, pw_im], axis=-1).reshape(gt * c, 4 * p)
        g_re = c_re * pr - c_im * pi
        g_im = -(c_re * pi + c_im * pr)
        g_re_all.append(g_re)
        g_im_all.append(g_im)
        lpre_ref[n] = p_re
        lpim_ref[n] = p_im
        p_re, p_im = p_re * ab_re - p_im * ab_im, p_re * ab_im + p_im * ab_re
    g_re_all = jnp.concatenate(g_re_all, axis=1)
    g_im_all = jnp.concatenate(g_im_all, axis=1)
    for gg in range(0, gt, 2):
        sl = slice(gg // 2 * 2 * p, (gg // 2 + 1) * 2 * p)
        qre_ref[:, sl] = jnp.where(low[0], g_re_all[gg], g_re_all[gg + 1])
        qim_ref[:, sl] = jnp.where(low[0], g_im_all[gg], g_im_all[gg + 1])
    rows = kt_ref.shape[2]
    zpad = jnp.zeros((gt, rows - npow * c, 2 * p), F32)
    dn = (((2,), (2,)), ((0,), (0,)))
    kt = (lax.dot_general(jnp.where(low, bb_re, 0.0), jnp.concatenate([g_re_all, zpad], axis=1), dn,
                          precision=lax.Precision.HIGHEST, preferred_element_type=F32)
          + lax.dot_general(jnp.where(low, bb_im, 0.0), jnp.concatenate([g_im_all, zpad], axis=1), dn,
                            precision=lax.Precision.HIGHEST, preferred_element_type=F32))
    eye = (lax.broadcasted_iota(jnp.int32, (c, rows), 0) == lax.broadcasted_iota(jnp.int32, (c, rows), 1))
    kt_ref[...] = kt + jnp.where(eye[None], d_ref[...], 0.0)


def _ssm_prep(a_re, a_im, log_dt, b_re, b_im, c_re, c_im, d_skip, npow):
    g, p = a_re.shape
    c = SSM_GROUP
    gt = 8
    kt_w = -(-npow * c // LANES) * LANES
    dup = lambda a: jnp.concatenate([a, a], axis=-1)
    gp = pl.BlockSpec((gt, 2 * p), lambda i: (i, 0))
    gcp = pl.BlockSpec((gt, c, 2 * p), lambda i: (i, 0, 0))
    qspec = pl.BlockSpec((npow * c, gt * p), lambda i: (0, i))
    lspec = pl.BlockSpec((npow, gt, 2 * p), lambda i: (0, i, 0))
    return pl.pallas_call(
        functools.partial(_ssm_prep_kernel, npow=npow, gt=gt),
        grid=(g // gt,),
        in_specs=[gp, gp, pl.BlockSpec((gt, 1), lambda i: (i, 0)), gcp, gcp, gcp, gcp,
                  pl.BlockSpec((gt, c, 1), lambda i: (i, 0, 0))],
        out_specs=[pl.BlockSpec((npow, gt * c, 4 * p), lambda i: (0, i, 0)), qspec, qspec,
                   pl.BlockSpec((gt, c, kt_w), lambda i: (i, 0, 0)), lspec, lspec],
        out_shape=[jax.ShapeDtypeStruct((npow, g * c, 4 * p), F32),
                   jax.ShapeDtypeStruct((npow * c, g * p), F32),
                   jax.ShapeDtypeStruct((npow * c, g * p), F32),
                   jax.ShapeDtypeStruct((g, c, kt_w), F32),
                   jax.ShapeDtypeStruct((npow, g, 2 * p), F32),
                   jax.ShapeDtypeStruct((npow, g, 2 * p), F32)],
        compiler_params=_params("arbitrary"),
        name="ssm_prep",
    )(dup(a_re), dup(a_im), log_dt.reshape(g, 1), dup(jnp.swapaxes(b_re, 1, 2)), dup(jnp.swapaxes(b_im, 1, 2)),
      dup(c_re), dup(c_im), d_skip.reshape(g, c, 1))


def _toeplitz_kernel(kt_ref, w_ref, *, t_max):
    c = SSM_GROUP
    kw = kt_ref.shape[2]
    x = kt_ref[...].reshape(BLOCK_W, kw).astype(BF16)
    mask = _group_mask(BLOCK_W, BLOCK_W, c, c)
    sel_r = lax.broadcasted_iota(jnp.int32, (kw, BLOCK_W), 0)
    sel_c = lax.broadcasted_iota(jnp.int32, (kw, BLOCK_W), 1)
    for n in range(t_max):
        e = jnp.where((sel_r // c == n) & (sel_r % c == sel_c % c), 1.0, 0.0).astype(BF16)
        tile = jnp.where(mask, _dot(x, e), 0.0)
        w_ref[pl.ds((t_max - 1 - n) * BLOCK_W, BLOCK_W), :] = tile.astype(w_ref.dtype)


def _toeplitz_tiles(kt, t_max):
    g, c, kw = kt.shape
    gb = g // GROUPS_PER_BLOCK
    rows = t_max * BLOCK_W
    return pl.pallas_call(
        functools.partial(_toeplitz_kernel, t_max=t_max),
        grid=(gb,),
        in_specs=[pl.BlockSpec((GROUPS_PER_BLOCK, c, kw), lambda b: (b, 0, 0))],
        out_specs=pl.BlockSpec((None, rows, BLOCK_W), lambda b: (b, 0, 0)),
        out_shape=jax.ShapeDtypeStruct((gb, rows, BLOCK_W), BF16),
        compiler_params=_params("arbitrary"),
        name="ssm_toeplitz",
    )(kt)


def _group_mask(rows, cols, row_div, col_div):
    return (lax.broadcasted_iota(jnp.int32, (rows, cols), 0) // row_div
            == lax.broadcasted_iota(jnp.int32, (rows, cols), 1) // col_div)


def _ssm_kernel(u_ref, w_ref, pd_ref, qre_ref, qim_ref, s0re_ref, s0im_ref, lre_ref, lim_ref,
                y_ref, fre_ref, fim_ref, ucat_ref, v_ref, s_ref, *, t_chunk, nseq, nc):
    c = SSM_GROUP
    rb = nseq * nc
    t_max = w_ref.shape[0] // BLOCK_W
    for t in range(t_chunk):
        for s in range(BLOCK_W // LANES):
            lo = t * BLOCK_W + s * LANES
            ucat_ref[:, lo:lo + LANES] = u_ref[s, pl.ds(t, rb, stride=t_chunk), :].astype(BF16)
    mask = _group_mask(BLOCK_W, STATE_W, c, SSM_STATE)
    reps = STATE_W // LANES
    acc = jnp.zeros((rb, 2 * STATE_W), F32)
    for t in range(t_chunk):
        blk = pd_ref[t_chunk - 1 - t]
        w_re = jnp.where(mask, jnp.concatenate([blk[:, :LANES]] * reps, axis=1), 0.0)
        w_im = jnp.where(mask, jnp.concatenate([blk[:, LANES:]] * reps, axis=1), 0.0)
        w = jnp.concatenate([w_re, w_im], axis=1).astype(BF16)
        acc = acc + _dot(ucat_ref[:, t * BLOCK_W:(t + 1) * BLOCK_W], w)
    v_ref[0] = acc[:, :STATE_W]
    v_ref[1] = acc[:, STATE_W:]
    ar = lre_ref[...]
    ai = lim_ref[...]
    if nc == 1:
        sr = s0re_ref[...]
        si = s0im_ref[...]
        s_ref[0] = sr
        s_ref[1] = si
        fre_ref[...] = ar * sr - ai * si + v_ref[0]
        fim_ref[...] = ar * si + ai * sr + v_ref[1]
    else:
        def body(k, carry):
            new = []
            for b in range(nseq):
                sr, si = carry[2 * b], carry[2 * b + 1]
                row = pl.ds(b * nc + k, 1)
                s_ref[0, row, :] = sr
                s_ref[1, row, :] = si
                new.append(ar * sr - ai * si + v_ref[0, row, :])
                new.append(ar * si + ai * sr + v_ref[1, row, :])
            return tuple(new)

        init = []
        for b in range(nseq):
            init += [s0re_ref[b:b + 1, :], s0im_ref[b:b + 1, :]]
        fin = lax.fori_loop(0, nc, body, tuple(init))
        for b in range(nseq):
            fre_ref[b:b + 1, :] = fin[2 * b]
            fim_ref[b:b + 1, :] = fin[2 * b + 1]
    s_re = s_ref[0].astype(BF16)
    s_im = s_ref[1].astype(BF16)
    for t in range(t_chunk):
        rows = slice((t + 1) * c, (t + 2) * c)
        q_re = jnp.where(mask, jnp.concatenate([qre_ref[rows, :]] * GROUPS_PER_BLOCK, axis=0), 0.0).astype(BF16)
        q_im = jnp.where(mask, jnp.concatenate([qim_ref[rows, :]] * GROUPS_PER_BLOCK, axis=0), 0.0).astype(BF16)
        y = (_dot(ucat_ref[:, :(t + 1) * BLOCK_W], w_ref[(t_max - 1 - t) * BLOCK_W:, :])
             + _dot_nt(s_re, q_re) + _dot_nt(s_im, q_im))
        for s in range(BLOCK_W // LANES):
            y_ref[s, pl.ds(t, rb, stride=t_chunk), :] = y[:, s * LANES:(s + 1) * LANES]


def _ssm(u_slabs, w_t, pd, q_re, q_im, s0_re, s0_im, lam_re, lam_im, t_chunk, nb, nc):
    gb = w_t.shape[0]
    c = SSM_GROUP
    nseq = max(1, min(nb, 256 // nc))
    nblk = nb // nseq
    rb = nseq * nc
    per = BLOCK_W // LANES
    gp = gb * STATE_W
    uspec = pl.BlockSpec((per, rb * t_chunk, LANES), lambda g, i: (g, i, 0))
    qspec = pl.BlockSpec(((t_chunk + 1) * c, STATE_W), lambda g, i: (0, g))
    sspec = pl.BlockSpec((None, nseq, STATE_W), lambda g, i: (i, 0, g))
    lspec = pl.BlockSpec((1, STATE_W), lambda g, i: (0, g))
    fshape = jax.ShapeDtypeStruct((nblk, nseq, gp), F32)
    y, f_re, f_im = pl.pallas_call(
        functools.partial(_ssm_kernel, t_chunk=t_chunk, nseq=nseq, nc=nc),
        grid=(gb, nblk),
        in_specs=[uspec,
                  pl.BlockSpec((None,) + w_t.shape[1:], lambda g, i: (g, 0, 0)),
                  pl.BlockSpec((t_chunk, BLOCK_W, 2 * LANES), lambda g, i: (0, g, 0)),
                  qspec, qspec, sspec, sspec, lspec, lspec],
        out_specs=[uspec, sspec, sspec],
        out_shape=[jax.ShapeDtypeStruct(u_slabs.shape, F32), fshape, fshape],
        scratch_shapes=[pltpu.VMEM((rb, t_chunk * BLOCK_W), BF16), pltpu.VMEM((2, rb, STATE_W), F32),
                        pltpu.VMEM((2, rb, STATE_W), F32)],
        compiler_params=_params("arbitrary", "arbitrary"),
        name="ssm_chunked",
    )(u_slabs, w_t, pd, q_re, q_im, s0_re.reshape(nblk, nseq, gp), s0_im.reshape(nblk, nseq, gp), lam_re, lam_im)
    return y, f_re.reshape(nb, gp), f_im.reshape(nb, gp)


def _lambda(lq1_ref, lk1_ref, lq2_ref, lk2_ref, lambda_init):
    return (jnp.exp(jnp.sum(lq1_ref[...] * lk1_ref[...], axis=-1, keepdims=True))
            - jnp.exp(jnp.sum(lq2_ref[...] * lk2_ref[...], axis=-1, keepdims=True)) + lambda_init)


def _bias_columns(n, off, pieces, key_side):
    hw = 2 * HEAD_DIM
    lane = lax.broadcasted_iota(jnp.int32, (1, hw), 1) - off
    valid = (lane >= 0) & (lane < 4 * len(pieces))
    a = lane // 4
    kap = lane % 4
    s_lane = pieces[-1]
    for i in range(len(pieces) - 2, -1, -1):
        s_lane = jnp.where(a == i, pieces[i], s_lane)
    pos = lax.broadcasted_iota(jnp.int32, (n, 1), 0)
    hi = (pos // 64 * 64).astype(F32)
    lo = (pos % 64).astype(F32)
    if key_side:
        val = jnp.where(kap < 2, s_lane, jnp.where(kap == 2, hi, lo))
    else:
        val = jnp.where(kap == 0, -hi, jnp.where(kap == 1, -lo, s_lane))
    return jnp.where(valid, val, 0.0)


def _attn_kernel(slope_ref, q_ref, k_ref, v_ref, lq1_ref, lk1_ref, lq2_ref, lk2_ref, g_ref, o_ref,
                 rel_ref, ka_ref, qb_ref, m_ref, l_ref, acc_ref, *, tq, tk, nq, lambda_init):
    h = pl.program_id(1)
    qi = pl.program_id(2)
    hd = HEAD_DIM
    hw = 2 * hd
    slope2 = slope_ref[h] * LOG2E
    first = lax.broadcasted_iota(jnp.int32, (1, hw), 1) < hd

    @pl.when(qi == 0)
    def _():
        sv = jnp.full((1, hw), slope2, F32)
        s1 = sv.astype(BF16).astype(F32)
        s2 = (sv - s1).astype(BF16).astype(F32)
        s3 = (sv - s1 - s2).astype(BF16).astype(F32)
        pieces = (s1, s2, s3)
        qb_ref[0] = _bias_columns(tq, hd, pieces, False)
        qb_ref[1] = _bias_columns(tq, 0, pieces, False)
        ka_ref[0] = _bias_columns(tk, hd, pieces, True)
        ka_ref[1] = _bias_columns(tk, 0, pieces, True)
        rel_ref[...] = (lax.broadcasted_iota(jnp.int32, (tk, tq), 1)
                        - lax.broadcasted_iota(jnp.int32, (tk, tq), 0)).astype(F32)

    q = q_ref[...] * (hd ** -0.5 * LOG2E)
    qm = (jnp.where(first, q, qb_ref[0]).astype(BF16), jnp.where(first, qb_ref[1], q).astype(BF16))
    ones = jnp.ones((16, tk), BF16)
    m_ref[...] = jnp.full_like(m_ref, NEG_BIG)
    l_ref[...] = jnp.zeros_like(l_ref)
    acc_ref[...] = jnp.zeros_like(acc_ref)

    def scores(kj, q_blk, masked):
        k = k_ref[pl.ds(kj * tk, tk), :]
        km = (jnp.where(first, k, ka_ref[0]).astype(BF16), jnp.where(first, ka_ref[1], k).astype(BF16))
        ts = [_dot_nt(km[mp], qm[mp]) for mp in range(2)]
        if masked:
            keep = rel_ref[...] >= 0.0
            ts = [jnp.where(keep, t, -jnp.inf) for t in ts]
        return ts, -slope2 * float(q_blk * tq - kj * tk)

    def update(kj, ts, c):
        v = v_ref[pl.ds(kj * tk, tk), :].astype(BF16)
        m_old = [m_ref[mp] for mp in range(2)]
        m_new = [jnp.maximum(m_old[mp], jnp.max(ts[mp], axis=0, keepdims=True) + c) for mp in range(2)]
        ps = [jnp.exp2(ts[mp] - (m_new[mp] - c)).astype(BF16) for mp in range(2)]
        for mp in range(2):
            alpha = jnp.exp2(m_old[mp] - m_new[mp])
            l_ref[mp] = alpha * l_ref[mp] + _dot(ones, ps[mp])[0:1]
            acc_ref[mp] = alpha * acc_ref[mp] + _dot_tn(v, ps[mp])
            m_ref[mp] = m_new[mp]

    for q_blk in range(nq):
        @pl.when(qi == q_blk)
        def _(q_blk=q_blk):
            nxt = scores(0, q_blk, q_blk == 0)
            for kj in range(q_blk + 1):
                cur = nxt
                if kj < q_blk:
                    nxt = scores(kj + 1, q_blk, kj + 1 == q_blk)
                update(kj, *cur)

    lam = _lambda(lq1_ref, lk1_ref, lq2_ref, lk2_ref, lambda_init)
    o = acc_ref[0] / l_ref[0] - lam * (acc_ref[1] / l_ref[1])
    o = o * lax.rsqrt(jnp.mean(o * o, axis=0, keepdims=True) + LN_EPS) * g_ref[...] * (1.0 - lambda_init)
    o_ref[...] = o.T.astype(o_ref.dtype)


def _attention(q, k, v, slopes, lam_vecs, subln_g, nbatch, seq, lambda_init, tq, tk):
    n, da = q.shape
    nh = da // (2 * HEAD_DIM)
    nq = seq // tq
    hw = 2 * HEAD_DIM
    vec = pl.BlockSpec((1, HEAD_DIM), lambda b, h, i: (0, 0))
    return pl.pallas_call(
        functools.partial(_attn_kernel, tq=tq, tk=tk, nq=nq, lambda_init=lambda_init),
        grid=(nbatch, nh, nq),
        in_specs=[pl.BlockSpec(memory_space=pltpu.SMEM),
                  pl.BlockSpec((tq, hw), lambda b, h, i: (b * nq + i, h)),
                  pl.BlockSpec((seq, hw), lambda b, h, i: (b, h)),
                  pl.BlockSpec((seq, hw), lambda b, h, i: (b, h)),
                  vec, vec, vec, vec,
                  pl.BlockSpec((hw, 1), lambda b, h, i: (0, 0))],
        out_specs=pl.BlockSpec((tq, hw), lambda b, h, i: (b * nq + i, h)),
        out_shape=jax.ShapeDtypeStruct((n, da), BF16),
        scratch_shapes=[pltpu.VMEM((tk, tq), F32), pltpu.VMEM((2, tk, hw), F32), pltpu.VMEM((2, tq, hw), F32),
                        pltpu.VMEM((2, 1, tq), F32),
                        pltpu.VMEM((2, 1, tq), F32), pltpu.VMEM((2, hw, tq), F32)],
        compiler_params=_params("arbitrary", "arbitrary", "arbitrary"),
        name="prompt_attention",
    )(slopes, q, k, v, *lam_vecs, subln_g.reshape(hw, 1))


def _dec_attn_kernel(pt_ref, q_ref, kn_ref, vn_ref, *rest, n_pages, page, tdec, nh, lambda_init):
    del pt_ref
    k_refs = rest[:n_pages] + (kn_ref,)
    v_refs = rest[n_pages:2 * n_pages] + (vn_ref,)
    lq1_ref, lk1_ref, lq2_ref, lk2_ref, g_ref, o_ref = rest[2 * n_pages:]
    hd = HEAD_DIM
    hw = 2 * hd
    ncol = 2 * nh * tdec
    past = n_pages * page
    q = q_ref[...] * (hd ** -0.5 * LOG2E)
    rid = lax.broadcasted_iota(jnp.int32, (ncol, hw), 0)
    cid = lax.broadcasted_iota(jnp.int32, (ncol, hw), 1)
    qt = jnp.zeros((ncol, hw), F32)
    for h in range(nh):
        for i in range(tdec):
            sel = (rid // (2 * tdec) == h) & (rid % tdec == i)
            qt = jnp.where(sel, jnp.broadcast_to(q[i:i + 1, h * hw:(h + 1) * hw], (ncol, hw)), qt)
    qt = jnp.where((rid // tdec) % 2 == cid // hd, qt, 0.0)
    zq = jnp.zeros_like(qt)
    qt2 = jnp.concatenate([jnp.concatenate([qt, zq], axis=1), jnp.concatenate([zq, qt], axis=1)],
                          axis=0).astype(BF16)
    col2 = lax.broadcasted_iota(jnp.int32, (1, 2 * ncol), 1)
    second = col2 // ncol
    col = col2 % ncol
    qidx = col % tdec
    hcol = col // (2 * tdec)
    slope = jnp.exp2((hcol + 1).astype(F32) * (-8.0 / nh)) * LOG2E

    def rows(n):
        r = lax.broadcasted_iota(jnp.int32, (n, 1), 0)
        return r // nh, r % nh

    tok, hrow = rows(page * nh)
    base = jnp.where(hrow == hcol, -slope * (past + qidx - tok - second * page).astype(F32), -jnp.inf)
    tokn, hrown = rows(tdec * nh)
    basen = jnp.where((hrown == hcol) & (qidx >= tokn) & (second == 0),
                      -slope * (qidx - tokn).astype(F32), -jnp.inf)
    blocks = [(j, j + 1, base, slope * float(j * page)) for j in range(0, n_pages - 1, 2)]
    if n_pages % 2:
        blocks.append((n_pages - 1, None, jnp.where(second == 0, base, -jnp.inf), slope * float((n_pages - 1) * page)))
    blocks.append((n_pages, None, basen, jnp.zeros_like(slope)))

    def pair(refs, a, b):
        xa = refs[a][...]
        xb = refs[b][...] if b is not None else jnp.zeros_like(xa)
        return jnp.concatenate([xa, xb], axis=1).astype(BF16)

    t_list = []
    m2 = jnp.full((1, 2 * ncol), NEG_BIG, F32)
    for a, b, bias, c in blocks:
        t = _dot_nt(pair(k_refs, a, b), qt2) + bias
        m2 = jnp.maximum(m2, jnp.max(t, axis=0, keepdims=True) + c)
        t_list.append(t)
    m = jnp.maximum(m2[:, :ncol], m2[:, ncol:])
    m2 = jnp.concatenate([m, m], axis=1)
    l2 = jnp.zeros((1, 2 * ncol), F32)
    acc2 = jnp.zeros((2 * ncol, 2 * hw), F32)
    for (a, b, bias, c), t in zip(blocks, t_list):
        p = jnp.exp2(t - (m2 - c))
        l2 = l2 + jnp.sum(p, axis=0, keepdims=True)
        p = p.astype(BF16)
        v = pair(v_refs, a, b)
        if p.shape[0] < LANES:
            pad = LANES - p.shape[0]
            p = jnp.concatenate([p, jnp.zeros((pad, 2 * ncol), BF16)], axis=0)
            v = jnp.concatenate([v, jnp.zeros((pad, 2 * hw), BF16)], axis=0)
        acc2 = acc2 + _dot_tn(p, v)
    l = l2[:, :ncol] + l2[:, ncol:]
    acc = acc2[:ncol, :hw] + acc2[ncol:, hw:]
    eye = lax.broadcasted_iota(jnp.int32, (ncol, ncol), 0) == lax.broadcasted_iota(jnp.int32, (ncol, ncol), 1)
    lcol = jnp.sum(jnp.where(eye, jnp.broadcast_to(l, (ncol, ncol)), 0.0), axis=1, keepdims=True)
    acc = acc / lcol
    lam = _lambda(lq1_ref, lk1_ref, lq2_ref, lk2_ref, lambda_init)
    g = g_ref[...]
    for h in range(nh):
        blk = acc[h * 2 * tdec:(h + 1) * 2 * tdec]
        o = blk[:tdec] - lam * blk[tdec:]
        o = o * lax.rsqrt(jnp.mean(o * o, axis=-1, keepdims=True) + LN_EPS) * g * (1.0 - lambda_init)
        o_ref[:, h * hw:(h + 1) * hw] = o.astype(o_ref.dtype)


def _decode_attention(q3, k3, v3, cache_k, cache_v, page_table, lam_vecs, subln_g, lambda_init):
    nb, tdec, da = q3.shape
    n_pages = page_table.shape[1]
    rows, hw = cache_k.shape[1:]
    nh = da // hw
    page = rows // nh
    new = pl.BlockSpec((None, tdec, da), lambda b, pt: (b, 0, 0))
    newp = pl.BlockSpec((None, tdec * nh, hw), lambda b, pt: (b, 0, 0))
    pages = [pl.BlockSpec((None, rows, hw), lambda b, pt, j=j: (pt[b, j], 0, 0)) for j in range(n_pages)]
    vec = pl.BlockSpec((1, HEAD_DIM), lambda b, pt: (0, 0))
    return pl.pallas_call(
        functools.partial(_dec_attn_kernel, n_pages=n_pages, page=page, tdec=tdec, nh=nh, lambda_init=lambda_init),
        grid_spec=pltpu.PrefetchScalarGridSpec(
            num_scalar_prefetch=1,
            grid=(nb,),
            in_specs=[new, newp, newp] + pages + pages + [vec, vec, vec, vec,
                                                          pl.BlockSpec((1, hw), lambda b, pt: (0, 0))],
            out_specs=new),
        out_shape=jax.ShapeDtypeStruct((nb, tdec, da), BF16),
        compiler_params=_params("arbitrary"),
        name="decode_attention",
    )(page_table, q3, k3.reshape(nb, tdec * nh, hw), v3.reshape(nb, tdec * nh, hw),
      *([cache_k] * n_pages), *([cache_v] * n_pages), *lam_vecs, subln_g)


def _mix_in_kernel(y_ref, o_ref, gs_ref, ga_ref, wglu_ref, wus_ref, wua_ref, out_ref):
    z = jax.nn.gelu(jnp.concatenate([y_ref[i] for i in range(y_ref.shape[0])], axis=1))
    zz = z * jax.nn.sigmoid(_dot(z.astype(BF16), wglu_ref[...]))
    y_ssm = _dot(zz.astype(BF16), wus_ref[...])
    y_att = _dot(o_ref[...], wua_ref[...])
    out_ref[...] = (jax.nn.sigmoid(gs_ref[...]) * y_ssm + jax.nn.sigmoid(ga_ref[...]) * y_att).astype(out_ref.dtype)


def _mix_in(y, o, gates, w_glu, w_up_ssm, w_up_att, tm):
    nslab, n, _ = y.shape
    da = o.shape[1]
    d = w_up_ssm.shape[1]
    full = lambda a: pl.BlockSpec(a.shape, lambda i: (0, 0))
    return pl.pallas_call(
        _mix_in_kernel,
        grid=(n // tm,),
        in_specs=[pl.BlockSpec((nslab, tm, LANES), lambda i: (0, i, 0)),
                  pl.BlockSpec((tm, da), lambda i: (i, 0)),
                  pl.BlockSpec((tm, d), lambda i: (i, 0)),
                  pl.BlockSpec((tm, d), lambda i: (i, 1)),
                  full(w_glu), full(w_up_ssm), full(w_up_att)],
        out_specs=pl.BlockSpec((tm, d), lambda i: (i, 0)),
        out_shape=jax.ShapeDtypeStruct((n, d), BF16),
        compiler_params=_params("arbitrary"),
        name="branch_mix",
    )(y, o, gates, gates, w_glu, w_up_ssm, w_up_att)


def _out_proj_kernel(mix_ref, x_ref, g1_ref, wo_ref, lg_ref, lb_ref, out_ref, *, alpha):
    r = alpha * x_ref[...] + g1_ref[...] * _dot(mix_ref[...], wo_ref[...])
    out_ref[...] = _ln(r) * lg_ref[...] + lb_ref[...]


def _out_proj(mix, x, mod3, rows_per_batch, w_o, ln_g, ln_b, alpha, tm):
    n, d = x.shape
    row = pl.BlockSpec((tm, d), lambda i: (i, 0))
    vec = pl.BlockSpec((1, d), lambda i: (0, 0))
    return pl.pallas_call(
        functools.partial(_out_proj_kernel, alpha=alpha),
        grid=(n // tm,),
        in_specs=[row, row, _mod_spec(mod3, tm, rows_per_batch, 2, 1),
                  pl.BlockSpec(w_o.shape, lambda i: (0, 0)), vec, vec],
        out_specs=row,
        out_shape=jax.ShapeDtypeStruct((n, d), F32),
        compiler_params=_params("arbitrary"),
        name="out_proj_ln1",
    )(mix, x, mod3, w_o, ln_g, ln_b)


def _ffn_kernel(x_ref, sh_ref, sc_ref, g2_ref, w1_ref, w2_ref, lg_ref, lb_ref, out_ref, h_ref, acc_ref,
                *, alpha, nf):
    f = pl.program_id(1)

    @pl.when(f == 0)
    def _():
        h_ref[...] = (_ln(x_ref[...]) * (1.0 + sc_ref[...]) + sh_ref[...]).astype(h_ref.dtype)
        acc_ref[...] = jnp.zeros_like(acc_ref)

    a = jnp.maximum(_dot(h_ref[...], w1_ref[...]), 0.0)
    acc_ref[...] += _dot((a * a).astype(BF16), w2_ref[...])

    @pl.when(f == nf - 1)
    def _():
        r = alpha * x_ref[...] + g2_ref[...] * acc_ref[...]
        out_ref[...] = _ln(r) * lg_ref[...] + lb_ref[...]


def _ffn(x, mod3, rows_per_batch, w1, w2, ln_g, ln_b, alpha, tm, tf):
    n, d = x.shape
    dff = w1.shape[1]
    nf = dff // tf
    row = pl.BlockSpec((tm, d), lambda i, f: (i, 0))
    vec = pl.BlockSpec((1, d), lambda i, f: (0, 0))
    return pl.pallas_call(
        functools.partial(_ffn_kernel, alpha=alpha, nf=nf),
        grid=(n // tm, nf),
        in_specs=[row,
                  _mod_spec(mod3, tm, rows_per_batch, 3, 2),
                  _mod_spec(mod3, tm, rows_per_batch, 4, 2),
                  _mod_spec(mod3, tm, rows_per_batch, 5, 2),
                  pl.BlockSpec((d, tf), lambda i, f: (0, f)),
                  pl.BlockSpec((tf, d), lambda i, f: (f, 0)),
                  vec, vec],
        out_specs=row,
        out_shape=jax.ShapeDtypeStruct((n, d), F32),
        scratch_shapes=[pltpu.VMEM((tm, d), BF16), pltpu.VMEM((tm, d), F32)],
        compiler_params=_params("arbitrary", "arbitrary"),
        name="ffn_ln2",
    )(x, mod3, mod3, mod3, w1, w2, ln_g, ln_b)


def _trunk_layer(x3, mod3, s0, cache, page_table, wts, tabs, lambda_init, alpha):
    nbatch, seq, d = x3.shape
    n = nbatch * seq
    x = x3.reshape(n, d)
    ds = wts["w_glu"].shape[0]
    da = wts["w_up_att"].shape[0]
    nh = da // (2 * HEAD_DIM)
    tm = min(512, seq) if mod3.shape[1] == 1 else min(512, n)
    tn = min(1024, ds)
    h = _ln_mod(x, mod3, seq, tm)
    w_in = wts["w_in"]
    u = _linear(h, w_in, 0, ds, tm, tn, slabs=True)
    q = _linear(h, w_in, ds, da, tm, tn)
    k = _linear(h, w_in, ds + da, da, tm, tn)
    v = _linear(h, w_in, ds + 2 * da, da, tm, tn)
    gates = _linear(h, w_in, ds + 3 * da, 2 * d, tm, tn)

    pd, q_re, q_im, w_t, lp_re, lp_im = tabs
    t_chunk = PROMPT_CHUNK if seq % PROMPT_CHUNK == 0 else seq
    gp = lp_re.shape[1] * SSM_STATE
    lam_re = lp_re[t_chunk, :, :SSM_STATE].reshape(1, gp)
    lam_im = lp_im[t_chunk, :, :SSM_STATE].reshape(1, gp)
    y, f_re, f_im = _ssm(u, w_t, pd, q_re, q_im, s0[0], s0[1], lam_re, lam_im, t_chunk, nbatch, seq // t_chunk)

    lam_vecs = wts["lam_vecs"]
    if cache is None:
        slopes = jnp.asarray([2.0 ** (-8.0 * (i + 1) / nh) for i in range(nh)], F32)
        tq = min(512, seq)
        o = _attention(q, k, v, slopes, lam_vecs, wts["subln_g"], nbatch, seq, lambda_init, tq, tq)
    else:
        o = _decode_attention(q.reshape(nbatch, seq, da), k.reshape(nbatch, seq, da), v.reshape(nbatch, seq, da),
                              cache[0], cache[1], page_table, lam_vecs, wts["subln_g"], lambda_init)
        o = o.reshape(n, da)

    mix = _mix_in(y, o, gates, wts["w_glu"], wts["w_up_ssm"], wts["w_up_att"], min(256, tm))
    x1 = _out_proj(mix, x, mod3, seq, wts["w_o"], wts["ln1_g"], wts["ln1_b"], alpha, tm)
    x2 = _ffn(x1, mod3, seq, wts["w_ff1"], wts["w_ff2"], wts["ln2_g"], wts["ln2_b"], alpha, tm,
              min(1024, wts["w_ff1"].shape[1]))
    return (x2.reshape(nbatch, seq, d), k.reshape(nbatch, seq, nh, 2 * HEAD_DIM),
            v.reshape(nbatch, seq, nh, 2 * HEAD_DIM), f_re, f_im)


def kernel(x_prompt, x_sample, c_prompt, c_sample, cache_k, cache_v, state_ssm_re, state_ssm_im, page_table,
           w_ada, b_ada, w_in, ssm_a_re, ssm_a_im, ssm_log_dt, ssm_b_re, ssm_b_im, ssm_c_re, ssm_c_im, ssm_d,
           w_glu, w_up_ssm, lam_q1, lam_k1, lam_q2, lam_k2, subln_g, w_up_att, w_o, ln1_g, ln1_b, w_ff1, w_ff2,
           ln2_g, ln2_b):
    depth = w_in.shape[0]
    bp, seq_p, d = x_prompt.shape
    bs, seq_s, _ = x_sample.shape
    g, p = ssm_a_re.shape[1:]
    alpha = (2 * depth) ** 0.25
    npow = PROMPT_CHUNK + 1
    xp, xs = x_prompt, x_sample
    outs_p, outs_s = [], []
    c_rows = jnp.concatenate([c_prompt, c_sample], axis=0)
    c_rows = jnp.pad(c_rows, ((0, (-(bp + bs)) % 8), (0, 0)))
    n_pool, page, nh, hw = cache_k.shape[1:]
    for l in range(depth):
        lambda_init = 0.8 - 0.6 * math.exp(-0.3 * l)
        mod = _ada(c_rows, w_ada[l], b_ada[l].reshape(1, 6 * d), min(1024, d))
        mod_p = mod[:bp].reshape(bp, 1, 6 * d)
        mod_s = jnp.repeat(mod[bp:bp + bs], seq_s, axis=0).reshape(1, bs * seq_s, 6 * d)
        wts = {
            "w_in": w_in[l].astype(BF16), "w_glu": w_glu[l].astype(BF16),
            "w_up_ssm": w_up_ssm[l].astype(BF16), "w_up_att": w_up_att[l].astype(BF16),
            "w_o": w_o[l].astype(BF16), "w_ff1": w_ff1[l].astype(BF16), "w_ff2": w_ff2[l].astype(BF16),
            "ln1_g": ln1_g[l].reshape(1, d), "ln1_b": ln1_b[l].reshape(1, d),
            "ln2_g": ln2_g[l].reshape(1, d), "ln2_b": ln2_b[l].reshape(1, d),
            "subln_g": subln_g[l].reshape(1, 2 * HEAD_DIM),
            "lam_vecs": tuple(a[l].reshape(1, HEAD_DIM) for a in (lam_q1, lam_k1, lam_q2, lam_k2)),
        }
        tabs = _ssm_prep(ssm_a_re[l], ssm_a_im[l], ssm_log_dt[l], ssm_b_re[l], ssm_b_im[l], ssm_c_re[l],
                         ssm_c_im[l], ssm_d[l], npow)
        tabs = list(tabs)
        tabs[3] = _toeplitz_tiles(tabs[3], PROMPT_CHUNK)
        zeros = jnp.zeros((bp, g * p), F32)
        res_p = _trunk_layer(xp, mod_p, (zeros, zeros), None, None, wts, tabs, lambda_init, alpha)
        cache = (cache_k[l].reshape(n_pool, page * nh, hw), cache_v[l].reshape(n_pool, page * nh, hw))
        s0 = (state_ssm_re[l].reshape(bs, g * p), state_ssm_im[l].reshape(bs, g * p))
        res_s = _trunk_layer(xs, mod_s, s0, cache, page_table, wts, tabs, lambda_init, alpha)
        xp, xs = res_p[0], res_s[0]
        outs_p.append(res_p[1:])
        outs_s.append(res_s[1:])

    def stack(outs, i, shape=None):
        a = jnp.stack([o[i] for o in outs])
        return a if shape is None else a.reshape((depth,) + shape)

    return (xp, xs,
            stack(outs_p, 0), stack(outs_p, 1), stack(outs_p, 2, (bp, g, p)), stack(outs_p, 3, (bp, g, p)),
            stack(outs_s, 0), stack(outs_s, 1), stack(outs_s, 2, (bs, g, p)), stack(outs_s, 3, (bs, g, p)))
```

```python
import functools
import math

import jax
import jax.numpy as jnp
from jax import lax
from jax.experimental import pallas as pl
from jax.experimental.pallas import tpu as pltpu

F32 = jnp.float32
BF16 = jnp.bfloat16

SSM_GROUP = 16
SSM_STATE = 64
HEAD_DIM = 64
LANES = 128
BLOCK_W = 256
GROUPS_PER_BLOCK = BLOCK_W // SSM_GROUP
STATE_W = GROUPS_PER_BLOCK * SSM_STATE
PROMPT_CHUNK = 16
LN_EPS = 1e-5
NEG_BIG = -1e30
LOG2E = 1.4426950408889634
VMEM_LIMIT = 56 * 1024 * 1024


def _params(*sem):
    return pltpu.CompilerParams(dimension_semantics=sem, vmem_limit_bytes=VMEM_LIMIT)


def _ln(x):
    mu = jnp.mean(x, axis=-1, keepdims=True)
    xc = x - mu
    var = jnp.mean(xc * xc, axis=-1, keepdims=True)
    return xc * lax.rsqrt(var + LN_EPS)


def _dot(a, b):
    return jnp.dot(a, b, preferred_element_type=F32)


def _dot_nt(a, b):
    return lax.dot_general(a, b, (((1,), (1,)), ((), ())), preferred_element_type=F32)


def _dot_tn(a, b):
    return lax.dot_general(a, b, (((0,), (0,)), ((), ())), preferred_element_type=F32)


def _mod_spec(mod3, tm, rows_per_batch, col, ngrid):
    if mod3.shape[1] == 1:
        d = mod3.shape[2] // 6
        per = rows_per_batch // tm
        if ngrid == 1:
            return pl.BlockSpec((None, 1, d), lambda i: (i // per, 0, col))
        return pl.BlockSpec((None, 1, d), lambda i, j: (i // per, 0, col))
    nslab = mod3.shape[0] // 6
    if ngrid == 1:
        return pl.BlockSpec((nslab, tm, LANES), lambda i: (col, i, 0))
    return pl.BlockSpec((nslab, tm, LANES), lambda i, j: (col, i, 0))


def _mod_val(ref):
    if len(ref.shape) == 2:
        return ref[...]
    return jnp.concatenate([ref[s] for s in range(ref.shape[0])], axis=1)


def _ada_kernel(c_ref, w_ref, b_ref, o_ref, os_ref, *, nrep, seq):
    c = c_ref[...]
    a = (c * jax.nn.sigmoid(c)).astype(BF16)
    res = _dot(a, w_ref[...].astype(BF16)) + b_ref[...]
    o_ref[...] = res
    for s in range(os_ref.shape[0]):
        for t in range(seq):
            os_ref[s, pl.ds(t, nrep, stride=seq), :] = res[:nrep, s * LANES:(s + 1) * LANES]


def _ada(c, w, b, tn, nrep, seq):
    m, d = c.shape
    n = w.shape[1]
    return pl.pallas_call(
        functools.partial(_ada_kernel, nrep=nrep, seq=seq),
        grid=(n // tn,),
        in_specs=[pl.BlockSpec((m, d), lambda j: (0, 0)),
                  pl.BlockSpec((d, tn), lambda j: (0, j)),
                  pl.BlockSpec((1, tn), lambda j: (0, j))],
        out_specs=[pl.BlockSpec((m, tn), lambda j: (0, j)),
                   pl.BlockSpec((tn // LANES, nrep * seq, LANES), lambda j: (j, 0, 0))],
        out_shape=[jax.ShapeDtypeStruct((m, n), F32),
                   jax.ShapeDtypeStruct((n // LANES, nrep * seq, LANES), F32)],
        compiler_params=_params("arbitrary"),
        name="ada_mod",
    )(c, w, b)


def _in_proj_kernel(x_ref, sh_ref, sc_ref, w_ref, u_ref, q_ref, k_ref, v_ref, g_ref, h_ref):
    j = pl.program_id(1)

    @pl.when(j == 0)
    def _():
        h_ref[...] = (_ln(x_ref[...]) * (1.0 + _mod_val(sc_ref)) + _mod_val(sh_ref)).astype(h_ref.dtype)

    @pl.when(j == 0)
    def _():
        acc = _dot(h_ref[...], w_ref[...])
        for s in range(u_ref.shape[0]):
            u_ref[s] = acc[:, s * LANES:(s + 1) * LANES]

    for idx, ref in ((1, q_ref), (2, k_ref), (3, v_ref)):
        @pl.when(j == idx)
        def _(ref=ref):
            ref[...] = _dot(h_ref[...], w_ref[...])

    @pl.when(j >= 4)
    def _():
        g_ref[...] = _dot(h_ref[...], w_ref[...])


def _in_proj(x, mod3, rows_per_batch, w, width, tm):
    n, d = x.shape
    tn = width
    nt = w.shape[1] // tn
    row = pl.BlockSpec((tm, tn), lambda i, j: (i, 0))
    mat = jax.ShapeDtypeStruct((n, width), F32)
    return pl.pallas_call(
        _in_proj_kernel,
        grid=(n // tm, nt),
        in_specs=[pl.BlockSpec((tm, d), lambda i, j: (i, 0)),
                  _mod_spec(mod3, tm, rows_per_batch, 0, 2),
                  _mod_spec(mod3, tm, rows_per_batch, 1, 2),
                  pl.BlockSpec((d, tn), lambda i, j: (0, j))],
        out_specs=[pl.BlockSpec((tn // LANES, tm, LANES), lambda i, j: (0, i, 0)), row, row, row,
                   pl.BlockSpec((tm, tn), lambda i, j: (i, jnp.maximum(j - 4, 0)))],
        out_shape=[jax.ShapeDtypeStruct((width // LANES, n, LANES), F32), mat, mat, mat,
                   jax.ShapeDtypeStruct((n, (nt - 4) * tn), F32)],
        scratch_shapes=[pltpu.VMEM((tm, d), BF16)],
        compiler_params=_params("arbitrary", "arbitrary"),
        name="in_proj",
    )(x, mod3, mod3, w)


def _ssm_prep_kernel(are_ref, aim_ref, ldt_ref, bre_ref, bim_ref, cre_ref, cim_ref, d_ref,
                     pd_ref, qre_ref, qim_ref, kt_ref, lpre_ref, lpim_ref, *, npow, gt):
    c = SSM_GROUP
    p = SSM_STATE
    a_re = are_ref[...]
    a_im = aim_ref[...]
    dt = jnp.exp(ldt_ref[...])
    mag = jnp.exp(dt * a_re)
    ab_re = mag * jnp.cos(dt * a_im)
    ab_im = mag * jnp.sin(dt * a_im)
    den = a_re * a_re + a_im * a_im
    f_re = ((ab_re - 1.0) * a_re + ab_im * a_im) / den
    f_im = (ab_im * a_re - (ab_re - 1.0) * a_im) / den
    b_re = bre_ref[...]
    b_im = bim_ref[...]
    bb_re = f_re[:, None, :] * b_re - f_im[:, None, :] * b_im
    bb_im = f_re[:, None, :] * b_im + f_im[:, None, :] * b_re
    c_re = cre_ref[...]
    c_im = cim_ref[...]
    low = lax.broadcasted_iota(jnp.int32, (1, 1, 2 * p), 2) < p
    p_re = jnp.ones_like(a_re)
    p_im = jnp.zeros_like(a_re)
    g_re_all, g_im_all = [], []
    for n in range(npow):
        pr = p_re[:, None, :]
        pi = p_im[:, None, :]
        pw_re = pr * bb_re - pi * bb_im
        pw_im = pr * bb_im + pi * bb_re
        pd_ref[n] = jnp.concatenate([pw_re, pw_im], axis=-1).reshape(gt * c, 4 * p)
        g_re = c_re * pr - c_im * pi
        g_im = -(c_re * pi + c_im * pr)
        g_re_all.append(g_re)
        g_im_all.append(g_im)
        lpre_ref[n] = p_re
        lpim_ref[n] = p_im
        p_re, p_im = p_re * ab_re - p_im * ab_im, p_re * ab_im + p_im * ab_re
    g_re_all = jnp.concatenate(g_re_all, axis=1)
    g_im_all = jnp.concatenate(g_im_all, axis=1)
    for gg in range(0, gt, 2):
        sl = slice(gg // 2 * 2 * p, (gg // 2 + 1) * 2 * p)
        qre_ref[:, sl] = jnp.where(low[0], g_re_all[gg], g_re_all[gg + 1])
        qim_ref[:, sl] = jnp.where(low[0], g_im_all[gg], g_im_all[gg + 1])
    rows = kt_ref.shape[2]
    zpad = jnp.zeros((gt, rows - npow * c, 2 * p), F32)
    dn = (((2,), (2,)), ((0,), (0,)))
    kt = (lax.dot_general(jnp.where(low, bb_re, 0.0), jnp.concatenate([g_re_all, zpad], axis=1), dn,
                          precision=lax.Precision.HIGHEST, preferred_element_type=F32)
          + lax.dot_general(jnp.where(low, bb_im, 0.0), jnp.concatenate([g_im_all, zpad], axis=1), dn,
                            precision=lax.Precision.HIGHEST, preferred_element_type=F32))
    eye = (lax.broadcasted_iota(jnp.int32, (c, rows), 0) == lax.broadcasted_iota(jnp.int32, (c, rows), 1))
    kt_ref[...] = kt + jnp.where(eye[None], d_ref[...], 0.0)


def _ssm_prep(a_re, a_im, log_dt, b_re, b_im, c_re, c_im, d_skip, npow):
    g, p = a_re.shape
    c = SSM_GROUP
    gt = 8
    kt_w = -(-npow * c // LANES) * LANES
    dup = lambda a: jnp.concatenate([a, a], axis=-1)
    gp = pl.BlockSpec((gt, 2 * p), lambda i: (i, 0))
    gcp = pl.BlockSpec((gt, c, 2 * p), lambda i: (i, 0, 0))
    qspec = pl.BlockSpec((npow * c, gt * p), lambda i: (0, i))
    lspec = pl.BlockSpec((npow, gt, 2 * p), lambda i: (0, i, 0))
    return pl.pallas_call(
        functools.partial(_ssm_prep_kernel, npow=npow, gt=gt),
        grid=(g // gt,),
        in_specs=[gp, gp, pl.BlockSpec((gt, 1), lambda i: (i, 0)), gcp, gcp, gcp, gcp,
                  pl.BlockSpec((gt, c, 1), lambda i: (i, 0, 0))],
        out_specs=[pl.BlockSpec((npow, gt * c, 4 * p), lambda i: (0, i, 0)), qspec, qspec,
                   pl.BlockSpec((gt, c, kt_w), lambda i: (i, 0, 0)), lspec, lspec],
        out_shape=[jax.ShapeDtypeStruct((npow, g * c, 4 * p), F32),
                   jax.ShapeDtypeStruct((npow * c, g * p), F32),
                   jax.ShapeDtypeStruct((npow * c, g * p), F32),
                   jax.ShapeDtypeStruct((g, c, kt_w), F32),
                   jax.ShapeDtypeStruct((npow, g, 2 * p), F32),
                   jax.ShapeDtypeStruct((npow, g, 2 * p), F32)],
        compiler_params=_params("arbitrary"),
        name="ssm_prep",
    )(dup(a_re), dup(a_im), log_dt.reshape(g, 1), dup(jnp.swapaxes(b_re, 1, 2)), dup(jnp.swapaxes(b_im, 1, 2)),
      dup(c_re), dup(c_im), d_skip.reshape(g, c, 1))


def _toeplitz_kernel(kt_ref, w_ref, *, t_max):
    c = SSM_GROUP
    kw = kt_ref.shape[2]
    x = kt_ref[...].reshape(BLOCK_W, kw).astype(BF16)
    mask = _group_mask(BLOCK_W, BLOCK_W, c, c)
    sel_r = lax.broadcasted_iota(jnp.int32, (kw, BLOCK_W), 0)
    sel_c = lax.broadcasted_iota(jnp.int32, (kw, BLOCK_W), 1)
    for n in range(t_max):
        e = jnp.where((sel_r // c == n) & (sel_r % c == sel_c % c), 1.0, 0.0).astype(BF16)
        tile = jnp.where(mask, _dot(x, e), 0.0)
        w_ref[pl.ds((t_max - 1 - n) * BLOCK_W, BLOCK_W), :] = tile.astype(w_ref.dtype)


def _toeplitz_tiles(kt, t_max):
    g, c, kw = kt.shape
    gb = g // GROUPS_PER_BLOCK
    rows = t_max * BLOCK_W
    return pl.pallas_call(
        functools.partial(_toeplitz_kernel, t_max=t_max),
        grid=(gb,),
        in_specs=[pl.BlockSpec((GROUPS_PER_BLOCK, c, kw), lambda b: (b, 0, 0))],
        out_specs=pl.BlockSpec((None, rows, BLOCK_W), lambda b: (b, 0, 0)),
        out_shape=jax.ShapeDtypeStruct((gb, rows, BLOCK_W), BF16),
        compiler_params=_params("arbitrary"),
        name="ssm_toeplitz",
    )(kt)


def _group_mask(rows, cols, row_div, col_div):
    return (lax.broadcasted_iota(jnp.int32, (rows, cols), 0) // row_div
            == lax.broadcasted_iota(jnp.int32, (rows, cols), 1) // col_div)


def _ssm_kernel(u_ref, w_ref, pd_ref, qre_ref, qim_ref, s0re_ref, s0im_ref, lre_ref, lim_ref,
                y_ref, fre_ref, fim_ref, ucat_ref, v_ref, s_ref, *, t_chunk, nseq, nc):
    c = SSM_GROUP
    rb = nseq * nc
    t_max = w_ref.shape[0] // BLOCK_W
    for t in range(t_chunk):
        for s in range(BLOCK_W // LANES):
            lo = t * BLOCK_W + s * LANES
            ucat_ref[:, lo:lo + LANES] = u_ref[s, pl.ds(t, rb, stride=t_chunk), :].astype(BF16)
    mask = _group_mask(BLOCK_W, STATE_W, c, SSM_STATE)
    reps = STATE_W // LANES
    acc = jnp.zeros((rb, 2 * STATE_W), F32)
    for t in range(t_chunk):
        blk = pd_ref[t_chunk - 1 - t]
        w_re = jnp.where(mask, jnp.concatenate([blk[:, :LANES]] * reps, axis=1), 0.0)
        w_im = jnp.where(mask, jnp.concatenate([blk[:, LANES:]] * reps, axis=1), 0.0)
        w = jnp.concatenate([w_re, w_im], axis=1).astype(BF16)
        acc = acc + _dot(ucat_ref[:, t * BLOCK_W:(t + 1) * BLOCK_W], w)
    v_ref[0] = acc[:, :STATE_W]
    v_ref[1] = acc[:, STATE_W:]
    ar = lre_ref[...]
    ai = lim_ref[...]
    if nc == 1:
        sr = s0re_ref[...]
        si = s0im_ref[...]
        s_ref[0] = sr
        s_ref[1] = si
        fre_ref[...] = ar * sr - ai * si + v_ref[0]
        fim_ref[...] = ar * si + ai * sr + v_ref[1]
    else:
        def body(k, carry):
            new = []
            for b in range(nseq):
                sr, si = carry[2 * b], carry[2 * b + 1]
                row = pl.ds(b * nc + k, 1)
                s_ref[0, row, :] = sr
                s_ref[1, row, :] = si
                new.append(ar * sr - ai * si + v_ref[0, row, :])
                new.append(ar * si + ai * sr + v_ref[1, row, :])
            return tuple(new)

        init = []
        for b in range(nseq):
            init += [s0re_ref[b:b + 1, :], s0im_ref[b:b + 1, :]]
        fin = lax.fori_loop(0, nc, body, tuple(init))
        for b in range(nseq):
            fre_ref[b:b + 1, :] = fin[2 * b]
            fim_ref[b:b + 1, :] = fin[2 * b + 1]
    s_re = s_ref[0].astype(BF16)
    s_im = s_ref[1].astype(BF16)
    for t in range(t_chunk):
        rows = slice((t + 1) * c, (t + 2) * c)
        q_re = jnp.where(mask, jnp.concatenate([qre_ref[rows, :]] * GROUPS_PER_BLOCK, axis=0), 0.0).astype(BF16)
        q_im = jnp.where(mask, jnp.concatenate([qim_ref[rows, :]] * GROUPS_PER_BLOCK, axis=0), 0.0).astype(BF16)
        y = (_dot(ucat_ref[:, :(t + 1) * BLOCK_W], w_ref[(t_max - 1 - t) * BLOCK_W:, :])
             + _dot_nt(s_re, q_re) + _dot_nt(s_im, q_im))
        for s in range(BLOCK_W // LANES):
            y_ref[s, pl.ds(t, rb, stride=t_chunk), :] = y[:, s * LANES:(s + 1) * LANES]


def _ssm(u_slabs, w_t, pd, q_re, q_im, s0_re, s0_im, lam_re, lam_im, t_chunk, nb, nc):
    gb = w_t.shape[0]
    c = SSM_GROUP
    nseq = max(1, min(nb, 256 // nc))
    nblk = nb // nseq
    rb = nseq * nc
    per = BLOCK_W // LANES
    gp = gb * STATE_W
    uspec = pl.BlockSpec((per, rb * t_chunk, LANES), lambda g, i: (g, i, 0))
    qspec = pl.BlockSpec(((t_chunk + 1) * c, STATE_W), lambda g, i: (0, g))
    sspec = pl.BlockSpec((None, nseq, STATE_W), lambda g, i: (i, 0, g))
    lspec = pl.BlockSpec((1, STATE_W), lambda g, i: (0, g))
    fshape = jax.ShapeDtypeStruct((nblk, nseq, gp), F32)
    y, f_re, f_im = pl.pallas_call(
        functools.partial(_ssm_kernel, t_chunk=t_chunk, nseq=nseq, nc=nc),
        grid=(gb, nblk),
        in_specs=[uspec,
                  pl.BlockSpec((None,) + w_t.shape[1:], lambda g, i: (g, 0, 0)),
                  pl.BlockSpec((t_chunk, BLOCK_W, 2 * LANES), lambda g, i: (0, g, 0)),
                  qspec, qspec, sspec, sspec, lspec, lspec],
        out_specs=[uspec, sspec, sspec],
        out_shape=[jax.ShapeDtypeStruct(u_slabs.shape, F32), fshape, fshape],
        scratch_shapes=[pltpu.VMEM((rb, t_chunk * BLOCK_W), BF16), pltpu.VMEM((2, rb, STATE_W), F32),
                        pltpu.VMEM((2, rb, STATE_W), F32)],
        compiler_params=_params("arbitrary", "arbitrary"),
        name="ssm_chunked",
    )(u_slabs, w_t, pd, q_re, q_im, s0_re.reshape(nblk, nseq, gp), s0_im.reshape(nblk, nseq, gp), lam_re, lam_im)
    return y, f_re.reshape(nb, gp), f_im.reshape(nb, gp)


def _lambda(lq1_ref, lk1_ref, lq2_ref, lk2_ref, lambda_init):
    return (jnp.exp(jnp.sum(lq1_ref[...] * lk1_ref[...], axis=-1, keepdims=True))
            - jnp.exp(jnp.sum(lq2_ref[...] * lk2_ref[...], axis=-1, keepdims=True)) + lambda_init)


def _bias_columns(n, off, pieces, key_side):
    hw = 2 * HEAD_DIM
    lane = lax.broadcasted_iota(jnp.int32, (1, hw), 1) - off
    valid = (lane >= 0) & (lane < 4 * len(pieces))
    a = lane // 4
    kap = lane % 4
    s_lane = pieces[-1]
    for i in range(len(pieces) - 2, -1, -1):
        s_lane = jnp.where(a == i, pieces[i], s_lane)
    pos = lax.broadcasted_iota(jnp.int32, (n, 1), 0)
    hi = (pos // 64 * 64).astype(F32)
    lo = (pos % 64).astype(F32)
    if key_side:
        val = jnp.where(kap < 2, s_lane, jnp.where(kap == 2, hi, lo))
    else:
        val = jnp.where(kap == 0, -hi, jnp.where(kap == 1, -lo, s_lane))
    return jnp.where(valid, val, 0.0)


def _attn_kernel(slope_ref, q_ref, k_ref, v_ref, lq1_ref, lk1_ref, lq2_ref, lk2_ref, g_ref, o_ref,
                 rel_ref, ka_ref, qb_ref, m_ref, l_ref, acc_ref, *, tq, tk, nq, lambda_init):
    h = pl.program_id(1)
    qi = pl.program_id(2)
    hd = HEAD_DIM
    hw = 2 * hd
    slope2 = slope_ref[h] * LOG2E
    first = lax.broadcasted_iota(jnp.int32, (1, hw), 1) < hd

    @pl.when(qi == 0)
    def _():
        sv = jnp.full((1, hw), slope2, F32)
        s1 = sv.astype(BF16).astype(F32)
        s2 = (sv - s1).astype(BF16).astype(F32)
        s3 = (sv - s1 - s2).astype(BF16).astype(F32)
        pieces = (s1, s2, s3)
        qb_ref[0] = _bias_columns(tq, hd, pieces, False)
        qb_ref[1] = _bias_columns(tq, 0, pieces, False)
        ka_ref[0] = _bias_columns(tk, hd, pieces, True)
        ka_ref[1] = _bias_columns(tk, 0, pieces, True)
        rel_ref[...] = (lax.broadcasted_iota(jnp.int32, (tk, tq), 1)
                        - lax.broadcasted_iota(jnp.int32, (tk, tq), 0)).astype(F32)

    q = q_ref[...] * (hd ** -0.5 * LOG2E)
    qm = (jnp.where(first, q, qb_ref[0]).astype(BF16), jnp.where(first, qb_ref[1], q).astype(BF16))
    ones = jnp.ones((16, tk), BF16)
    m_ref[...] = jnp.full_like(m_ref, NEG_BIG)
    l_ref[...] = jnp.zeros_like(l_ref)
    acc_ref[...] = jnp.zeros_like(acc_ref)

    def scores(kj, q_blk, masked):
        k = k_ref[pl.ds(kj * tk, tk), :]
        km = (jnp.where(first, k, ka_ref[0]).astype(BF16), jnp.where(first, ka_ref[1], k).astype(BF16))
        ts = [_dot_nt(km[mp], qm[mp]) for mp in range(2)]
        if masked:
            keep = rel_ref[...] >= 0.0
            ts = [jnp.where(keep, t, -jnp.inf) for t in ts]
        return ts, -slope2 * float(q_blk * tq - kj * tk)

    def update(kj, ts, c):
        v = v_ref[pl.ds(kj * tk, tk), :].astype(BF16)
        m_old = [m_ref[mp] for mp in range(2)]
        m_new = [jnp.maximum(m_old[mp], jnp.max(ts[mp], axis=0, keepdims=True) + c) for mp in range(2)]
        ps = [jnp.exp2(ts[mp] - (m_new[mp] - c)).astype(BF16) for mp in range(2)]
        for mp in range(2):
            alpha = jnp.exp2(m_old[mp] - m_new[mp])
            l_ref[mp] = alpha * l_ref[mp] + _dot(ones, ps[mp])[0:1]
            acc_ref[mp] = alpha * acc_ref[mp] + _dot_tn(v, ps[mp])
            m_ref[mp] = m_new[mp]

    for q_blk in range(nq):
        @pl.when(qi == q_blk)
        def _(q_blk=q_blk):
            nxt = scores(0, q_blk, q_blk == 0)
            for kj in range(q_blk + 1):
                cur = nxt
                if kj < q_blk:
                    nxt = scores(kj + 1, q_blk, kj + 1 == q_blk)
                update(kj, *cur)

    lam = _lambda(lq1_ref, lk1_ref, lq2_ref, lk2_ref, lambda_init)
    o = acc_ref[0] / l_ref[0] - lam * (acc_ref[1] / l_ref[1])
    o = o * lax.rsqrt(jnp.mean(o * o, axis=0, keepdims=True) + LN_EPS) * g_ref[...] * (1.0 - lambda_init)
    o_ref[...] = o.T.astype(o_ref.dtype)


def _attention(q, k, v, slopes, lam_vecs, subln_g, nbatch, seq, lambda_init, tq, tk):
    n, da = q.shape
    nh = da // (2 * HEAD_DIM)
    nq = seq // tq
    hw = 2 * HEAD_DIM
    vec = pl.BlockSpec((1, HEAD_DIM), lambda b, h, i: (0, 0))
    return pl.pallas_call(
        functools.partial(_attn_kernel, tq=tq, tk=tk, nq=nq, lambda_init=lambda_init),
        grid=(nbatch, nh, nq),
        in_specs=[pl.BlockSpec(memory_space=pltpu.SMEM),
                  pl.BlockSpec((tq, hw), lambda b, h, i: (b * nq + i, h)),
                  pl.BlockSpec((seq, hw), lambda b, h, i: (b, h)),
                  pl.BlockSpec((seq, hw), lambda b, h, i: (b, h)),
                  vec, vec, vec, vec,
                  pl.BlockSpec((hw, 1), lambda b, h, i: (0, 0))],
        out_specs=pl.BlockSpec((tq, hw), lambda b, h, i: (b * nq + i, h)),
        out_shape=jax.ShapeDtypeStruct((n, da), BF16),
        scratch_shapes=[pltpu.VMEM((tk, tq), F32), pltpu.VMEM((2, tk, hw), F32), pltpu.VMEM((2, tq, hw), F32),
                        pltpu.VMEM((2, 1, tq), F32),
                        pltpu.VMEM((2, 1, tq), F32), pltpu.VMEM((2, hw, tq), F32)],
        compiler_params=_params("arbitrary", "arbitrary", "arbitrary"),
        name="prompt_attention",
    )(slopes, q, k, v, *lam_vecs, subln_g.reshape(hw, 1))


def _dec_attn_kernel(pt_ref, q_ref, kn_ref, vn_ref, *rest, n_pages, page, tdec, nh, lambda_init):
    del pt_ref
    k_refs = rest[:n_pages] + (kn_ref,)
    v_refs = rest[n_pages:2 * n_pages] + (vn_ref,)
    lq1_ref, lk1_ref, lq2_ref, lk2_ref, g_ref, o_ref = rest[2 * n_pages:]
    hd = HEAD_DIM
    hw = 2 * hd
    ncol = 2 * nh * tdec
    past = n_pages * page
    q = q_ref[...] * (hd ** -0.5 * LOG2E)
    rid = lax.broadcasted_iota(jnp.int32, (ncol, hw), 0)
    cid = lax.broadcasted_iota(jnp.int32, (ncol, hw), 1)
    qt = jnp.zeros((ncol, hw), F32)
    for h in range(nh):
        for i in range(tdec):
            sel = (rid // (2 * tdec) == h) & (rid % tdec == i)
            qt = jnp.where(sel, jnp.broadcast_to(q[i:i + 1, h * hw:(h + 1) * hw], (ncol, hw)), qt)
    qt = jnp.where((rid // tdec) % 2 == cid // hd, qt, 0.0)
    zq = jnp.zeros_like(qt)
    qt2 = jnp.concatenate([jnp.concatenate([qt, zq], axis=1), jnp.concatenate([zq, qt], axis=1)],
                          axis=0).astype(BF16)
    col2 = lax.broadcasted_iota(jnp.int32, (1, 2 * ncol), 1)
    second = col2 // ncol
    col = col2 % ncol
    qidx = col % tdec
    hcol = col // (2 * tdec)
    slope = jnp.exp2((hcol + 1).astype(F32) * (-8.0 / nh)) * LOG2E

    def rows(n):
        r = lax.broadcasted_iota(jnp.int32, (n, 1), 0)
        return r // nh, r % nh

    tok, hrow = rows(page * nh)
    base = jnp.where(hrow == hcol, -slope * (past + qidx - tok - second * page).astype(F32), -jnp.inf)
    tokn, hrown = rows(tdec * nh)
    basen = jnp.where((hrown == hcol) & (qidx >= tokn) & (second == 0),
                      -slope * (qidx - tokn).astype(F32), -jnp.inf)
    blocks = [(j, j + 1, base, slope * float(j * page)) for j in range(0, n_pages - 1, 2)]
    if n_pages % 2:
        blocks.append((n_pages - 1, None, jnp.where(second == 0, base, -jnp.inf), slope * float((n_pages - 1) * page)))
    blocks.append((n_pages, None, basen, jnp.zeros_like(slope)))

    def pair(refs, a, b):
        xa = refs[a][...]
        xb = refs[b][...] if b is not None else jnp.zeros_like(xa)
        return jnp.concatenate([xa, xb], axis=1).astype(BF16)

    t_list = []
    m2 = jnp.full((1, 2 * ncol), NEG_BIG, F32)
    for a, b, bias, c in blocks:
        t = _dot_nt(pair(k_refs, a, b), qt2) + bias
        m2 = jnp.maximum(m2, jnp.max(t, axis=0, keepdims=True) + c)
        t_list.append(t)
    m = jnp.maximum(m2[:, :ncol], m2[:, ncol:])
    m2 = jnp.concatenate([m, m], axis=1)
    l2 = jnp.zeros((1, 2 * ncol), F32)
    acc2 = jnp.zeros((2 * ncol, 2 * hw), F32)
    for (a, b, bias, c), t in zip(blocks, t_list):
        p = jnp.exp2(t - (m2 - c))
        l2 = l2 + jnp.sum(p, axis=0, keepdims=True)
        p = p.astype(BF16)
        v = pair(v_refs, a, b)
        if p.shape[0] < LANES:
            pad = LANES - p.shape[0]
            p = jnp.concatenate([p, jnp.zeros((pad, 2 * ncol), BF16)], axis=0)
            v = jnp.concatenate([v, jnp.zeros((pad, 2 * hw), BF16)], axis=0)
        acc2 = acc2 + _dot_tn(p, v)
    l = l2[:, :ncol] + l2[:, ncol:]
    acc = acc2[:ncol, :hw] + acc2[ncol:, hw:]
    eye = lax.broadcasted_iota(jnp.int32, (ncol, ncol), 0) == lax.broadcasted_iota(jnp.int32, (ncol, ncol), 1)
    lcol = jnp.sum(jnp.where(eye, jnp.broadcast_to(l, (ncol, ncol)), 0.0), axis=1, keepdims=True)
    acc = acc / lcol
    lam = _lambda(lq1_ref, lk1_ref, lq2_ref, lk2_ref, lambda_init)
    g = g_ref[...]
    for h in range(nh):
        blk = acc[h * 2 * tdec:(h + 1) * 2 * tdec]
        o = blk[:tdec] - lam * blk[tdec:]
        o = o * lax.rsqrt(jnp.mean(o * o, axis=-1, keepdims=True) + LN_EPS) * g * (1.0 - lambda_init)
        o_ref[:, h * hw:(h + 1) * hw] = o.astype(o_ref.dtype)


def _decode_attention(q3, k3, v3, cache_k, cache_v, page_table, lam_vecs, subln_g, lambda_init):
    nb, tdec, da = q3.shape
    n_pages = page_table.shape[1]
    rows, hw = cache_k.shape[1:]
    nh = da // hw
    page = rows // nh
    new = pl.BlockSpec((None, tdec, da), lambda b, pt: (b, 0, 0))
    newp = pl.BlockSpec((None, tdec * nh, hw), lambda b, pt: (b, 0, 0))
    pages = [pl.BlockSpec((None, rows, hw), lambda b, pt, j=j: (pt[b, j], 0, 0)) for j in range(n_pages)]
    vec = pl.BlockSpec((1, HEAD_DIM), lambda b, pt: (0, 0))
    return pl.pallas_call(
        functools.partial(_dec_attn_kernel, n_pages=n_pages, page=page, tdec=tdec, nh=nh, lambda_init=lambda_init),
        grid_spec=pltpu.PrefetchScalarGridSpec(
            num_scalar_prefetch=1,
            grid=(nb,),
            in_specs=[new, newp, newp] + pages + pages + [vec, vec, vec, vec,
                                                          pl.BlockSpec((1, hw), lambda b, pt: (0, 0))],
            out_specs=new),
        out_shape=jax.ShapeDtypeStruct((nb, tdec, da), BF16),
        compiler_params=_params("arbitrary"),
        name="decode_attention",
    )(page_table, q3, k3.reshape(nb, tdec * nh, hw), v3.reshape(nb, tdec * nh, hw),
      *([cache_k] * n_pages), *([cache_v] * n_pages), *lam_vecs, subln_g)


def _mix_in_kernel(y_ref, o_ref, gs_ref, ga_ref, wglu_ref, wus_ref, wua_ref, out_ref):
    z = jax.nn.gelu(jnp.concatenate([y_ref[i] for i in range(y_ref.shape[0])], axis=1))
    zz = z * jax.nn.sigmoid(_dot(z.astype(BF16), wglu_ref[...]))
    y_ssm = _dot(zz.astype(BF16), wus_ref[...])
    y_att = _dot(o_ref[...], wua_ref[...])
    out_ref[...] = (jax.nn.sigmoid(gs_ref[...]) * y_ssm + jax.nn.sigmoid(ga_ref[...]) * y_att).astype(out_ref.dtype)


def _mix_in(y, o, gates, w_glu, w_up_ssm, w_up_att, tm):
    nslab, n, _ = y.shape
    da = o.shape[1]
    d = w_up_ssm.shape[1]
    full = lambda a: pl.BlockSpec(a.shape, lambda i: (0, 0))
    return pl.pallas_call(
        _mix_in_kernel,
        grid=(n // tm,),
        in_specs=[pl.BlockSpec((nslab, tm, LANES), lambda i: (0, i, 0)),
                  pl.BlockSpec((tm, da), lambda i: (i, 0)),
                  pl.BlockSpec((tm, d), lambda i: (i, 0)),
                  pl.BlockSpec((tm, d), lambda i: (i, 1)),
                  full(w_glu), full(w_up_ssm), full(w_up_att)],
        out_specs=pl.BlockSpec((tm, d), lambda i: (i, 0)),
        out_shape=jax.ShapeDtypeStruct((n, d), BF16),
        compiler_params=_params("arbitrary"),
        name="branch_mix",
    )(y, o, gates, gates, w_glu, w_up_ssm, w_up_att)


def _out_proj_kernel(mix_ref, x_ref, g1_ref, wo_ref, lg_ref, lb_ref, out_ref, *, alpha):
    r = alpha * x_ref[...] + _mod_val(g1_ref) * _dot(mix_ref[...], wo_ref[...])
    out_ref[...] = _ln(r) * lg_ref[...] + lb_ref[...]


def _out_proj(mix, x, mod3, rows_per_batch, w_o, ln_g, ln_b, alpha, tm):
    n, d = x.shape
    row = pl.BlockSpec((tm, d), lambda i: (i, 0))
    vec = pl.BlockSpec((1, d), lambda i: (0, 0))
    return pl.pallas_call(
        functools.partial(_out_proj_kernel, alpha=alpha),
        grid=(n // tm,),
        in_specs=[row, row, _mod_spec(mod3, tm, rows_per_batch, 2, 1),
                  pl.BlockSpec(w_o.shape, lambda i: (0, 0)), vec, vec],
        out_specs=row,
        out_shape=jax.ShapeDtypeStruct((n, d), F32),
        compiler_params=_params("arbitrary"),
        name="out_proj_ln1",
    )(mix, x, mod3, w_o, ln_g, ln_b)


def _ffn_kernel(x_ref, sh_ref, sc_ref, g2_ref, w1_ref, w2_ref, lg_ref, lb_ref, out_ref, h_ref, acc_ref,
                *, alpha, nf):
    f = pl.program_id(1)

    @pl.when(f == 0)
    def _():
        h_ref[...] = (_ln(x_ref[...]) * (1.0 + _mod_val(sc_ref)) + _mod_val(sh_ref)).astype(h_ref.dtype)
        acc_ref[...] = jnp.zeros_like(acc_ref)

    a = jnp.maximum(_dot(h_ref[...], w1_ref[...]), 0.0)
    acc_ref[...] += _dot((a * a).astype(BF16), w2_ref[...])

    @pl.when(f == nf - 1)
    def _():
        r = alpha * x_ref[...] + _mod_val(g2_ref) * acc_ref[...]
        out_ref[...] = _ln(r) * lg_ref[...] + lb_ref[...]


def _ffn(x, mod3, rows_per_batch, w1, w2, ln_g, ln_b, alpha, tm, tf):
    n, d = x.shape
    dff = w1.shape[1]
    nf = dff // tf
    row = pl.BlockSpec((tm, d), lambda i, f: (i, 0))
    vec = pl.BlockSpec((1, d), lambda i, f: (0, 0))
    return pl.pallas_call(
        functools.partial(_ffn_kernel, alpha=alpha, nf=nf),
        grid=(n // tm, nf),
        in_specs=[row,
                  _mod_spec(mod3, tm, rows_per_batch, 3, 2),
                  _mod_spec(mod3, tm, rows_per_batch, 4, 2),
                  _mod_spec(mod3, tm, rows_per_batch, 5, 2),
                  pl.BlockSpec((d, tf), lambda i, f: (0, f)),
                  pl.BlockSpec((tf, d), lambda i, f: (f, 0)),
                  vec, vec],
        out_specs=row,
        out_shape=jax.ShapeDtypeStruct((n, d), F32),
        scratch_shapes=[pltpu.VMEM((tm, d), BF16), pltpu.VMEM((tm, d), F32)],
        compiler_params=_params("arbitrary", "arbitrary"),
        name="ffn_ln2",
    )(x, mod3, mod3, mod3, w1, w2, ln_g, ln_b)


def _trunk_layer(x3, mod3, s0, cache, page_table, wts, tabs, lambda_init, alpha):
    nbatch, seq, d = x3.shape
    n = nbatch * seq
    x = x3.reshape(n, d)
    ds = wts["w_glu"].shape[0]
    da = wts["w_up_att"].shape[0]
    nh = da // (2 * HEAD_DIM)
    tm = min(512, seq) if mod3.shape[1] == 1 else min(256, n)
    assert ds == da
    u, q, k, v, gates = _in_proj(x, mod3, seq, wts["w_in"], ds, tm)

    pd, q_re, q_im, w_t, lp_re, lp_im = tabs
    t_chunk = PROMPT_CHUNK if seq % PROMPT_CHUNK == 0 else seq
    gp = lp_re.shape[1] * SSM_STATE
    lam_re = lp_re[t_chunk, :, :SSM_STATE].reshape(1, gp)
    lam_im = lp_im[t_chunk, :, :SSM_STATE].reshape(1, gp)
    y, f_re, f_im = _ssm(u, w_t, pd, q_re, q_im, s0[0], s0[1], lam_re, lam_im, t_chunk, nbatch, seq // t_chunk)

    lam_vecs = wts["lam_vecs"]
    if cache is None:
        slopes = jnp.asarray([2.0 ** (-8.0 * (i + 1) / nh) for i in range(nh)], F32)
        tq = min(512, seq)
        o = _attention(q, k, v, slopes, lam_vecs, wts["subln_g"], nbatch, seq, lambda_init, tq, tq)
    else:
        o = _decode_attention(q.reshape(nbatch, seq, da), k.reshape(nbatch, seq, da), v.reshape(nbatch, seq, da),
                              cache[0], cache[1], page_table, lam_vecs, wts["subln_g"], lambda_init)
        o = o.reshape(n, da)

    mix = _mix_in(y, o, gates, wts["w_glu"], wts["w_up_ssm"], wts["w_up_att"], min(256, tm))
    x1 = _out_proj(mix, x, mod3, seq, wts["w_o"], wts["ln1_g"], wts["ln1_b"], alpha, tm)
    x2 = _ffn(x1, mod3, seq, wts["w_ff1"], wts["w_ff2"], wts["ln2_g"], wts["ln2_b"], alpha, tm,
              min(1024, wts["w_ff1"].shape[1]))
    return (x2.reshape(nbatch, seq, d), k.reshape(nbatch, seq, nh, 2 * HEAD_DIM),
            v.reshape(nbatch, seq, nh, 2 * HEAD_DIM), f_re, f_im)


def kernel(x_prompt, x_sample, c_prompt, c_sample, cache_k, cache_v, state_ssm_re, state_ssm_im, page_table,
           w_ada, b_ada, w_in, ssm_a_re, ssm_a_im, ssm_log_dt, ssm_b_re, ssm_b_im, ssm_c_re, ssm_c_im, ssm_d,
           w_glu, w_up_ssm, lam_q1, lam_k1, lam_q2, lam_k2, subln_g, w_up_att, w_o, ln1_g, ln1_b, w_ff1, w_ff2,
           ln2_g, ln2_b):
    depth = w_in.shape[0]
    bp, seq_p, d = x_prompt.shape
    bs, seq_s, _ = x_sample.shape
    g, p = ssm_a_re.shape[1:]
    alpha = (2 * depth) ** 0.25
    npow = PROMPT_CHUNK + 1
    xp, xs = x_prompt, x_sample
    outs_p, outs_s = [], []
    c_rows = jnp.concatenate([c_sample, c_prompt], axis=0)
    c_rows = jnp.pad(c_rows, ((0, (-(bp + bs)) % 8), (0, 0)))
    n_pool, page, nh, hw = cache_k.shape[1:]
    for l in range(depth):
        lambda_init = 0.8 - 0.6 * math.exp(-0.3 * l)
        mod, mod_s = _ada(c_rows, w_ada[l], b_ada[l].reshape(1, 6 * d), min(1024, d), bs, seq_s)
        mod_p = mod[bs:bs + bp].reshape(bp, 1, 6 * d)
        wts = {
            "w_in": w_in[l].astype(BF16), "w_glu": w_glu[l].astype(BF16),
            "w_up_ssm": w_up_ssm[l].astype(BF16), "w_up_att": w_up_att[l].astype(BF16),
            "w_o": w_o[l].astype(BF16), "w_ff1": w_ff1[l].astype(BF16), "w_ff2": w_ff2[l].astype(BF16),
            "ln1_g": ln1_g[l].reshape(1, d), "ln1_b": ln1_b[l].reshape(1, d),
            "ln2_g": ln2_g[l].reshape(1, d), "ln2_b": ln2_b[l].reshape(1, d),
            "subln_g": subln_g[l].reshape(1, 2 * HEAD_DIM),
            "lam_vecs": tuple(a[l].reshape(1, HEAD_DIM) for a in (lam_q1, lam_k1, lam_q2, lam_k2)),
        }
        tabs = _ssm_prep(ssm_a_re[l], ssm_a_im[l], ssm_log_dt[l], ssm_b_re[l], ssm_b_im[l], ssm_c_re[l],
                         ssm_c_im[l], ssm_d[l], npow)
        tabs = list(tabs)
        tabs[3] = _toeplitz_tiles(tabs[3], PROMPT_CHUNK)
        zeros = jnp.zeros((bp, g * p), F32)
        res_p = _trunk_layer(xp, mod_p, (zeros, zeros), None, None, wts, tabs, lambda_init, alpha)
        cache = (cache_k[l].reshape(n_pool, page * nh, hw), cache_v[l].reshape(n_pool, page * nh, hw))
        s0 = (state_ssm_re[l].reshape(bs, g * p), state_ssm_im[l].reshape(bs, g * p))
        res_s = _trunk_layer(xs, mod_s, s0, cache, page_table, wts, tabs, lambda_init, alpha)
        xp, xs = res_p[0], res_s[0]
        outs_p.append(res_p[1:])
        outs_s.append(res_s[1:])

    def stack(outs, i, shape=None):
        a = jnp.stack([o[i] for o in outs])
        return a if shape is None else a.reshape((depth,) + shape)

    return (xp, xs,
            stack(outs_p, 0), stack(outs_p, 1), stack(outs_p, 2, (bp, g, p)), stack(outs_p, 3, (bp, g, p)),
            stack(outs_s, 0), stack(outs_s, 1), stack(outs_s, 2, (bs, g, p)), stack(outs_s, 3, (bs, g, p)))
```

```python
import functools
import math

import jax
import jax.numpy as jnp
from jax import lax
from jax.experimental import pallas as pl
from jax.experimental.pallas import tpu as pltpu

F32 = jnp.float32
BF16 = jnp.bfloat16

SSM_GROUP = 16
SSM_STATE = 64
HEAD_DIM = 64
LANES = 128
BLOCK_W = 256
GROUPS_PER_BLOCK = BLOCK_W // SSM_GROUP
STATE_W = GROUPS_PER_BLOCK * SSM_STATE
PROMPT_CHUNK = 16
LN_EPS = 1e-5
NEG_BIG = -1e30
LOG2E = 1.4426950408889634
VMEM_LIMIT = 56 * 1024 * 1024


def _params(*sem):
    return pltpu.CompilerParams(dimension_semantics=sem, vmem_limit_bytes=VMEM_LIMIT)


def _ln(x):
    mu = jnp.mean(x, axis=-1, keepdims=True)
    xc = x - mu
    var = jnp.mean(xc * xc, axis=-1, keepdims=True)
    return xc * lax.rsqrt(var + LN_EPS)


def _dot(a, b):
    return jnp.dot(a, b, preferred_element_type=F32)


def _dot_nt(a, b):
    return lax.dot_general(a, b, (((1,), (1,)), ((), ())), preferred_element_type=F32)


def _dot_tn(a, b):
    return lax.dot_general(a, b, (((0,), (0,)), ((), ())), preferred_element_type=F32)


def _mod_spec(mod3, tm, rows_per_batch, col, ngrid):
    if mod3.shape[1] == 1:
        d = mod3.shape[2] // 6
        per = rows_per_batch // tm
        if ngrid == 1:
            return pl.BlockSpec((None, 1, d), lambda i: (i // per, 0, col))
        return pl.BlockSpec((None, 1, d), lambda i, j: (i // per, 0, col))
    nslab = mod3.shape[0] // 6
    once = pl.Buffered(1)
    if ngrid == 1:
        return pl.BlockSpec((nslab, tm, LANES), lambda i: (col, i, 0), pipeline_mode=once)
    return pl.BlockSpec((nslab, tm, LANES), lambda i, j: (col, i, 0), pipeline_mode=once)


def _mod_val(ref):
    if len(ref.shape) == 2:
        return ref[...]
    return jnp.concatenate([ref[s] for s in range(ref.shape[0])], axis=1)


def _ada_kernel(c_ref, w_ref, b_ref, o_ref, os_ref, *, nrep, seq):
    c = c_ref[...]
    a = (c * jax.nn.sigmoid(c)).astype(BF16)
    res = _dot(a, w_ref[...].astype(BF16)) + b_ref[...]
    o_ref[...] = res
    for s in range(os_ref.shape[0]):
        for t in range(seq):
            os_ref[s, pl.ds(t, nrep, stride=seq), :] = res[:nrep, s * LANES:(s + 1) * LANES]


def _ada(c, w, b, tn, nrep, seq):
    m, d = c.shape
    n = w.shape[1]
    return pl.pallas_call(
        functools.partial(_ada_kernel, nrep=nrep, seq=seq),
        grid=(n // tn,),
        in_specs=[pl.BlockSpec((m, d), lambda j: (0, 0)),
                  pl.BlockSpec((d, tn), lambda j: (0, j)),
                  pl.BlockSpec((1, tn), lambda j: (0, j))],
        out_specs=[pl.BlockSpec((m, tn), lambda j: (0, j)),
                   pl.BlockSpec((tn // LANES, nrep * seq, LANES), lambda j: (j, 0, 0))],
        out_shape=[jax.ShapeDtypeStruct((m, n), F32),
                   jax.ShapeDtypeStruct((n // LANES, nrep * seq, LANES), F32)],
        compiler_params=_params("arbitrary"),
        name="ada_mod",
    )(c, w, b)


def _in_proj_kernel(x_ref, sh_ref, sc_ref, w_ref, u_ref, q_ref, k_ref, v_ref, g_ref, *rest):
    j = pl.program_id(1)
    h_ref = rest[-1]
    if len(rest) == 2:
        wb_ref = rest[0]
        wb_ref[...] = w_ref[...].astype(wb_ref.dtype)
    else:
        wb_ref = w_ref

    @pl.when(j == 0)
    def _():
        h_ref[...] = (_ln(x_ref[...]) * (1.0 + _mod_val(sc_ref)) + _mod_val(sh_ref)).astype(h_ref.dtype)

    @pl.when(j == 0)
    def _():
        acc = _dot(h_ref[...], wb_ref[...])
        for s in range(u_ref.shape[0]):
            u_ref[s] = acc[:, s * LANES:(s + 1) * LANES]

    for idx, ref in ((1, q_ref), (2, k_ref), (3, v_ref)):
        @pl.when(j == idx)
        def _(ref=ref):
            ref[...] = _dot(h_ref[...], wb_ref[...])

    @pl.when(j >= 4)
    def _():
        g_ref[...] = _dot(h_ref[...], wb_ref[...])


def _in_proj(x, mod3, rows_per_batch, w, width, tm):
    n, d = x.shape
    tn = width
    nt = w.shape[1] // tn
    emit = w.dtype != BF16
    assert not emit or n == tm
    once = dict(pipeline_mode=pl.Buffered(1)) if emit else {}
    row = pl.BlockSpec((tm, tn), lambda i, j: (i, 0), **once)
    wspec = pl.BlockSpec((d, tn), lambda i, j: (0, j))
    mat = jax.ShapeDtypeStruct((n, width), F32)
    return pl.pallas_call(
        _in_proj_kernel,
        grid=(n // tm, nt),
        in_specs=[pl.BlockSpec((tm, d), lambda i, j: (i, 0), **once),
                  _mod_spec(mod3, tm, rows_per_batch, 0, 2),
                  _mod_spec(mod3, tm, rows_per_batch, 1, 2),
                  wspec],
        out_specs=[pl.BlockSpec((tn // LANES, tm, LANES), lambda i, j: (0, i, 0), **once), row, row, row,
                   pl.BlockSpec((tm, tn), lambda i, j: (i, jnp.maximum(j - 4, 0)))] + ([wspec] if emit else []),
        out_shape=[jax.ShapeDtypeStruct((width // LANES, n, LANES), F32), mat, mat, mat,
                   jax.ShapeDtypeStruct((n, (nt - 4) * tn), F32)]
        + ([jax.ShapeDtypeStruct(w.shape, BF16)] if emit else []),
        scratch_shapes=[pltpu.VMEM((tm, d), BF16)],
        compiler_params=_params("arbitrary", "arbitrary"),
        name="in_proj",
    )(x, mod3, mod3, w)


def _ssm_prep_kernel(are_ref, aim_ref, ldt_ref, bre_ref, bim_ref, cre_ref, cim_ref, d_ref,
                     pd_ref, qre_ref, qim_ref, kt_ref, lpre_ref, lpim_ref, *, npow, gt):
    c = SSM_GROUP
    p = SSM_STATE
    a_re = are_ref[...]
    a_im = aim_ref[...]
    dt = jnp.exp(ldt_ref[...])
    mag = jnp.exp(dt * a_re)
    ab_re = mag * jnp.cos(dt * a_im)
    ab_im = mag * jnp.sin(dt * a_im)
    den = a_re * a_re + a_im * a_im
    f_re = ((ab_re - 1.0) * a_re + ab_im * a_im) / den
    f_im = (ab_im * a_re - (ab_re - 1.0) * a_im) / den
    b_re = bre_ref[...]
    b_im = bim_ref[...]
    bb_re = f_re[:, None, :] * b_re - f_im[:, None, :] * b_im
    bb_im = f_re[:, None, :] * b_im + f_im[:, None, :] * b_re
    c_re = cre_ref[...]
    c_im = cim_ref[...]
    low = lax.broadcasted_iota(jnp.int32, (1, 1, 2 * p), 2) < p
    p_re = jnp.ones_like(a_re)
    p_im = jnp.zeros_like(a_re)
    g_re_all, g_im_all = [], []
    for n in range(npow):
        pr = p_re[:, None, :]
        pi = p_im[:, None, :]
        pw_re = pr * bb_re - pi * bb_im
        pw_im = pr * bb_im + pi * bb_re
        pd_ref[n] = jnp.concatenate([pw_re, pw_im], axis=-1).reshape(gt * c, 4 * p)
        g_re = c_re * pr - c_im * pi
        g_im = -(c_re * pi + c_im * pr)
        g_re_all.append(g_re)
        g_im_all.append(g_im)
        lpre_ref[n] = p_re
        lpim_ref[n] = p_im
        p_re, p_im = p_re * ab_re - p_im * ab_im, p_re * ab_im + p_im * ab_re
    g_re_all = jnp.concatenate(g_re_all, axis=1)
    g_im_all = jnp.concatenate(g_im_all, axis=1)
    for gg in range(0, gt, 2):
        sl = slice(gg // 2 * 2 * p, (gg // 2 + 1) * 2 * p)
        qre_ref[:, sl] = jnp.where(low[0], g_re_all[gg], g_re_all[gg + 1])
        qim_ref[:, sl] = jnp.where(low[0], g_im_all[gg], g_im_all[gg + 1])
    rows = kt_ref.shape[2]
    zpad = jnp.zeros((gt, rows - npow * c, 2 * p), F32)
    dn = (((2,), (2,)), ((0,), (0,)))
    kt = (lax.dot_general(jnp.where(low, bb_re, 0.0), jnp.concatenate([g_re_all, zpad], axis=1), dn,
                          precision=lax.Precision.HIGHEST, preferred_element_type=F32)
          + lax.dot_general(jnp.where(low, bb_im, 0.0), jnp.concatenate([g_im_all, zpad], axis=1), dn,
                            precision=lax.Precision.HIGHEST, preferred_element_type=F32))
    eye = (lax.broadcasted_iota(jnp.int32, (c, rows), 0) == lax.broadcasted_iota(jnp.int32, (c, rows), 1))
    kt_ref[...] = kt + jnp.where(eye[None], d_ref[...], 0.0)


def _ssm_prep(a_re, a_im, log_dt, b_re, b_im, c_re, c_im, d_skip, npow):
    g, p = a_re.shape
    c = SSM_GROUP
    gt = 8
    kt_w = -(-npow * c // LANES) * LANES
    dup = lambda a: jnp.concatenate([a, a], axis=-1)
    gp = pl.BlockSpec((gt, 2 * p), lambda i: (i, 0))
    gcp = pl.BlockSpec((gt, c, 2 * p), lambda i: (i, 0, 0))
    qspec = pl.BlockSpec((npow * c, gt * p), lambda i: (0, i))
    lspec = pl.BlockSpec((npow, gt, 2 * p), lambda i: (0, i, 0))
    return pl.pallas_call(
        functools.partial(_ssm_prep_kernel, npow=npow, gt=gt),
        grid=(g // gt,),
        in_specs=[gp, gp, pl.BlockSpec((gt, 1), lambda i: (i, 0)), gcp, gcp, gcp, gcp,
                  pl.BlockSpec((gt, c, 1), lambda i: (i, 0, 0))],
        out_specs=[pl.BlockSpec((npow, gt * c, 4 * p), lambda i: (0, i, 0)), qspec, qspec,
                   pl.BlockSpec((gt, c, kt_w), lambda i: (i, 0, 0)), lspec, lspec],
        out_shape=[jax.ShapeDtypeStruct((npow, g * c, 4 * p), F32),
                   jax.ShapeDtypeStruct((npow * c, g * p), F32),
                   jax.ShapeDtypeStruct((npow * c, g * p), F32),
                   jax.ShapeDtypeStruct((g, c, kt_w), F32),
                   jax.ShapeDtypeStruct((npow, g, 2 * p), F32),
                   jax.ShapeDtypeStruct((npow, g, 2 * p), F32)],
        compiler_params=_params("arbitrary"),
        name="ssm_prep",
    )(dup(a_re), dup(a_im), log_dt.reshape(g, 1), dup(jnp.swapaxes(b_re, 1, 2)), dup(jnp.swapaxes(b_im, 1, 2)),
      dup(c_re), dup(c_im), d_skip.reshape(g, c, 1))


def _toeplitz_kernel(kt_ref, w_ref, *, t_max):
    c = SSM_GROUP
    kw = kt_ref.shape[2]
    x = kt_ref[...].reshape(BLOCK_W, kw).astype(BF16)
    mask = _group_mask(BLOCK_W, BLOCK_W, c, c)
    sel_r = lax.broadcasted_iota(jnp.int32, (kw, BLOCK_W), 0)
    sel_c = lax.broadcasted_iota(jnp.int32, (kw, BLOCK_W), 1)
    for n in range(t_max):
        e = jnp.where((sel_r // c == n) & (sel_r % c == sel_c % c), 1.0, 0.0).astype(BF16)
        tile = jnp.where(mask, _dot(x, e), 0.0)
        w_ref[pl.ds((t_max - 1 - n) * BLOCK_W, BLOCK_W), :] = tile.astype(w_ref.dtype)


def _toeplitz_tiles(kt, t_max):
    g, c, kw = kt.shape
    gb = g // GROUPS_PER_BLOCK
    rows = t_max * BLOCK_W
    return pl.pallas_call(
        functools.partial(_toeplitz_kernel, t_max=t_max),
        grid=(gb,),
        in_specs=[pl.BlockSpec((GROUPS_PER_BLOCK, c, kw), lambda b: (b, 0, 0))],
        out_specs=pl.BlockSpec((None, rows, BLOCK_W), lambda b: (b, 0, 0)),
        out_shape=jax.ShapeDtypeStruct((gb, rows, BLOCK_W), BF16),
        compiler_params=_params("arbitrary"),
        name="ssm_toeplitz",
    )(kt)


def _group_mask(rows, cols, row_div, col_div):
    return (lax.broadcasted_iota(jnp.int32, (rows, cols), 0) // row_div
            == lax.broadcasted_iota(jnp.int32, (rows, cols), 1) // col_div)


def _ssm_kernel(u_ref, w_ref, pd_ref, qre_ref, qim_ref, s0re_ref, s0im_ref, lre_ref, lim_ref,
                y_ref, fre_ref, fim_ref, ucat_ref, v_ref, s_ref, *, t_chunk, nseq, nc):
    c = SSM_GROUP
    rb = nseq * nc
    t_max = w_ref.shape[0] // BLOCK_W
    for t in range(t_chunk):
        for s in range(BLOCK_W // LANES):
            lo = t * BLOCK_W + s * LANES
            ucat_ref[:, lo:lo + LANES] = u_ref[s, pl.ds(t, rb, stride=t_chunk), :].astype(BF16)
    mask = _group_mask(BLOCK_W, STATE_W, c, SSM_STATE)
    reps = STATE_W // LANES
    acc = jnp.zeros((rb, 2 * STATE_W), F32)
    for t in range(t_chunk):
        blk = pd_ref[t_chunk - 1 - t]
        w_re = jnp.where(mask, jnp.concatenate([blk[:, :LANES]] * reps, axis=1), 0.0)
        w_im = jnp.where(mask, jnp.concatenate([blk[:, LANES:]] * reps, axis=1), 0.0)
        w = jnp.concatenate([w_re, w_im], axis=1).astype(BF16)
        acc = acc + _dot(ucat_ref[:, t * BLOCK_W:(t + 1) * BLOCK_W], w)
    v_ref[0] = acc[:, :STATE_W]
    v_ref[1] = acc[:, STATE_W:]
    ar = lre_ref[...]
    ai = lim_ref[...]
    if nc == 1:
        sr = s0re_ref[...]
        si = s0im_ref[...]
        s_ref[0] = sr
        s_ref[1] = si
        fre_ref[...] = ar * sr - ai * si + v_ref[0]
        fim_ref[...] = ar * si + ai * sr + v_ref[1]
    else:
        def body(k, carry):
            new = []
            for b in range(nseq):
                sr, si = carry[2 * b], carry[2 * b + 1]
                row = pl.ds(b * nc + k, 1)
                s_ref[0, row, :] = sr
                s_ref[1, row, :] = si
                new.append(ar * sr - ai * si + v_ref[0, row, :])
                new.append(ar * si + ai * sr + v_ref[1, row, :])
            return tuple(new)

        init = []
        for b in range(nseq):
            init += [s0re_ref[b:b + 1, :], s0im_ref[b:b + 1, :]]
        fin = lax.fori_loop(0, nc, body, tuple(init))
        for b in range(nseq):
            fre_ref[b:b + 1, :] = fin[2 * b]
            fim_ref[b:b + 1, :] = fin[2 * b + 1]
    s_re = s_ref[0].astype(BF16)
    s_im = s_ref[1].astype(BF16)
    for t in range(t_chunk):
        rows = slice((t + 1) * c, (t + 2) * c)
        q_re = jnp.where(mask, jnp.concatenate([qre_ref[rows, :]] * GROUPS_PER_BLOCK, axis=0), 0.0).astype(BF16)
        q_im = jnp.where(mask, jnp.concatenate([qim_ref[rows, :]] * GROUPS_PER_BLOCK, axis=0), 0.0).astype(BF16)
        y = (_dot(ucat_ref[:, :(t + 1) * BLOCK_W], w_ref[(t_max - 1 - t) * BLOCK_W:, :])
             + _dot_nt(s_re, q_re) + _dot_nt(s_im, q_im))
        for s in range(BLOCK_W // LANES):
            y_ref[s, pl.ds(t, rb, stride=t_chunk), :] = y[:, s * LANES:(s + 1) * LANES]


def _ssm(u_slabs, w_t, pd, q_re, q_im, s0_re, s0_im, lam_re, lam_im, t_chunk, nb, nc):
    gb = w_t.shape[0]
    c = SSM_GROUP
    nseq = max(1, min(nb, 256 // nc))
    nblk = nb // nseq
    rb = nseq * nc
    per = BLOCK_W // LANES
    gp = gb * STATE_W
    uspec = pl.BlockSpec((per, rb * t_chunk, LANES), lambda g, i: (g, i, 0))
    qspec = pl.BlockSpec(((t_chunk + 1) * c, STATE_W), lambda g, i: (0, g))
    sspec = pl.BlockSpec((None, nseq, STATE_W), lambda g, i: (i, 0, g))
    lspec = pl.BlockSpec((1, STATE_W), lambda g, i: (0, g))
    fshape = jax.ShapeDtypeStruct((nblk, nseq, gp), F32)
    y, f_re, f_im = pl.pallas_call(
        functools.partial(_ssm_kernel, t_chunk=t_chunk, nseq=nseq, nc=nc),
        grid=(gb, nblk),
        in_specs=[uspec,
                  pl.BlockSpec((None,) + w_t.shape[1:], lambda g, i: (g, 0, 0)),
                  pl.BlockSpec((t_chunk, BLOCK_W, 2 * LANES), lambda g, i: (0, g, 0)),
                  qspec, qspec, sspec, sspec, lspec, lspec],
        out_specs=[uspec, sspec, sspec],
        out_shape=[jax.ShapeDtypeStruct(u_slabs.shape, F32), fshape, fshape],
        scratch_shapes=[pltpu.VMEM((rb, t_chunk * BLOCK_W), BF16), pltpu.VMEM((2, rb, STATE_W), F32),
                        pltpu.VMEM((2, rb, STATE_W), F32)],
        compiler_params=_params("arbitrary", "arbitrary"),
        name="ssm_chunked",
    )(u_slabs, w_t, pd, q_re, q_im, s0_re.reshape(nblk, nseq, gp), s0_im.reshape(nblk, nseq, gp), lam_re, lam_im)
    return y, f_re.reshape(nb, gp), f_im.reshape(nb, gp)


def _lambda(lq1_ref, lk1_ref, lq2_ref, lk2_ref, lambda_init):
    return (jnp.exp(jnp.sum(lq1_ref[...] * lk1_ref[...], axis=-1, keepdims=True))
            - jnp.exp(jnp.sum(lq2_ref[...] * lk2_ref[...], axis=-1, keepdims=True)) + lambda_init)


def _bias_columns(n, off, pieces, key_side):
    hw = 2 * HEAD_DIM
    lane = lax.broadcasted_iota(jnp.int32, (1, hw), 1) - off
    valid = (lane >= 0) & (lane < 4 * len(pieces))
    a = lane // 4
    kap = lane % 4
    s_lane = pieces[-1]
    for i in range(len(pieces) - 2, -1, -1):
        s_lane = jnp.where(a == i, pieces[i], s_lane)
    pos = lax.broadcasted_iota(jnp.int32, (n, 1), 0)
    hi = (pos // 64 * 64).astype(F32)
    lo = (pos % 64).astype(F32)
    if key_side:
        val = jnp.where(kap < 2, s_lane, jnp.where(kap == 2, hi, lo))
    else:
        val = jnp.where(kap == 0, -hi, jnp.where(kap == 1, -lo, s_lane))
    return jnp.where(valid, val, 0.0)


def _attn_kernel(slope_ref, q_ref, k_ref, v_ref, lq1_ref, lk1_ref, lq2_ref, lk2_ref, g_ref, o_ref,
                 rel_ref, ka_ref, qb_ref, m_ref, l_ref, acc_ref, *, tq, tk, nq, lambda_init):
    h = pl.program_id(1)
    qi = pl.program_id(2)
    hd = HEAD_DIM
    hw = 2 * hd
    slope2 = slope_ref[h] * LOG2E
    first = lax.broadcasted_iota(jnp.int32, (1, hw), 1) < hd

    @pl.when(qi == 0)
    def _():
        sv = jnp.full((1, hw), slope2, F32)
        s1 = sv.astype(BF16).astype(F32)
        s2 = (sv - s1).astype(BF16).astype(F32)
        s3 = (sv - s1 - s2).astype(BF16).astype(F32)
        pieces = (s1, s2, s3)
        qb_ref[0] = _bias_columns(tq, hd, pieces, False)
        qb_ref[1] = _bias_columns(tq, 0, pieces, False)
        ka_ref[0] = _bias_columns(tk, hd, pieces, True)
        ka_ref[1] = _bias_columns(tk, 0, pieces, True)
        rel_ref[...] = (lax.broadcasted_iota(jnp.int32, (tk, tq), 1)
                        - lax.broadcasted_iota(jnp.int32, (tk, tq), 0)).astype(F32)

    q = q_ref[...] * (hd ** -0.5 * LOG2E)
    qm = (jnp.where(first, q, qb_ref[0]).astype(BF16), jnp.where(first, qb_ref[1], q).astype(BF16))
    ones = jnp.ones((16, tk), BF16)
    m_ref[...] = jnp.full_like(m_ref, NEG_BIG)
    l_ref[...] = jnp.zeros_like(l_ref)
    acc_ref[...] = jnp.zeros_like(acc_ref)

    def scores(kj, q_blk, masked):
        k = k_ref[pl.ds(kj * tk, tk), :]
        km = (jnp.where(first, k, ka_ref[0]).astype(BF16), jnp.where(first, ka_ref[1], k).astype(BF16))
        ts = [_dot_nt(km[mp], qm[mp]) for mp in range(2)]
        if masked:
            keep = rel_ref[...] >= 0.0
            ts = [jnp.where(keep, t, -jnp.inf) for t in ts]
        return ts, -slope2 * float(q_blk * tq - kj * tk)

    def update(kj, ts, c):
        v = v_ref[pl.ds(kj * tk, tk), :].astype(BF16)
        m_old = [m_ref[mp] for mp in range(2)]
        m_new = [jnp.maximum(m_old[mp], jnp.max(ts[mp], axis=0, keepdims=True) + c) for mp in range(2)]
        ps = [jnp.exp2(ts[mp] - (m_new[mp] - c)).astype(BF16) for mp in range(2)]
        for mp in range(2):
            alpha = jnp.exp2(m_old[mp] - m_new[mp])
            l_ref[mp] = alpha * l_ref[mp] + _dot(ones, ps[mp])[0:1]
            acc_ref[mp] = alpha * acc_ref[mp] + _dot_tn(v, ps[mp])
            m_ref[mp] = m_new[mp]

    for q_blk in range(nq):
        @pl.when(qi == q_blk)
        def _(q_blk=q_blk):
            nxt = scores(0, q_blk, q_blk == 0)
            for kj in range(q_blk + 1):
                cur = nxt
                if kj < q_blk:
                    nxt = scores(kj + 1, q_blk, kj + 1 == q_blk)
                update(kj, *cur)

    lam = _lambda(lq1_ref, lk1_ref, lq2_ref, lk2_ref, lambda_init)
    o = acc_ref[0] / l_ref[0] - lam * (acc_ref[1] / l_ref[1])
    o = o * lax.rsqrt(jnp.mean(o * o, axis=0, keepdims=True) + LN_EPS) * g_ref[...] * (1.0 - lambda_init)
    o_ref[...] = o.T.astype(o_ref.dtype)


def _attention(q, k, v, slopes, lam_vecs, subln_g, nbatch, seq, lambda_init, tq, tk):
    n, da = q.shape
    nh = da // (2 * HEAD_DIM)
    nq = seq // tq
    hw = 2 * HEAD_DIM
    vec = pl.BlockSpec((1, HEAD_DIM), lambda b, h, i: (0, 0))
    return pl.pallas_call(
        functools.partial(_attn_kernel, tq=tq, tk=tk, nq=nq, lambda_init=lambda_init),
        grid=(nbatch, nh, nq),
        in_specs=[pl.BlockSpec(memory_space=pltpu.SMEM),
                  pl.BlockSpec((tq, hw), lambda b, h, i: (b * nq + i, h)),
                  pl.BlockSpec((seq, hw), lambda b, h, i: (b, h)),
                  pl.BlockSpec((seq, hw), lambda b, h, i: (b, h)),
                  vec, vec, vec, vec,
                  pl.BlockSpec((hw, 1), lambda b, h, i: (0, 0))],
        out_specs=pl.BlockSpec((tq, hw), lambda b, h, i: (b * nq + i, h)),
        out_shape=jax.ShapeDtypeStruct((n, da), BF16),
        scratch_shapes=[pltpu.VMEM((tk, tq), F32), pltpu.VMEM((2, tk, hw), F32), pltpu.VMEM((2, tq, hw), F32),
                        pltpu.VMEM((2, 1, tq), F32),
                        pltpu.VMEM((2, 1, tq), F32), pltpu.VMEM((2, hw, tq), F32)],
        compiler_params=_params("arbitrary", "arbitrary", "arbitrary"),
        name="prompt_attention",
    )(slopes, q, k, v, *lam_vecs, subln_g.reshape(hw, 1))


def _dec_attn_kernel(pt_ref, q_ref, kn_ref, vn_ref, *rest, n_pages, page, tdec, nh, lambda_init):
    del pt_ref
    k_refs = rest[:n_pages] + (kn_ref,)
    v_refs = rest[n_pages:2 * n_pages] + (vn_ref,)
    lq1_ref, lk1_ref, lq2_ref, lk2_ref, g_ref, o_ref = rest[2 * n_pages:]
    hd = HEAD_DIM
    hw = 2 * hd
    ncol = 2 * nh * tdec
    past = n_pages * page
    q = q_ref[...] * (hd ** -0.5 * LOG2E)
    rid = lax.broadcasted_iota(jnp.int32, (ncol, hw), 0)
    cid = lax.broadcasted_iota(jnp.int32, (ncol, hw), 1)
    qt = jnp.zeros((ncol, hw), F32)
    for h in range(nh):
        for i in range(tdec):
            sel = (rid // (2 * tdec) == h) & (rid % tdec == i)
            qt = jnp.where(sel, jnp.broadcast_to(q[i:i + 1, h * hw:(h + 1) * hw], (ncol, hw)), qt)
    qt = jnp.where((rid // tdec) % 2 == cid // hd, qt, 0.0)
    zq = jnp.zeros_like(qt)
    qt2 = jnp.concatenate([jnp.concatenate([qt, zq], axis=1), jnp.concatenate([zq, qt], axis=1)],
                          axis=0).astype(BF16)
    col2 = lax.broadcasted_iota(jnp.int32, (1, 2 * ncol), 1)
    second = col2 // ncol
    col = col2 % ncol
    qidx = col % tdec
    hcol = col // (2 * tdec)
    slope = jnp.exp2((hcol + 1).astype(F32) * (-8.0 / nh)) * LOG2E

    def rows(n):
        r = lax.broadcasted_iota(jnp.int32, (n, 1), 0)
        return r // nh, r % nh

    tok, hrow = rows(page * nh)
    base = jnp.where(hrow == hcol, -slope * (past + qidx - tok - second * page).astype(F32), -jnp.inf)
    tokn, hrown = rows(tdec * nh)
    basen = jnp.where((hrown == hcol) & (qidx >= tokn) & (second == 0),
                      -slope * (qidx - tokn).astype(F32), -jnp.inf)
    blocks = [(j, j + 1, base, slope * float(j * page)) for j in range(0, n_pages - 1, 2)]
    if n_pages % 2:
        blocks.append((n_pages - 1, None, jnp.where(second == 0, base, -jnp.inf), slope * float((n_pages - 1) * page)))
    blocks.append((n_pages, None, basen, jnp.zeros_like(slope)))

    def pair(refs, a, b):
        xa = refs[a][...]
        xb = refs[b][...] if b is not None else jnp.zeros_like(xa)
        return jnp.concatenate([xa, xb], axis=1).astype(BF16)

    t_list = []
    m2 = jnp.full((1, 2 * ncol), NEG_BIG, F32)
    for a, b, bias, c in blocks:
        t = _dot_nt(pair(k_refs, a, b), qt2) + bias
        m2 = jnp.maximum(m2, jnp.max(t, axis=0, keepdims=True) + c)
        t_list.append(t)
    m = jnp.maximum(m2[:, :ncol], m2[:, ncol:])
    m2 = jnp.concatenate([m, m], axis=1)
    l2 = jnp.zeros((1, 2 * ncol), F32)
    acc2 = jnp.zeros((2 * ncol, 2 * hw), F32)
    for (a, b, bias, c), t in zip(blocks, t_list):
        p = jnp.exp2(t - (m2 - c))
        l2 = l2 + jnp.sum(p, axis=0, keepdims=True)
        p = p.astype(BF16)
        v = pair(v_refs, a, b)
        if p.shape[0] < LANES:
            pad = LANES - p.shape[0]
            p = jnp.concatenate([p, jnp.zeros((pad, 2 * ncol), BF16)], axis=0)
            v = jnp.concatenate([v, jnp.zeros((pad, 2 * hw), BF16)], axis=0)
        acc2 = acc2 + _dot_tn(p, v)
    l = l2[:, :ncol] + l2[:, ncol:]
    acc = acc2[:ncol, :hw] + acc2[ncol:, hw:]
    eye = lax.broadcasted_iota(jnp.int32, (ncol, ncol), 0) == lax.broadcasted_iota(jnp.int32, (ncol, ncol), 1)
    lcol = jnp.sum(jnp.where(eye, jnp.broadcast_to(l, (ncol, ncol)), 0.0), axis=1, keepdims=True)
    acc = acc / lcol
    lam = _lambda(lq1_ref, lk1_ref, lq2_ref, lk2_ref, lambda_init)
    g = g_ref[...]
    for h in range(nh):
        blk = acc[h * 2 * tdec:(h + 1) * 2 * tdec]
        o = blk[:tdec] - lam * blk[tdec:]
        o = o * lax.rsqrt(jnp.mean(o * o, axis=-1, keepdims=True) + LN_EPS) * g * (1.0 - lambda_init)
        o_ref[:, h * hw:(h + 1) * hw] = o.astype(o_ref.dtype)


def _decode_attention(q3, k3, v3, cache_k, cache_v, page_table, lam_vecs, subln_g, lambda_init):
    nb, tdec, da = q3.shape
    n_pages = page_table.shape[1]
    rows, hw = cache_k.shape[1:]
    nh = da // hw
    page = rows // nh
    new = pl.BlockSpec((None, tdec, da), lambda b, pt: (b, 0, 0))
    newp = pl.BlockSpec((None, tdec * nh, hw), lambda b, pt: (b, 0, 0))
    pages = [pl.BlockSpec((None, rows, hw), lambda b, pt, j=j: (pt[b, j], 0, 0)) for j in range(n_pages)]
    vec = pl.BlockSpec((1, HEAD_DIM), lambda b, pt: (0, 0))
    return pl.pallas_call(
        functools.partial(_dec_attn_kernel, n_pages=n_pages, page=page, tdec=tdec, nh=nh, lambda_init=lambda_init),
        grid_spec=pltpu.PrefetchScalarGridSpec(
            num_scalar_prefetch=1,
            grid=(nb,),
            in_specs=[new, newp, newp] + pages + pages + [vec, vec, vec, vec,
                                                          pl.BlockSpec((1, hw), lambda b, pt: (0, 0))],
            out_specs=new),
        out_shape=jax.ShapeDtypeStruct((nb, tdec, da), BF16),
        compiler_params=_params("arbitrary"),
        name="decode_attention",
    )(page_table, q3, k3.reshape(nb, tdec * nh, hw), v3.reshape(nb, tdec * nh, hw),
      *([cache_k] * n_pages), *([cache_v] * n_pages), *lam_vecs, subln_g)


def _mix_in_kernel(y_ref, o_ref, gs_ref, ga_ref, wglu_ref, wus_ref, wua_ref, out_ref):
    z = jax.nn.gelu(jnp.concatenate([y_ref[i] for i in range(y_ref.shape[0])], axis=1))
    zz = z * jax.nn.sigmoid(_dot(z.astype(BF16), wglu_ref[...]))
    y_ssm = _dot(zz.astype(BF16), wus_ref[...])
    y_att = _dot(o_ref[...], wua_ref[...])
    out_ref[...] = (jax.nn.sigmoid(gs_ref[...]) * y_ssm + jax.nn.sigmoid(ga_ref[...]) * y_att).astype(out_ref.dtype)


def _mix_in(y, o, gates, w_glu, w_up_ssm, w_up_att, tm):
    nslab, n, _ = y.shape
    da = o.shape[1]
    d = w_up_ssm.shape[1]
    full = lambda a: pl.BlockSpec(a.shape, lambda i: (0, 0))
    return pl.pallas_call(
        _mix_in_kernel,
        grid=(n // tm,),
        in_specs=[pl.BlockSpec((nslab, tm, LANES), lambda i: (0, i, 0)),
                  pl.BlockSpec((tm, da), lambda i: (i, 0)),
                  pl.BlockSpec((tm, d), lambda i: (i, 0)),
                  pl.BlockSpec((tm, d), lambda i: (i, 1)),
                  full(w_glu), full(w_up_ssm), full(w_up_att)],
        out_specs=pl.BlockSpec((tm, d), lambda i: (i, 0)),
        out_shape=jax.ShapeDtypeStruct((n, d), BF16),
        compiler_params=_params("arbitrary"),
        name="branch_mix",
    )(y, o, gates, gates, w_glu, w_up_ssm, w_up_att)


def _out_proj_kernel(mix_ref, x_ref, g1_ref, wo_ref, lg_ref, lb_ref, out_ref, *, alpha):
    r = alpha * x_ref[...] + _mod_val(g1_ref) * _dot(mix_ref[...], wo_ref[...])
    out_ref[...] = _ln(r) * lg_ref[...] + lb_ref[...]


def _out_proj(mix, x, mod3, rows_per_batch, w_o, ln_g, ln_b, alpha, tm):
    n, d = x.shape
    row = pl.BlockSpec((tm, d), lambda i: (i, 0))
    vec = pl.BlockSpec((1, d), lambda i: (0, 0))
    return pl.pallas_call(
        functools.partial(_out_proj_kernel, alpha=alpha),
        grid=(n // tm,),
        in_specs=[row, row, _mod_spec(mod3, tm, rows_per_batch, 2, 1),
                  pl.BlockSpec(w_o.shape, lambda i: (0, 0)), vec, vec],
        out_specs=row,
        out_shape=jax.ShapeDtypeStruct((n, d), F32),
        compiler_params=_params("arbitrary"),
        name="out_proj_ln1",
    )(mix, x, mod3, w_o, ln_g, ln_b)


def _ffn_kernel(x_ref, sh_ref, sc_ref, g2_ref, w1_ref, w2_ref, lg_ref, lb_ref, out_ref, *rest, alpha, nf):
    f = pl.program_id(1)
    h_ref, acc_ref = rest[-2:]
    if len(rest) == 4:
        w1b_ref, w2b_ref = rest[:2]
        w1b_ref[...] = w1_ref[...].astype(w1b_ref.dtype)
        w2b_ref[...] = w2_ref[...].astype(w2b_ref.dtype)
    else:
        w1b_ref, w2b_ref = w1_ref, w2_ref

    @pl.when(f == 0)
    def _():
        h_ref[...] = (_ln(x_ref[...]) * (1.0 + _mod_val(sc_ref)) + _mod_val(sh_ref)).astype(h_ref.dtype)
        acc_ref[...] = jnp.zeros_like(acc_ref)

    a = jnp.maximum(_dot(h_ref[...], w1b_ref[...]), 0.0)
    acc_ref[...] += _dot((a * a).astype(BF16), w2b_ref[...])

    @pl.when(f == nf - 1)
    def _():
        r = alpha * x_ref[...] + _mod_val(g2_ref) * acc_ref[...]
        out_ref[...] = _ln(r) * lg_ref[...] + lb_ref[...]


def _ffn(x, mod3, rows_per_batch, w1, w2, ln_g, ln_b, alpha, tm, tf):
    n, d = x.shape
    dff = w1.shape[1]
    nf = dff // tf
    emit = w1.dtype != BF16
    assert not emit or n == tm
    once = dict(pipeline_mode=pl.Buffered(1)) if emit else {}
    row = pl.BlockSpec((tm, d), lambda i, f: (i, 0), **once)
    vec = pl.BlockSpec((1, d), lambda i, f: (0, 0))
    w1spec = pl.BlockSpec((d, tf), lambda i, f: (0, f))
    w2spec = pl.BlockSpec((tf, d), lambda i, f: (f, 0))
    res = pl.pallas_call(
        functools.partial(_ffn_kernel, alpha=alpha, nf=nf),
        grid=(n // tm, nf),
        in_specs=[row,
                  _mod_spec(mod3, tm, rows_per_batch, 3, 2),
                  _mod_spec(mod3, tm, rows_per_batch, 4, 2),
                  _mod_spec(mod3, tm, rows_per_batch, 5, 2),
                  w1spec, w2spec, vec, vec],
        out_specs=[row] + ([w1spec, w2spec] if emit else []),
        out_shape=[jax.ShapeDtypeStruct((n, d), F32)]
        + ([jax.ShapeDtypeStruct(w1.shape, BF16), jax.ShapeDtypeStruct(w2.shape, BF16)] if emit else []),
        scratch_shapes=[pltpu.VMEM((tm, d), BF16), pltpu.VMEM((tm, d), F32)],
        compiler_params=_params("arbitrary", "arbitrary"),
        name="ffn_ln2",
    )(x, mod3, mod3, mod3, w1, w2, ln_g, ln_b)
    return res


def _trunk_layer(x3, mod3, s0, cache, page_table, wts, tabs, lambda_init, alpha):
    nbatch, seq, d = x3.shape
    n = nbatch * seq
    x = x3.reshape(n, d)
    ds = wts["w_glu"].shape[0]
    da = wts["w_up_att"].shape[0]
    nh = da // (2 * HEAD_DIM)
    tm = min(512, seq) if mod3.shape[1] == 1 else min(512, n)
    assert ds == da
    u, q, k, v, gates, *w_in_b = _in_proj(x, mod3, seq, wts["w_in"], ds, tm)

    pd, q_re, q_im, w_t, lp_re, lp_im = tabs
    t_chunk = PROMPT_CHUNK if seq % PROMPT_CHUNK == 0 else seq
    gp = lp_re.shape[1] * SSM_STATE
    lam_re = lp_re[t_chunk, :, :SSM_STATE].reshape(1, gp)
    lam_im = lp_im[t_chunk, :, :SSM_STATE].reshape(1, gp)
    y, f_re, f_im = _ssm(u, w_t, pd, q_re, q_im, s0[0], s0[1], lam_re, lam_im, t_chunk, nbatch, seq // t_chunk)

    lam_vecs = wts["lam_vecs"]
    if cache is None:
        slopes = jnp.asarray([2.0 ** (-8.0 * (i + 1) / nh) for i in range(nh)], F32)
        tq = min(512, seq)
        o = _attention(q, k, v, slopes, lam_vecs, wts["subln_g"], nbatch, seq, lambda_init, tq, tq)
    else:
        o = _decode_attention(q.reshape(nbatch, seq, da), k.reshape(nbatch, seq, da), v.reshape(nbatch, seq, da),
                              cache[0], cache[1], page_table, lam_vecs, wts["subln_g"], lambda_init)
        o = o.reshape(n, da)

    mix = _mix_in(y, o, gates, wts["w_glu"], wts["w_up_ssm"], wts["w_up_att"], min(256, tm))
    x1 = _out_proj(mix, x, mod3, seq, wts["w_o"], wts["ln1_g"], wts["ln1_b"], alpha, tm)
    tf = 1024 if wts["w_ff1"].dtype == BF16 else 256
    x2, *w_ff_b = _ffn(x1, mod3, seq, wts["w_ff1"], wts["w_ff2"], wts["ln2_g"], wts["ln2_b"], alpha, tm,
                       min(tf, wts["w_ff1"].shape[1]))
    return (x2.reshape(nbatch, seq, d), k.reshape(nbatch, seq, nh, 2 * HEAD_DIM),
            v.reshape(nbatch, seq, nh, 2 * HEAD_DIM), f_re, f_im, tuple(w_in_b) + tuple(w_ff_b))


def kernel(x_prompt, x_sample, c_prompt, c_sample, cache_k, cache_v, state_ssm_re, state_ssm_im, page_table,
           w_ada, b_ada, w_in, ssm_a_re, ssm_a_im, ssm_log_dt, ssm_b_re, ssm_b_im, ssm_c_re, ssm_c_im, ssm_d,
           w_glu, w_up_ssm, lam_q1, lam_k1, lam_q2, lam_k2, subln_g, w_up_att, w_o, ln1_g, ln1_b, w_ff1, w_ff2,
           ln2_g, ln2_b):
    depth = w_in.shape[0]
    bp, seq_p, d = x_prompt.shape
    bs, seq_s, _ = x_sample.shape
    g, p = ssm_a_re.shape[1:]
    alpha = (2 * depth) ** 0.25
    npow = PROMPT_CHUNK + 1
    xp, xs = x_prompt, x_sample
    outs_p, outs_s = [], []
    c_rows = jnp.concatenate([c_sample, c_prompt], axis=0)
    c_rows = jnp.pad(c_rows, ((0, (-(bp + bs)) % 8), (0, 0)))
    n_pool, page, nh, hw = cache_k.shape[1:]
    for l in range(depth):
        lambda_init = 0.8 - 0.6 * math.exp(-0.3 * l)
        mod, mod_s = _ada(c_rows, w_ada[l], b_ada[l].reshape(1, 6 * d), min(1024, d), bs, seq_s)
        mod_p = mod[bs:bs + bp].reshape(bp, 1, 6 * d)
        wts = {
            "w_in": w_in[l], "w_ff1": w_ff1[l], "w_ff2": w_ff2[l],
            "w_glu": w_glu[l].astype(BF16), "w_up_ssm": w_up_ssm[l].astype(BF16),
            "w_up_att": w_up_att[l].astype(BF16), "w_o": w_o[l].astype(BF16),
            "ln1_g": ln1_g[l].reshape(1, d), "ln1_b": ln1_b[l].reshape(1, d),
            "ln2_g": ln2_g[l].reshape(1, d), "ln2_b": ln2_b[l].reshape(1, d),
            "subln_g": subln_g[l].reshape(1, 2 * HEAD_DIM),
            "lam_vecs": tuple(a[l].reshape(1, HEAD_DIM) for a in (lam_q1, lam_k1, lam_q2, lam_k2)),
        }
        tabs = _ssm_prep(ssm_a_re[l], ssm_a_im[l], ssm_log_dt[l], ssm_b_re[l], ssm_b_im[l], ssm_c_re[l],
                         ssm_c_im[l], ssm_d[l], npow)
        tabs = list(tabs)
        tabs[3] = _toeplitz_tiles(tabs[3], PROMPT_CHUNK)
        cache = (cache_k[l].reshape(n_pool, page * nh, hw), cache_v[l].reshape(n_pool, page * nh, hw))
        s0 = (state_ssm_re[l].reshape(bs, g * p), state_ssm_im[l].reshape(bs, g * p))
        res_s = _trunk_layer(xs, mod_s, s0, cache, page_table, wts, tabs, lambda_init, alpha)
        wts["w_in"], wts["w_ff1"], wts["w_ff2"] = res_s[5]
        zeros = jnp.zeros((bp, g * p), F32)
        res_p = _trunk_layer(xp, mod_p, (zeros, zeros), None, None, wts, tabs, lambda_init, alpha)
        xp, xs = res_p[0], res_s[0]
        outs_p.append(res_p[1:5])
        outs_s.append(res_s[1:5])

    def stack(outs, i, shape=None):
        a = jnp.stack([o[i] for o in outs])
        return a if shape is None else a.reshape((depth,) + shape)

    return (xp, xs,
            stack(outs_p, 0), stack(outs_p, 1), stack(outs_p, 2, (bp, g, p)), stack(outs_p, 3, (bp, g, p)),
            stack(outs_s, 0), stack(outs_s, 1), stack(outs_s, 2, (bs, g, p)), stack(outs_s, 3, (bs, g, p)))
```

```python
import functools
import math

import jax
import jax.numpy as jnp
from jax import lax
from jax.experimental import pallas as pl
from jax.experimental.pallas import tpu as pltpu

F32 = jnp.float32
BF16 = jnp.bfloat16

SSM_GROUP = 16
SSM_STATE = 64
HEAD_DIM = 64
LANES = 128
BLOCK_W = 256
GROUPS_PER_BLOCK = BLOCK_W // SSM_GROUP
STATE_W = GROUPS_PER_BLOCK * SSM_STATE
PROMPT_CHUNK = 16
LN_EPS = 1e-5
NEG_BIG = -1e30
LOG2E = 1.4426950408889634
VMEM_LIMIT = 56 * 1024 * 1024


def _params(*sem):
    return pltpu.CompilerParams(dimension_semantics=sem, vmem_limit_bytes=VMEM_LIMIT)


def _ln(x):
    mu = jnp.mean(x, axis=-1, keepdims=True)
    xc = x - mu
    var = jnp.mean(xc * xc, axis=-1, keepdims=True)
    return xc * lax.rsqrt(var + LN_EPS)


def _dot(a, b):
    return jnp.dot(a, b, preferred_element_type=F32)


def _dot_nt(a, b):
    return lax.dot_general(a, b, (((1,), (1,)), ((), ())), preferred_element_type=F32)


def _dot_tn(a, b):
    return lax.dot_general(a, b, (((0,), (0,)), ((), ())), preferred_element_type=F32)


def _mod_spec(mod3, tm, rows_per_batch, col, ngrid):
    if mod3.shape[1] == 1:
        d = mod3.shape[2] // 6
        per = rows_per_batch // tm
        if ngrid == 1:
            return pl.BlockSpec((None, 1, d), lambda i: (i // per, 0, col))
        return pl.BlockSpec((None, 1, d), lambda i, j: (i // per, 0, col))
    nslab = mod3.shape[0] // 6
    once = pl.Buffered(1)
    if ngrid == 1:
        return pl.BlockSpec((nslab, tm, LANES), lambda i: (col, i, 0), pipeline_mode=once)
    return pl.BlockSpec((nslab, tm, LANES), lambda i, j: (col, i, 0), pipeline_mode=once)


def _mod_val(ref):
    if len(ref.shape) == 2:
        return ref[...]
    return jnp.concatenate([ref[s] for s in range(ref.shape[0])], axis=1)


def _ada_kernel(c_ref, w_ref, b_ref, o_ref, os_ref, *, nrep, seq):
    c = c_ref[...]
    a = (c * jax.nn.sigmoid(c)).astype(BF16)
    res = _dot(a, w_ref[...].astype(BF16)) + b_ref[...]
    o_ref[...] = res
    for s in range(os_ref.shape[0]):
        for t in range(seq):
            os_ref[s, pl.ds(t, nrep, stride=seq), :] = res[:nrep, s * LANES:(s + 1) * LANES]


def _ada(c, w, b, tn, nrep, seq):
    m, d = c.shape
    n = w.shape[1]
    return pl.pallas_call(
        functools.partial(_ada_kernel, nrep=nrep, seq=seq),
        grid=(n // tn,),
        in_specs=[pl.BlockSpec((m, d), lambda j: (0, 0)),
                  pl.BlockSpec((d, tn), lambda j: (0, j)),
                  pl.BlockSpec((1, tn), lambda j: (0, j))],
        out_specs=[pl.BlockSpec((m, tn), lambda j: (0, j)),
                   pl.BlockSpec((tn // LANES, nrep * seq, LANES), lambda j: (j, 0, 0))],
        out_shape=[jax.ShapeDtypeStruct((m, n), F32),
                   jax.ShapeDtypeStruct((n // LANES, nrep * seq, LANES), F32)],
        compiler_params=_params("arbitrary"),
        name="ada_mod",
    )(c, w, b)


def _in_proj_kernel(x_ref, sh_ref, sc_ref, w_ref, u_ref, q_ref, k_ref, v_ref, g_ref, *rest):
    j = pl.program_id(1)
    h_ref = rest[-1]
    if len(rest) == 2:
        wb_ref = rest[0]
        wb_ref[...] = w_ref[...].astype(wb_ref.dtype)
    else:
        wb_ref = w_ref

    @pl.when(j == 0)
    def _():
        h_ref[...] = (_ln(x_ref[...]) * (1.0 + _mod_val(sc_ref)) + _mod_val(sh_ref)).astype(h_ref.dtype)

    @pl.when(j == 0)
    def _():
        acc = _dot(h_ref[...], wb_ref[...])
        for s in range(u_ref.shape[0]):
            u_ref[s] = acc[:, s * LANES:(s + 1) * LANES]

    for idx, ref in ((1, q_ref), (2, k_ref), (3, v_ref)):
        @pl.when(j == idx)
        def _(ref=ref):
            ref[...] = _dot(h_ref[...], wb_ref[...])

    @pl.when(j >= 4)
    def _():
        g_ref[...] = _dot(h_ref[...], wb_ref[...])


def _in_proj(x, mod3, rows_per_batch, w, width, tm):
    n, d = x.shape
    tn = width
    nt = w.shape[1] // tn
    emit = w.dtype != BF16
    assert not emit or n == tm
    once = dict(pipeline_mode=pl.Buffered(1)) if emit else {}
    row = pl.BlockSpec((tm, tn), lambda i, j: (i, 0), **once)
    wspec = pl.BlockSpec((d, tn), lambda i, j: (0, j))
    mat = jax.ShapeDtypeStruct((n, width), F32)
    return pl.pallas_call(
        _in_proj_kernel,
        grid=(n // tm, nt),
        in_specs=[pl.BlockSpec((tm, d), lambda i, j: (i, 0), **once),
                  _mod_spec(mod3, tm, rows_per_batch, 0, 2),
                  _mod_spec(mod3, tm, rows_per_batch, 1, 2),
                  wspec],
        out_specs=[pl.BlockSpec((tn // LANES, tm, LANES), lambda i, j: (0, i, 0), **once), row, row, row,
                   pl.BlockSpec((tm, tn), lambda i, j: (i, jnp.maximum(j - 4, 0)))] + ([wspec] if emit else []),
        out_shape=[jax.ShapeDtypeStruct((width // LANES, n, LANES), F32), mat, mat, mat,
                   jax.ShapeDtypeStruct((n, (nt - 4) * tn), F32)]
        + ([jax.ShapeDtypeStruct(w.shape, BF16)] if emit else []),
        scratch_shapes=[pltpu.VMEM((tm, d), BF16)],
        compiler_params=_params("arbitrary", "arbitrary"),
        name="in_proj",
    )(x, mod3, mod3, w)


def _ssm_prep_kernel(are_ref, aim_ref, ldt_ref, bre_ref, bim_ref, cre_ref, cim_ref, d_ref,
                     pd_ref, qre_ref, qim_ref, kt_ref, lpre_ref, lpim_ref, *, npow, gt):
    c = SSM_GROUP
    p = SSM_STATE
    a_re = are_ref[...]
    a_im = aim_ref[...]
    dt = jnp.exp(ldt_ref[...])
    mag = jnp.exp(dt * a_re)
    ab_re = mag * jnp.cos(dt * a_im)
    ab_im = mag * jnp.sin(dt * a_im)
    den = a_re * a_re + a_im * a_im
    f_re = ((ab_re - 1.0) * a_re + ab_im * a_im) / den
    f_im = (ab_im * a_re - (ab_re - 1.0) * a_im) / den
    b_re = bre_ref[...]
    b_im = bim_ref[...]
    bb_re = f_re[:, None, :] * b_re - f_im[:, None, :] * b_im
    bb_im = f_re[:, None, :] * b_im + f_im[:, None, :] * b_re
    c_re = cre_ref[...]
    c_im = cim_ref[...]
    low = lax.broadcasted_iota(jnp.int32, (1, 1, 2 * p), 2) < p
    p_re = jnp.ones_like(a_re)
    p_im = jnp.zeros_like(a_re)
    g_re_all, g_im_all = [], []
    for n in range(npow):
        pr = p_re[:, None, :]
        pi = p_im[:, None, :]
        pw_re = pr * bb_re - pi * bb_im
        pw_im = pr * bb_im + pi * bb_re
        pd_ref[n] = jnp.concatenate([pw_re, pw_im], axis=-1).reshape(gt * c, 4 * p)
        g_re = c_re * pr - c_im * pi
        g_im = -(c_re * pi + c_im * pr)
        g_re_all.append(g_re)
        g_im_all.append(g_im)
        lpre_ref[n] = p_re
        lpim_ref[n] = p_im
        p_re, p_im = p_re * ab_re - p_im * ab_im, p_re * ab_im + p_im * ab_re
    g_re_all = jnp.concatenate(g_re_all, axis=1)
    g_im_all = jnp.concatenate(g_im_all, axis=1)
    for gg in range(0, gt, 2):
        sl = slice(gg // 2 * 2 * p, (gg // 2 + 1) * 2 * p)
        qre_ref[:, sl] = jnp.where(low[0], g_re_all[gg], g_re_all[gg + 1])
        qim_ref[:, sl] = jnp.where(low[0], g_im_all[gg], g_im_all[gg + 1])
    rows = kt_ref.shape[2]
    zpad = jnp.zeros((gt, rows - npow * c, 2 * p), F32)
    dn = (((2,), (2,)), ((0,), (0,)))
    kt = (lax.dot_general(jnp.where(low, bb_re, 0.0), jnp.concatenate([g_re_all, zpad], axis=1), dn,
                          precision=lax.Precision.HIGHEST, preferred_element_type=F32)
          + lax.dot_general(jnp.where(low, bb_im, 0.0), jnp.concatenate([g_im_all, zpad], axis=1), dn,
                            precision=lax.Precision.HIGHEST, preferred_element_type=F32))
    eye = (lax.broadcasted_iota(jnp.int32, (c, rows), 0) == lax.broadcasted_iota(jnp.int32, (c, rows), 1))
    kt_ref[...] = kt + jnp.where(eye[None], d_ref[...], 0.0)


def _ssm_prep(a_re, a_im, log_dt, b_re, b_im, c_re, c_im, d_skip, npow):
    g, p = a_re.shape
    c = SSM_GROUP
    gt = 8
    kt_w = -(-npow * c // LANES) * LANES
    dup = lambda a: jnp.concatenate([a, a], axis=-1)
    gp = pl.BlockSpec((gt, 2 * p), lambda i: (i, 0))
    gcp = pl.BlockSpec((gt, c, 2 * p), lambda i: (i, 0, 0))
    qspec = pl.BlockSpec((npow * c, gt * p), lambda i: (0, i))
    lspec = pl.BlockSpec((npow, gt, 2 * p), lambda i: (0, i, 0))
    return pl.pallas_call(
        functools.partial(_ssm_prep_kernel, npow=npow, gt=gt),
        grid=(g // gt,),
        in_specs=[gp, gp, pl.BlockSpec((gt, 1), lambda i: (i, 0)), gcp, gcp, gcp, gcp,
                  pl.BlockSpec((gt, c, 1), lambda i: (i, 0, 0))],
        out_specs=[pl.BlockSpec((npow, gt * c, 4 * p), lambda i: (0, i, 0)), qspec, qspec,
                   pl.BlockSpec((gt, c, kt_w), lambda i: (i, 0, 0)), lspec, lspec],
        out_shape=[jax.ShapeDtypeStruct((npow, g * c, 4 * p), F32),
                   jax.ShapeDtypeStruct((npow * c, g * p), F32),
                   jax.ShapeDtypeStruct((npow * c, g * p), F32),
                   jax.ShapeDtypeStruct((g, c, kt_w), F32),
                   jax.ShapeDtypeStruct((npow, g, 2 * p), F32),
                   jax.ShapeDtypeStruct((npow, g, 2 * p), F32)],
        compiler_params=_params("arbitrary"),
        name="ssm_prep",
    )(dup(a_re), dup(a_im), log_dt.reshape(g, 1), dup(jnp.swapaxes(b_re, 1, 2)), dup(jnp.swapaxes(b_im, 1, 2)),
      dup(c_re), dup(c_im), d_skip.reshape(g, c, 1))


def _toeplitz_kernel(kt_ref, w_ref, *, t_max):
    c = SSM_GROUP
    kw = kt_ref.shape[2]
    x = kt_ref[...].reshape(BLOCK_W, kw).astype(BF16)
    mask = _group_mask(BLOCK_W, BLOCK_W, c, c)
    sel_r = lax.broadcasted_iota(jnp.int32, (kw, BLOCK_W), 0)
    sel_c = lax.broadcasted_iota(jnp.int32, (kw, BLOCK_W), 1)
    for n in range(t_max):
        e = jnp.where((sel_r // c == n) & (sel_r % c == sel_c % c), 1.0, 0.0).astype(BF16)
        tile = jnp.where(mask, _dot(x, e), 0.0)
        w_ref[pl.ds((t_max - 1 - n) * BLOCK_W, BLOCK_W), :] = tile.astype(w_ref.dtype)


def _toeplitz_tiles(kt, t_max):
    g, c, kw = kt.shape
    gb = g // GROUPS_PER_BLOCK
    rows = t_max * BLOCK_W
    return pl.pallas_call(
        functools.partial(_toeplitz_kernel, t_max=t_max),
        grid=(gb,),
        in_specs=[pl.BlockSpec((GROUPS_PER_BLOCK, c, kw), lambda b: (b, 0, 0))],
        out_specs=pl.BlockSpec((None, rows, BLOCK_W), lambda b: (b, 0, 0)),
        out_shape=jax.ShapeDtypeStruct((gb, rows, BLOCK_W), BF16),
        compiler_params=_params("arbitrary"),
        name="ssm_toeplitz",
    )(kt)


def _group_mask(rows, cols, row_div, col_div):
    return (lax.broadcasted_iota(jnp.int32, (rows, cols), 0) // row_div
            == lax.broadcasted_iota(jnp.int32, (rows, cols), 1) // col_div)


def _ssm_kernel(u_ref, w_ref, pd_ref, qre_ref, qim_ref, s0re_ref, s0im_ref, lre_ref, lim_ref,
                y_ref, fre_ref, fim_ref, ucat_ref, v_ref, s_ref, *, t_chunk, nseq, nc):
    c = SSM_GROUP
    rb = nseq * nc
    t_max = w_ref.shape[0] // BLOCK_W
    for t in range(t_chunk):
        for s in range(BLOCK_W // LANES):
            lo = t * BLOCK_W + s * LANES
            ucat_ref[:, lo:lo + LANES] = u_ref[s, pl.ds(t, rb, stride=t_chunk), :].astype(BF16)
    mask = _group_mask(BLOCK_W, STATE_W, c, SSM_STATE)
    reps = STATE_W // LANES
    acc = jnp.zeros((rb, 2 * STATE_W), F32)
    for t in range(t_chunk):
        blk = pd_ref[t_chunk - 1 - t]
        w_re = jnp.where(mask, jnp.concatenate([blk[:, :LANES]] * reps, axis=1), 0.0)
        w_im = jnp.where(mask, jnp.concatenate([blk[:, LANES:]] * reps, axis=1), 0.0)
        w = jnp.concatenate([w_re, w_im], axis=1).astype(BF16)
        acc = acc + _dot(ucat_ref[:, t * BLOCK_W:(t + 1) * BLOCK_W], w)
    v_ref[0] = acc[:, :STATE_W]
    v_ref[1] = acc[:, STATE_W:]
    ar = lre_ref[...]
    ai = lim_ref[...]
    if nc == 1:
        sr = s0re_ref[...]
        si = s0im_ref[...]
        s_ref[0] = sr
        s_ref[1] = si
        fre_ref[...] = ar * sr - ai * si + v_ref[0]
        fim_ref[...] = ar * si + ai * sr + v_ref[1]
    else:
        def body(k, carry):
            new = []
            for b in range(nseq):
                sr, si = carry[2 * b], carry[2 * b + 1]
                row = pl.ds(b * nc + k, 1)
                s_ref[0, row, :] = sr
                s_ref[1, row, :] = si
                new.append(ar * sr - ai * si + v_ref[0, row, :])
                new.append(ar * si + ai * sr + v_ref[1, row, :])
            return tuple(new)

        init = []
        for b in range(nseq):
            init += [s0re_ref[b:b + 1, :], s0im_ref[b:b + 1, :]]
        fin = lax.fori_loop(0, nc, body, tuple(init))
        for b in range(nseq):
            fre_ref[b:b + 1, :] = fin[2 * b]
            fim_ref[b:b + 1, :] = fin[2 * b + 1]
    s_re = s_ref[0].astype(BF16)
    s_im = s_ref[1].astype(BF16)
    for t in range(t_chunk):
        rows = slice((t + 1) * c, (t + 2) * c)
        q_re = jnp.where(mask, jnp.concatenate([qre_ref[rows, :]] * GROUPS_PER_BLOCK, axis=0), 0.0).astype(BF16)
        q_im = jnp.where(mask, jnp.concatenate([qim_ref[rows, :]] * GROUPS_PER_BLOCK, axis=0), 0.0).astype(BF16)
        y = (_dot(ucat_ref[:, :(t + 1) * BLOCK_W], w_ref[(t_max - 1 - t) * BLOCK_W:, :])
             + _dot_nt(s_re, q_re) + _dot_nt(s_im, q_im))
        for s in range(BLOCK_W // LANES):
            y_ref[s, pl.ds(t, rb, stride=t_chunk), :] = y[:, s * LANES:(s + 1) * LANES]


def _ssm(u_slabs, w_t, pd, q_re, q_im, s0_re, s0_im, lam_re, lam_im, t_chunk, nb, nc):
    gb = w_t.shape[0]
    c = SSM_GROUP
    nseq = max(1, min(nb, 256 // nc))
    nblk = nb // nseq
    rb = nseq * nc
    per = BLOCK_W // LANES
    gp = gb * STATE_W
    uspec = pl.BlockSpec((per, rb * t_chunk, LANES), lambda g, i: (g, i, 0))
    qspec = pl.BlockSpec(((t_chunk + 1) * c, STATE_W), lambda g, i: (0, g))
    sspec = pl.BlockSpec((None, nseq, STATE_W), lambda g, i: (i, 0, g))
    lspec = pl.BlockSpec((1, STATE_W), lambda g, i: (0, g))
    fshape = jax.ShapeDtypeStruct((nblk, nseq, gp), F32)
    y, f_re, f_im = pl.pallas_call(
        functools.partial(_ssm_kernel, t_chunk=t_chunk, nseq=nseq, nc=nc),
        grid=(gb, nblk),
        in_specs=[uspec,
                  pl.BlockSpec((None,) + w_t.shape[1:], lambda g, i: (g, 0, 0)),
                  pl.BlockSpec((t_chunk, BLOCK_W, 2 * LANES), lambda g, i: (0, g, 0)),
                  qspec, qspec, sspec, sspec, lspec, lspec],
        out_specs=[uspec, sspec, sspec],
        out_shape=[jax.ShapeDtypeStruct(u_slabs.shape, F32), fshape, fshape],
        scratch_shapes=[pltpu.VMEM((rb, t_chunk * BLOCK_W), BF16), pltpu.VMEM((2, rb, STATE_W), F32),
                        pltpu.VMEM((2, rb, STATE_W), F32)],
        compiler_params=_params("arbitrary", "arbitrary"),
        name="ssm_chunked",
    )(u_slabs, w_t, pd, q_re, q_im, s0_re.reshape(nblk, nseq, gp), s0_im.reshape(nblk, nseq, gp), lam_re, lam_im)
    return y, f_re.reshape(nb, gp), f_im.reshape(nb, gp)


def _lambda(lq1_ref, lk1_ref, lq2_ref, lk2_ref, lambda_init):
    return (jnp.exp(jnp.sum(lq1_ref[...] * lk1_ref[...], axis=-1, keepdims=True))
            - jnp.exp(jnp.sum(lq2_ref[...] * lk2_ref[...], axis=-1, keepdims=True)) + lambda_init)


def _bias_columns(n, off, pieces, key_side):
    hw = 2 * HEAD_DIM
    lane = lax.broadcasted_iota(jnp.int32, (1, hw), 1) - off
    valid = (lane >= 0) & (lane < 4 * len(pieces))
    a = lane // 4
    kap = lane % 4
    s_lane = pieces[-1]
    for i in range(len(pieces) - 2, -1, -1):
        s_lane = jnp.where(a == i, pieces[i], s_lane)
    pos = lax.broadcasted_iota(jnp.int32, (n, 1), 0)
    hi = (pos // 64 * 64).astype(F32)
    lo = (pos % 64).astype(F32)
    if key_side:
        val = jnp.where(kap < 2, s_lane, jnp.where(kap == 2, hi, lo))
    else:
        val = jnp.where(kap == 0, -hi, jnp.where(kap == 1, -lo, s_lane))
    return jnp.where(valid, val, 0.0)


def _attn_kernel(slope_ref, q_ref, k_ref, v_ref, lq1_ref, lk1_ref, lq2_ref, lk2_ref, g_ref, o_ref,
                 rel_ref, ka_ref, qb_ref, m_ref, l_ref, acc_ref, *, tq, tk, nq, lambda_init):
    h = pl.program_id(1)
    qi = pl.program_id(2)
    hd = HEAD_DIM
    hw = 2 * hd
    slope2 = slope_ref[h] * LOG2E
    first = lax.broadcasted_iota(jnp.int32, (1, hw), 1) < hd

    @pl.when(qi == 0)
    def _():
        sv = jnp.full((1, hw), slope2, F32)
        s1 = sv.astype(BF16).astype(F32)
        s2 = (sv - s1).astype(BF16).astype(F32)
        s3 = (sv - s1 - s2).astype(BF16).astype(F32)
        pieces = (s1, s2, s3)
        qb_ref[0] = _bias_columns(tq, hd, pieces, False)
        qb_ref[1] = _bias_columns(tq, 0, pieces, False)
        ka_ref[0] = _bias_columns(tk, hd, pieces, True)
        ka_ref[1] = _bias_columns(tk, 0, pieces, True)
        rel_ref[...] = (lax.broadcasted_iota(jnp.int32, (tk, tq), 1)
                        - lax.broadcasted_iota(jnp.int32, (tk, tq), 0)).astype(F32)

    q = q_ref[...] * (hd ** -0.5 * LOG2E)
    qm = (jnp.where(first, q, qb_ref[0]).astype(BF16), jnp.where(first, qb_ref[1], q).astype(BF16))
    m_ref[...] = jnp.full_like(m_ref, NEG_BIG)
    l_ref[...] = jnp.zeros_like(l_ref)
    acc_ref[...] = jnp.zeros_like(acc_ref)

    def scores(k0, nk, q0, nq_, masked):
        kp = k0 % tk
        k = k_ref[pl.ds(k0, nk), :]
        km = (jnp.where(first, k, ka_ref[0, kp:kp + nk]).astype(BF16),
              jnp.where(first, ka_ref[1, kp:kp + nk], k).astype(BF16))
        ts = [_dot_nt(km[mp], qm[mp][q0:q0 + nq_]) for mp in range(2)]
        if masked:
            keep = rel_ref[:nk, :nq_] >= 0.0
            ts = [jnp.where(keep, t, -jnp.inf) for t in ts]
        return ts

    def update(k0, nk, q0, nq_, ts, c):
        v = v_ref[pl.ds(k0, nk), :].astype(BF16)
        ones = jnp.ones((16, nk), BF16)
        lanes = slice(q0, q0 + nq_)
        m_old = [m_ref[mp, :, lanes] for mp in range(2)]
        m_new = [jnp.maximum(m_old[mp], jnp.max(ts[mp], axis=0, keepdims=True) + c) for mp in range(2)]
        ps = [jnp.exp2(ts[mp] - (m_new[mp] - c)).astype(BF16) for mp in range(2)]
        for mp in range(2):
            alpha = jnp.exp2(m_old[mp] - m_new[mp])
            l_ref[mp, :, lanes] = alpha * l_ref[mp, :, lanes] + _dot(ones, ps[mp])[0:1]
            acc_ref[mp, :, lanes] = alpha * acc_ref[mp, :, lanes] + _dot_tn(v, ps[mp])
            m_ref[mp, :, lanes] = m_new[mp]

    half = tk // 2
    for q_blk in range(nq):
        @pl.when(qi == q_blk)
        def _(q_blk=q_blk):
            pieces_ = [(kj * tk, tk, 0, tq, False, -slope2 * float((q_blk - kj) * tq)) for kj in range(q_blk)]
            pieces_ += [(q_blk * tk, half, 0, tq, True, 0.0), (q_blk * tk + half, half, half, tq - half, True, 0.0)]
            nxt = scores(*pieces_[0][:5])
            for n, pc in enumerate(pieces_):
                cur = nxt
                if n + 1 < len(pieces_):
                    nxt = scores(*pieces_[n + 1][:5])
                update(*pc[:4], cur, pc[5])

    lam = _lambda(lq1_ref, lk1_ref, lq2_ref, lk2_ref, lambda_init)
    o = acc_ref[0] / l_ref[0] - lam * (acc_ref[1] / l_ref[1])
    o = o * lax.rsqrt(jnp.mean(o * o, axis=0, keepdims=True) + LN_EPS) * g_ref[...] * (1.0 - lambda_init)
    o_ref[...] = o.T.astype(o_ref.dtype)


def _attention(q, k, v, slopes, lam_vecs, subln_g, nbatch, seq, lambda_init, tq, tk):
    n, da = q.shape
    nh = da // (2 * HEAD_DIM)
    nq = seq // tq
    hw = 2 * HEAD_DIM
    vec = pl.BlockSpec((1, HEAD_DIM), lambda b, h, i: (0, 0))
    return pl.pallas_call(
        functools.partial(_attn_kernel, tq=tq, tk=tk, nq=nq, lambda_init=lambda_init),
        grid=(nbatch, nh, nq),
        in_specs=[pl.BlockSpec(memory_space=pltpu.SMEM),
                  pl.BlockSpec((tq, hw), lambda b, h, i: (b * nq + i, h)),
                  pl.BlockSpec((seq, hw), lambda b, h, i: (b, h)),
                  pl.BlockSpec((seq, hw), lambda b, h, i: (b, h)),
                  vec, vec, vec, vec,
                  pl.BlockSpec((hw, 1), lambda b, h, i: (0, 0))],
        out_specs=pl.BlockSpec((tq, hw), lambda b, h, i: (b * nq + i, h)),
        out_shape=jax.ShapeDtypeStruct((n, da), BF16),
        scratch_shapes=[pltpu.VMEM((tk, tq), F32), pltpu.VMEM((2, tk, hw), F32), pltpu.VMEM((2, tq, hw), F32),
                        pltpu.VMEM((2, 1, tq), F32),
                        pltpu.VMEM((2, 1, tq), F32), pltpu.VMEM((2, hw, tq), F32)],
        compiler_params=_params("arbitrary", "arbitrary", "arbitrary"),
        name="prompt_attention",
    )(slopes, q, k, v, *lam_vecs, subln_g.reshape(hw, 1))


def _dec_attn_kernel(pt_ref, q_ref, kn_ref, vn_ref, *rest, n_pages, page, tdec, nh, lambda_init):
    del pt_ref
    k_refs = rest[:n_pages] + (kn_ref,)
    v_refs = rest[n_pages:2 * n_pages] + (vn_ref,)
    lq1_ref, lk1_ref, lq2_ref, lk2_ref, g_ref, o_ref = rest[2 * n_pages:]
    hd = HEAD_DIM
    hw = 2 * hd
    ncol = 2 * nh * tdec
    past = n_pages * page
    q = q_ref[...] * (hd ** -0.5 * LOG2E)
    rid = lax.broadcasted_iota(jnp.int32, (ncol, hw), 0)
    cid = lax.broadcasted_iota(jnp.int32, (ncol, hw), 1)
    qt = jnp.zeros((ncol, hw), F32)
    for h in range(nh):
        for i in range(tdec):
            sel = (rid // (2 * tdec) == h) & (rid % tdec == i)
            qt = jnp.where(sel, jnp.broadcast_to(q[i:i + 1, h * hw:(h + 1) * hw], (ncol, hw)), qt)
    qt = jnp.where((rid // tdec) % 2 == cid // hd, qt, 0.0)
    zq = jnp.zeros_like(qt)
    qt2 = jnp.concatenate([jnp.concatenate([qt, zq], axis=1), jnp.concatenate([zq, qt], axis=1)],
                          axis=0).astype(BF16)
    col2 = lax.broadcasted_iota(jnp.int32, (1, 2 * ncol), 1)
    second = col2 // ncol
    col = col2 % ncol
    qidx = col % tdec
    hcol = col // (2 * tdec)
    slope = jnp.exp2((hcol + 1).astype(F32) * (-8.0 / nh)) * LOG2E

    def rows(n):
        r = lax.broadcasted_iota(jnp.int32, (n, 1), 0)
        return r // nh, r % nh

    tok, hrow = rows(page * nh)
    base = jnp.where(hrow == hcol, -slope * (past + qidx - tok - second * page).astype(F32), -jnp.inf)
    tokn, hrown = rows(tdec * nh)
    basen = jnp.where((hrown == hcol) & (qidx >= tokn) & (second == 0),
                      -slope * (qidx - tokn).astype(F32), -jnp.inf)
    blocks = [(j, j + 1, base, slope * float(j * page)) for j in range(0, n_pages - 1, 2)]
    if n_pages % 2:
        blocks.append((n_pages - 1, None, jnp.where(second == 0, base, -jnp.inf), slope * float((n_pages - 1) * page)))
    blocks.append((n_pages, None, basen, jnp.zeros_like(slope)))

    def pair(refs, a, b):
        xa = refs[a][...]
        xb = refs[b][...] if b is not None else jnp.zeros_like(xa)
        return jnp.concatenate([xa, xb], axis=1).astype(BF16)

    t_list = []
    m2 = jnp.full((1, 2 * ncol), NEG_BIG, F32)
    for a, b, bias, c in blocks:
        t = _dot_nt(pair(k_refs, a, b), qt2) + bias
        m2 = jnp.maximum(m2, jnp.max(t, axis=0, keepdims=True) + c)
        t_list.append(t)
    m = jnp.maximum(m2[:, :ncol], m2[:, ncol:])
    m2 = jnp.concatenate([m, m], axis=1)
    l2 = jnp.zeros((1, 2 * ncol), F32)
    acc2 = jnp.zeros((2 * ncol, 2 * hw), F32)
    for (a, b, bias, c), t in zip(blocks, t_list):
        p = jnp.exp2(t - (m2 - c))
        l2 = l2 + jnp.sum(p, axis=0, keepdims=True)
        p = p.astype(BF16)
        v = pair(v_refs, a, b)
        if p.shape[0] < LANES:
            pad = LANES - p.shape[0]
            p = jnp.concatenate([p, jnp.zeros((pad, 2 * ncol), BF16)], axis=0)
            v = jnp.concatenate([v, jnp.zeros((pad, 2 * hw), BF16)], axis=0)
        acc2 = acc2 + _dot_tn(p, v)
    l = l2[:, :ncol] + l2[:, ncol:]
    acc = acc2[:ncol, :hw] + acc2[ncol:, hw:]
    eye = lax.broadcasted_iota(jnp.int32, (ncol, ncol), 0) == lax.broadcasted_iota(jnp.int32, (ncol, ncol), 1)
    lcol = jnp.sum(jnp.where(eye, jnp.broadcast_to(l, (ncol, ncol)), 0.0), axis=1, keepdims=True)
    acc = acc / lcol
    lam = _lambda(lq1_ref, lk1_ref, lq2_ref, lk2_ref, lambda_init)
    g = g_ref[...]
    for h in range(nh):
        blk = acc[h * 2 * tdec:(h + 1) * 2 * tdec]
        o = blk[:tdec] - lam * blk[tdec:]
        o = o * lax.rsqrt(jnp.mean(o * o, axis=-1, keepdims=True) + LN_EPS) * g * (1.0 - lambda_init)
        o_ref[:, h * hw:(h + 1) * hw] = o.astype(o_ref.dtype)


def _decode_attention(q3, k3, v3, cache_k, cache_v, page_table, lam_vecs, subln_g, lambda_init):
    nb, tdec, da = q3.shape
    n_pages = page_table.shape[1]
    rows, hw = cache_k.shape[1:]
    nh = da // hw
    page = rows // nh
    new = pl.BlockSpec((None, tdec, da), lambda b, pt: (b, 0, 0))
    newp = pl.BlockSpec((None, tdec * nh, hw), lambda b, pt: (b, 0, 0))
    pages = [pl.BlockSpec((None, rows, hw), lambda b, pt, j=j: (pt[b, j], 0, 0)) for j in range(n_pages)]
    vec = pl.BlockSpec((1, HEAD_DIM), lambda b, pt: (0, 0))
    return pl.pallas_call(
        functools.partial(_dec_attn_kernel, n_pages=n_pages, page=page, tdec=tdec, nh=nh, lambda_init=lambda_init),
        grid_spec=pltpu.PrefetchScalarGridSpec(
            num_scalar_prefetch=1,
            grid=(nb,),
            in_specs=[new, newp, newp] + pages + pages + [vec, vec, vec, vec,
                                                          pl.BlockSpec((1, hw), lambda b, pt: (0, 0))],
            out_specs=new),
        out_shape=jax.ShapeDtypeStruct((nb, tdec, da), BF16),
        compiler_params=_params("arbitrary"),
        name="decode_attention",
    )(page_table, q3, k3.reshape(nb, tdec * nh, hw), v3.reshape(nb, tdec * nh, hw),
      *([cache_k] * n_pages), *([cache_v] * n_pages), *lam_vecs, subln_g)


def _mix_in_kernel(y_ref, o_ref, gs_ref, ga_ref, wglu_ref, wus_ref, wua_ref, out_ref):
    z = jax.nn.gelu(jnp.concatenate([y_ref[i] for i in range(y_ref.shape[0])], axis=1))
    zz = z * jax.nn.sigmoid(_dot(z.astype(BF16), wglu_ref[...]))
    y_ssm = _dot(zz.astype(BF16), wus_ref[...])
    y_att = _dot(o_ref[...], wua_ref[...])
    out_ref[...] = (jax.nn.sigmoid(gs_ref[...]) * y_ssm + jax.nn.sigmoid(ga_ref[...]) * y_att).astype(out_ref.dtype)


def _mix_in(y, o, gates, w_glu, w_up_ssm, w_up_att, tm):
    nslab, n, _ = y.shape
    da = o.shape[1]
    d = w_up_ssm.shape[1]
    full = lambda a: pl.BlockSpec(a.shape, lambda i: (0, 0))
    return pl.pallas_call(
        _mix_in_kernel,
        grid=(n // tm,),
        in_specs=[pl.BlockSpec((nslab, tm, LANES), lambda i: (0, i, 0)),
                  pl.BlockSpec((tm, da), lambda i: (i, 0)),
                  pl.BlockSpec((tm, d), lambda i: (i, 0)),
                  pl.BlockSpec((tm, d), lambda i: (i, 1)),
                  full(w_glu), full(w_up_ssm), full(w_up_att)],
        out_specs=pl.BlockSpec((tm, d), lambda i: (i, 0)),
        out_shape=jax.ShapeDtypeStruct((n, d), BF16),
        compiler_params=_params("arbitrary"),
        name="branch_mix",
    )(y, o, gates, gates, w_glu, w_up_ssm, w_up_att)


def _out_proj_kernel(mix_ref, x_ref, g1_ref, wo_ref, lg_ref, lb_ref, out_ref, *, alpha):
    r = alpha * x_ref[...] + _mod_val(g1_ref) * _dot(mix_ref[...], wo_ref[...])
    out_ref[...] = _ln(r) * lg_ref[...] + lb_ref[...]


def _out_proj(mix, x, mod3, rows_per_batch, w_o, ln_g, ln_b, alpha, tm):
    n, d = x.shape
    row = pl.BlockSpec((tm, d), lambda i: (i, 0))
    vec = pl.BlockSpec((1, d), lambda i: (0, 0))
    return pl.pallas_call(
        functools.partial(_out_proj_kernel, alpha=alpha),
        grid=(n // tm,),
        in_specs=[row, row, _mod_spec(mod3, tm, rows_per_batch, 2, 1),
                  pl.BlockSpec(w_o.shape, lambda i: (0, 0)), vec, vec],
        out_specs=row,
        out_shape=jax.ShapeDtypeStruct((n, d), F32),
        compiler_params=_params("arbitrary"),
        name="out_proj_ln1",
    )(mix, x, mod3, w_o, ln_g, ln_b)


def _ffn_kernel(x_ref, sh_ref, sc_ref, g2_ref, w1_ref, w2_ref, lg_ref, lb_ref, out_ref, *rest, alpha, nf):
    f = pl.program_id(1)
    h_ref, acc_ref = rest[-2:]
    if len(rest) == 4:
        w1b_ref, w2b_ref = rest[:2]
        w1b_ref[...] = w1_ref[...].astype(w1b_ref.dtype)
        w2b_ref[...] = w2_ref[...].astype(w2b_ref.dtype)
    else:
        w1b_ref, w2b_ref = w1_ref, w2_ref

    @pl.when(f == 0)
    def _():
        h_ref[...] = (_ln(x_ref[...]) * (1.0 + _mod_val(sc_ref)) + _mod_val(sh_ref)).astype(h_ref.dtype)
        acc_ref[...] = jnp.zeros_like(acc_ref)

    a = jnp.maximum(_dot(h_ref[...], w1b_ref[...]), 0.0)
    acc_ref[...] += _dot((a * a).astype(BF16), w2b_ref[...])

    @pl.when(f == nf - 1)
    def _():
        r = alpha * x_ref[...] + _mod_val(g2_ref) * acc_ref[...]
        out_ref[...] = _ln(r) * lg_ref[...] + lb_ref[...]


def _ffn(x, mod3, rows_per_batch, w1, w2, ln_g, ln_b, alpha, tm, tf):
    n, d = x.shape
    dff = w1.shape[1]
    nf = dff // tf
    emit = w1.dtype != BF16
    assert not emit or n == tm
    once = dict(pipeline_mode=pl.Buffered(1)) if emit else {}
    row = pl.BlockSpec((tm, d), lambda i, f: (i, 0), **once)
    vec = pl.BlockSpec((1, d), lambda i, f: (0, 0))
    w1spec = pl.BlockSpec((d, tf), lambda i, f: (0, f))
    w2spec = pl.BlockSpec((tf, d), lambda i, f: (f, 0))
    res = pl.pallas_call(
        functools.partial(_ffn_kernel, alpha=alpha, nf=nf),
        grid=(n // tm, nf),
        in_specs=[row,
                  _mod_spec(mod3, tm, rows_per_batch, 3, 2),
                  _mod_spec(mod3, tm, rows_per_batch, 4, 2),
                  _mod_spec(mod3, tm, rows_per_batch, 5, 2),
                  w1spec, w2spec, vec, vec],
        out_specs=[row] + ([w1spec, w2spec] if emit else []),
        out_shape=[jax.ShapeDtypeStruct((n, d), F32)]
        + ([jax.ShapeDtypeStruct(w1.shape, BF16), jax.ShapeDtypeStruct(w2.shape, BF16)] if emit else []),
        scratch_shapes=[pltpu.VMEM((tm, d), BF16), pltpu.VMEM((tm, d), F32)],
        compiler_params=_params("arbitrary", "arbitrary"),
        name="ffn_ln2",
    )(x, mod3, mod3, mod3, w1, w2, ln_g, ln_b)
    return res


def _trunk_layer(x3, mod3, s0, cache, page_table, wts, tabs, lambda_init, alpha):
    nbatch, seq, d = x3.shape
    n = nbatch * seq
    x = x3.reshape(n, d)
    ds = wts["w_glu"].shape[0]
    da = wts["w_up_att"].shape[0]
    nh = da // (2 * HEAD_DIM)
    tm = min(512, seq) if mod3.shape[1] == 1 else min(512, n)
    assert ds == da
    u, q, k, v, gates, *w_in_b = _in_proj(x, mod3, seq, wts["w_in"], ds, tm)

    pd, q_re, q_im, w_t, lp_re, lp_im = tabs
    t_chunk = PROMPT_CHUNK if seq % PROMPT_CHUNK == 0 else seq
    gp = lp_re.shape[1] * SSM_STATE
    lam_re = lp_re[t_chunk, :, :SSM_STATE].reshape(1, gp)
    lam_im = lp_im[t_chunk, :, :SSM_STATE].reshape(1, gp)
    y, f_re, f_im = _ssm(u, w_t, pd, q_re, q_im, s0[0], s0[1], lam_re, lam_im, t_chunk, nbatch, seq // t_chunk)

    lam_vecs = wts["lam_vecs"]
    if cache is None:
        slopes = jnp.asarray([2.0 ** (-8.0 * (i + 1) / nh) for i in range(nh)], F32)
        tq = min(512, seq)
        o = _attention(q, k, v, slopes, lam_vecs, wts["subln_g"], nbatch, seq, lambda_init, tq, tq)
    else:
        o = _decode_attention(q.reshape(nbatch, seq, da), k.reshape(nbatch, seq, da), v.reshape(nbatch, seq, da),
                              cache[0], cache[1], page_table, lam_vecs, wts["subln_g"], lambda_init)
        o = o.reshape(n, da)

    mix = _mix_in(y, o, gates, wts["w_glu"], wts["w_up_ssm"], wts["w_up_att"], tm)
    x1 = _out_proj(mix, x, mod3, seq, wts["w_o"], wts["ln1_g"], wts["ln1_b"], alpha, tm)
    tf = 1024 if wts["w_ff1"].dtype == BF16 else 512
    x2, *w_ff_b = _ffn(x1, mod3, seq, wts["w_ff1"], wts["w_ff2"], wts["ln2_g"], wts["ln2_b"], alpha, tm,
                       min(tf, wts["w_ff1"].shape[1]))
    return (x2.reshape(nbatch, seq, d), k.reshape(nbatch, seq, nh, 2 * HEAD_DIM),
            v.reshape(nbatch, seq, nh, 2 * HEAD_DIM), f_re, f_im, tuple(w_in_b) + tuple(w_ff_b))


def kernel(x_prompt, x_sample, c_prompt, c_sample, cache_k, cache_v, state_ssm_re, state_ssm_im, page_table,
           w_ada, b_ada, w_in, ssm_a_re, ssm_a_im, ssm_log_dt, ssm_b_re, ssm_b_im, ssm_c_re, ssm_c_im, ssm_d,
           w_glu, w_up_ssm, lam_q1, lam_k1, lam_q2, lam_k2, subln_g, w_up_att, w_o, ln1_g, ln1_b, w_ff1, w_ff2,
           ln2_g, ln2_b):
    depth = w_in.shape[0]
    bp, seq_p, d = x_prompt.shape
    bs, seq_s, _ = x_sample.shape
    g, p = ssm_a_re.shape[1:]
    alpha = (2 * depth) ** 0.25
    npow = PROMPT_CHUNK + 1
    xp, xs = x_prompt, x_sample
    outs_p, outs_s = [], []
    c_rows = jnp.concatenate([c_sample, c_prompt], axis=0)
    c_rows = jnp.pad(c_rows, ((0, (-(bp + bs)) % 8), (0, 0)))
    n_pool, page, nh, hw = cache_k.shape[1:]
    for l in range(depth):
        lambda_init = 0.8 - 0.6 * math.exp(-0.3 * l)
        mod, mod_s = _ada(c_rows, w_ada[l], b_ada[l].reshape(1, 6 * d), min(1024, d), bs, seq_s)
        mod_p = mod[bs:bs + bp].reshape(bp, 1, 6 * d)
        wts = {
            "w_in": w_in[l], "w_ff1": w_ff1[l], "w_ff2": w_ff2[l],
            "w_glu": w_glu[l].astype(BF16), "w_up_ssm": w_up_ssm[l].astype(BF16),
            "w_up_att": w_up_att[l].astype(BF16), "w_o": w_o[l].astype(BF16),
            "ln1_g": ln1_g[l].reshape(1, d), "ln1_b": ln1_b[l].reshape(1, d),
            "ln2_g": ln2_g[l].reshape(1, d), "ln2_b": ln2_b[l].reshape(1, d),
            "subln_g": subln_g[l].reshape(1, 2 * HEAD_DIM),
            "lam_vecs": tuple(a[l].reshape(1, HEAD_DIM) for a in (lam_q1, lam_k1, lam_q2, lam_k2)),
        }
        tabs = _ssm_prep(ssm_a_re[l], ssm_a_im[l], ssm_log_dt[l], ssm_b_re[l], ssm_b_im[l], ssm_c_re[l],
                         ssm_c_im[l], ssm_d[l], npow)
        tabs = list(tabs)
        tabs[3] = _toeplitz_tiles(tabs[3], PROMPT_CHUNK)
        cache = (cache_k[l].reshape(n_pool, page * nh, hw), cache_v[l].reshape(n_pool, page * nh, hw))
        s0 = (state_ssm_re[l].reshape(bs, g * p), state_ssm_im[l].reshape(bs, g * p))
        res_s = _trunk_layer(xs, mod_s, s0, cache, page_table, wts, tabs, lambda_init, alpha)
        wts["w_in"], wts["w_ff1"], wts["w_ff2"] = res_s[5]
        zeros = jnp.zeros((bp, g * p), F32)
        res_p = _trunk_layer(xp, mod_p, (zeros, zeros), None, None, wts, tabs, lambda_init, alpha)
        xp, xs = res_p[0], res_s[0]
        outs_p.append(res_p[1:5])
        outs_s.append(res_s[1:5])

    def stack(outs, i, shape=None):
        a = jnp.stack([o[i] for o in outs])
        return a if shape is None else a.reshape((depth,) + shape)

    return (xp, xs,
            stack(outs_p, 0), stack(outs_p, 1), stack(outs_p, 2, (bp, g, p)), stack(outs_p, 3, (bp, g, p)),
            stack(outs_s, 0), stack(outs_s, 1), stack(outs_s, 2, (bs, g, p)), stack(outs_s, 3, (bs, g, p)))
```

```python
import functools
import math

import jax
import jax.numpy as jnp
from jax import lax
from jax.experimental import pallas as pl
from jax.experimental.pallas import tpu as pltpu

F32 = jnp.float32
BF16 = jnp.bfloat16

SSM_GROUP = 16
SSM_STATE = 64
HEAD_DIM = 64
LANES = 128
BLOCK_W = 256
GROUPS_PER_BLOCK = BLOCK_W // SSM_GROUP
STATE_W = GROUPS_PER_BLOCK * SSM_STATE
PROMPT_CHUNK = 16
LN_EPS = 1e-5
NEG_BIG = -1e30
LOG2E = 1.4426950408889634
VMEM_LIMIT = 56 * 1024 * 1024


def _params(*sem):
    return pltpu.CompilerParams(dimension_semantics=sem, vmem_limit_bytes=VMEM_LIMIT)


def _ln(x):
    mu = jnp.mean(x, axis=-1, keepdims=True)
    xc = x - mu
    var = jnp.mean(xc * xc, axis=-1, keepdims=True)
    return xc * lax.rsqrt(var + LN_EPS)


def _dot(a, b):
    return jnp.dot(a, b, preferred_element_type=F32)


def _dot_nt(a, b):
    return lax.dot_general(a, b, (((1,), (1,)), ((), ())), preferred_element_type=F32)


def _dot_tn(a, b):
    return lax.dot_general(a, b, (((0,), (0,)), ((), ())), preferred_element_type=F32)


def _mod_spec(mod3, tm, rows_per_batch, col, ngrid):
    if mod3.shape[1] == 1:
        d = mod3.shape[2] // 6
        per = rows_per_batch // tm
        if ngrid == 1:
            return pl.BlockSpec((None, 1, d), lambda i: (i // per, 0, col))
        return pl.BlockSpec((None, 1, d), lambda i, j: (i // per, 0, col))
    nslab = mod3.shape[0] // 6
    once = pl.Buffered(1)
    if ngrid == 1:
        return pl.BlockSpec((nslab, tm, LANES), lambda i: (col, i, 0), pipeline_mode=once)
    return pl.BlockSpec((nslab, tm, LANES), lambda i, j: (col, i, 0), pipeline_mode=once)


def _mod_val(ref):
    if len(ref.shape) == 2:
        return ref[...]
    return jnp.concatenate([ref[s] for s in range(ref.shape[0])], axis=1)


def _ada_kernel(c_ref, w_ref, b_ref, o_ref, os_ref, *, nrep, seq):
    c = c_ref[...]
    a = (c * jax.nn.sigmoid(c)).astype(BF16)
    res = _dot(a, w_ref[...].astype(BF16)) + b_ref[...]
    o_ref[...] = res
    for s in range(os_ref.shape[0]):
        for t in range(seq):
            os_ref[s, pl.ds(t, nrep, stride=seq), :] = res[:nrep, s * LANES:(s + 1) * LANES]


def _ada(c, w, b, tn, nrep, seq):
    m, d = c.shape
    n = w.shape[1]
    return pl.pallas_call(
        functools.partial(_ada_kernel, nrep=nrep, seq=seq),
        grid=(n // tn,),
        in_specs=[pl.BlockSpec((m, d), lambda j: (0, 0)),
                  pl.BlockSpec((d, tn), lambda j: (0, j)),
                  pl.BlockSpec((1, tn), lambda j: (0, j))],
        out_specs=[pl.BlockSpec((m, tn), lambda j: (0, j)),
                   pl.BlockSpec((tn // LANES, nrep * seq, LANES), lambda j: (j, 0, 0))],
        out_shape=[jax.ShapeDtypeStruct((m, n), F32),
                   jax.ShapeDtypeStruct((n // LANES, nrep * seq, LANES), F32)],
        compiler_params=_params("arbitrary"),
        name="ada_mod",
    )(c, w, b)


def _in_proj_kernel(x_ref, sh_ref, sc_ref, w_ref, u_ref, q_ref, k_ref, v_ref, g_ref, *rest, per):
    j = pl.program_id(1)
    h_ref = rest[-1]
    if len(rest) == 2:
        wb_ref = rest[0]
        sub = wb_ref.shape[2]
        for t in range(wb_ref.shape[0]):
            wb_ref[t] = w_ref[:, t * sub:(t + 1) * sub].astype(wb_ref.dtype)
        weight = lambda: w_ref[...].astype(BF16)
    else:
        weight = lambda: w_ref[...]

    @pl.when(j == 0)
    def _():
        h_ref[...] = (_ln(x_ref[...]) * (1.0 + _mod_val(sc_ref)) + _mod_val(sh_ref)).astype(h_ref.dtype)

    @pl.when(j < per)
    def _():
        acc = _dot(h_ref[...], weight())
        for s in range(u_ref.shape[0]):
            u_ref[s] = acc[:, s * LANES:(s + 1) * LANES]

    for idx, ref in ((1, q_ref), (2, k_ref), (3, v_ref)):
        @pl.when((j >= idx * per) & (j < (idx + 1) * per))
        def _(ref=ref):
            ref[...] = _dot(h_ref[...], weight())

    @pl.when(j >= 4 * per)
    def _():
        g_ref[...] = _dot(h_ref[...], weight())


IN_PROJ_TILE = 512


def _in_proj(x, mod3, rows_per_batch, w, width, tm):
    n, d = x.shape
    emit = w.dtype != BF16
    if emit:
        assert n == tm
        tn = width
        nt = w.shape[1] // tn
        wspec = pl.BlockSpec((d, tn), lambda i, j: (0, j))
    else:
        tn = w.shape[2]
        nt = w.shape[0]
        wspec = pl.BlockSpec((None, d, tn), lambda i, j: (j, 0, 0))
    per = width // tn
    ncol = nt * tn
    once = dict(pipeline_mode=pl.Buffered(1)) if emit else {}
    clamp = lambda j, first: jnp.clip(j - first * per, 0, per - 1)
    row = lambda first: pl.BlockSpec((tm, tn), lambda i, j: (i, clamp(j, first)), **once)
    mat = jax.ShapeDtypeStruct((n, width), F32)
    emit_specs, emit_shapes = [], []
    if emit:
        tile = min(IN_PROJ_TILE, tn)
        emit_specs = [pl.BlockSpec((tn // tile, d, tile), lambda i, j: (j, 0, 0))]
        emit_shapes = [jax.ShapeDtypeStruct((ncol // tile, d, tile), BF16)]
    return pl.pallas_call(
        functools.partial(_in_proj_kernel, per=per),
        grid=(n // tm, nt),
        in_specs=[pl.BlockSpec((tm, d), lambda i, j: (i, 0), **once),
                  _mod_spec(mod3, tm, rows_per_batch, 0, 2),
                  _mod_spec(mod3, tm, rows_per_batch, 1, 2),
                  wspec],
        out_specs=[pl.BlockSpec((tn // LANES, tm, LANES), lambda i, j: (clamp(j, 0), i, 0), **once),
                   row(1), row(2), row(3),
                   pl.BlockSpec((tm, tn), lambda i, j: (i, jnp.maximum(j - 4 * per, 0)))] + emit_specs,
        out_shape=[jax.ShapeDtypeStruct((width // LANES, n, LANES), F32), mat, mat, mat,
                   jax.ShapeDtypeStruct((n, ncol - 4 * width), F32)] + emit_shapes,
        scratch_shapes=[pltpu.VMEM((tm, d), BF16)],
        compiler_params=_params("arbitrary", "arbitrary"),
        name="in_proj",
    )(x, mod3, mod3, w)


def _ssm_prep_kernel(are_ref, aim_ref, ldt_ref, bre_ref, bim_ref, cre_ref, cim_ref, d_ref,
                     pd_ref, qre_ref, qim_ref, kt_ref, lpre_ref, lpim_ref, *, npow, gt):
    c = SSM_GROUP
    p = SSM_STATE
    a_re = are_ref[...]
    a_im = aim_ref[...]
    dt = jnp.exp(ldt_ref[...])
    mag = jnp.exp(dt * a_re)
    ab_re = mag * jnp.cos(dt * a_im)
    ab_im = mag * jnp.sin(dt * a_im)
    den = a_re * a_re + a_im * a_im
    f_re = ((ab_re - 1.0) * a_re + ab_im * a_im) / den
    f_im = (ab_im * a_re - (ab_re - 1.0) * a_im) / den
    b_re = bre_ref[...]
    b_im = bim_ref[...]
    bb_re = f_re[:, None, :] * b_re - f_im[:, None, :] * b_im
    bb_im = f_re[:, None, :] * b_im + f_im[:, None, :] * b_re
    c_re = cre_ref[...]
    c_im = cim_ref[...]
    low = lax.broadcasted_iota(jnp.int32, (1, 1, 2 * p), 2) < p
    p_re = jnp.ones_like(a_re)
    p_im = jnp.zeros_like(a_re)
    g_re_all, g_im_all = [], []
    for n in range(npow):
        pr = p_re[:, None, :]
        pi = p_im[:, None, :]
        pw_re = pr * bb_re - pi * bb_im
        pw_im = pr * bb_im + pi * bb_re
        pd_ref[n] = jnp.concatenate([pw_re, pw_im], axis=-1).reshape(gt * c, 4 * p)
        g_re = c_re * pr - c_im * pi
        g_im = -(c_re * pi + c_im * pr)
        g_re_all.append(g_re)
        g_im_all.append(g_im)
        lpre_ref[n] = p_re
        lpim_ref[n] = p_im
        p_re, p_im = p_re * ab_re - p_im * ab_im, p_re * ab_im + p_im * ab_re
    g_re_all = jnp.concatenate(g_re_all, axis=1)
    g_im_all = jnp.concatenate(g_im_all, axis=1)
    for gg in range(0, gt, 2):
        sl = slice(gg // 2 * 2 * p, (gg // 2 + 1) * 2 * p)
        qre_ref[:, sl] = jnp.where(low[0], g_re_all[gg], g_re_all[gg + 1])
        qim_ref[:, sl] = jnp.where(low[0], g_im_all[gg], g_im_all[gg + 1])
    rows = kt_ref.shape[2]
    zpad = jnp.zeros((gt, rows - npow * c, 2 * p), F32)
    dn = (((2,), (2,)), ((0,), (0,)))
    kt = (lax.dot_general(jnp.where(low, bb_re, 0.0), jnp.concatenate([g_re_all, zpad], axis=1), dn,
                          precision=lax.Precision.HIGHEST, preferred_element_type=F32)
          + lax.dot_general(jnp.where(low, bb_im, 0.0), jnp.concatenate([g_im_all, zpad], axis=1), dn,
                            precision=lax.Precision.HIGHEST, preferred_element_type=F32))
    eye = (lax.broadcasted_iota(jnp.int32, (c, rows), 0) == lax.broadcasted_iota(jnp.int32, (c, rows), 1))
    kt_ref[...] = kt + jnp.where(eye[None], d_ref[...], 0.0)


def _ssm_prep(a_re, a_im, log_dt, b_re, b_im, c_re, c_im, d_skip, npow):
    g, p = a_re.shape
    c = SSM_GROUP
    gt = 8
    kt_w = -(-npow * c // LANES) * LANES
    dup = lambda a: jnp.concatenate([a, a], axis=-1)
    gp = pl.BlockSpec((gt, 2 * p), lambda i: (i, 0))
    gcp = pl.BlockSpec((gt, c, 2 * p), lambda i: (i, 0, 0))
    qspec = pl.BlockSpec((npow * c, gt * p), lambda i: (0, i))
    lspec = pl.BlockSpec((npow, gt, 2 * p), lambda i: (0, i, 0))
    return pl.pallas_call(
        functools.partial(_ssm_prep_kernel, npow=npow, gt=gt),
        grid=(g // gt,),
        in_specs=[gp, gp, pl.BlockSpec((gt, 1), lambda i: (i, 0)), gcp, gcp, gcp, gcp,
                  pl.BlockSpec((gt, c, 1), lambda i: (i, 0, 0))],
        out_specs=[pl.BlockSpec((npow, gt * c, 4 * p), lambda i: (0, i, 0)), qspec, qspec,
                   pl.BlockSpec((gt, c, kt_w), lambda i: (i, 0, 0)), lspec, lspec],
        out_shape=[jax.ShapeDtypeStruct((npow, g * c, 4 * p), F32),
                   jax.ShapeDtypeStruct((npow * c, g * p), F32),
                   jax.ShapeDtypeStruct((npow * c, g * p), F32),
                   jax.ShapeDtypeStruct((g, c, kt_w), F32),
                   jax.ShapeDtypeStruct((npow, g, 2 * p), F32),
                   jax.ShapeDtypeStruct((npow, g, 2 * p), F32)],
        compiler_params=_params("arbitrary"),
        name="ssm_prep",
    )(dup(a_re), dup(a_im), log_dt.reshape(g, 1), dup(jnp.swapaxes(b_re, 1, 2)), dup(jnp.swapaxes(b_im, 1, 2)),
      dup(c_re), dup(c_im), d_skip.reshape(g, c, 1))


def _toeplitz_kernel(kt_ref, w_ref, *, t_max):
    c = SSM_GROUP
    kw = kt_ref.shape[2]
    x = kt_ref[...].reshape(BLOCK_W, kw).astype(BF16)
    mask = _group_mask(BLOCK_W, BLOCK_W, c, c)
    sel_r = lax.broadcasted_iota(jnp.int32, (kw, BLOCK_W), 0)
    sel_c = lax.broadcasted_iota(jnp.int32, (kw, BLOCK_W), 1)
    for n in range(t_max):
        e = jnp.where((sel_r // c == n) & (sel_r % c == sel_c % c), 1.0, 0.0).astype(BF16)
        tile = jnp.where(mask, _dot(x, e), 0.0)
        w_ref[pl.ds((t_max - 1 - n) * BLOCK_W, BLOCK_W), :] = tile.astype(w_ref.dtype)


def _toeplitz_tiles(kt, t_max):
    g, c, kw = kt.shape
    gb = g // GROUPS_PER_BLOCK
    rows = t_max * BLOCK_W
    return pl.pallas_call(
        functools.partial(_toeplitz_kernel, t_max=t_max),
        grid=(gb,),
        in_specs=[pl.BlockSpec((GROUPS_PER_BLOCK, c, kw), lambda b: (b, 0, 0))],
        out_specs=pl.BlockSpec((None, rows, BLOCK_W), lambda b: (b, 0, 0)),
        out_shape=jax.ShapeDtypeStruct((gb, rows, BLOCK_W), BF16),
        compiler_params=_params("arbitrary"),
        name="ssm_toeplitz",
    )(kt)


def _group_mask(rows, cols, row_div, col_div):
    return (lax.broadcasted_iota(jnp.int32, (rows, cols), 0) // row_div
            == lax.broadcasted_iota(jnp.int32, (rows, cols), 1) // col_div)


def _ssm_kernel(u_ref, w_ref, pd_ref, qre_ref, qim_ref, s0re_ref, s0im_ref, lre_ref, lim_ref,
                y_ref, fre_ref, fim_ref, ucat_ref, v_ref, s_ref, *, t_chunk, nseq, nc):
    c = SSM_GROUP
    rb = nseq * nc
    t_max = w_ref.shape[0] // BLOCK_W
    for t in range(t_chunk):
        for s in range(BLOCK_W // LANES):
            lo = t * BLOCK_W + s * LANES
            ucat_ref[:, lo:lo + LANES] = u_ref[s, pl.ds(t, rb, stride=t_chunk), :].astype(BF16)
    mask = _group_mask(BLOCK_W, STATE_W, c, SSM_STATE)
    reps = STATE_W // LANES
    acc = jnp.zeros((rb, 2 * STATE_W), F32)
    for t in range(t_chunk):
        blk = pd_ref[t_chunk - 1 - t]
        w_re = jnp.where(mask, jnp.concatenate([blk[:, :LANES]] * reps, axis=1), 0.0)
        w_im = jnp.where(mask, jnp.concatenate([blk[:, LANES:]] * reps, axis=1), 0.0)
        w = jnp.concatenate([w_re, w_im], axis=1).astype(BF16)
        acc = acc + _dot(ucat_ref[:, t * BLOCK_W:(t + 1) * BLOCK_W], w)
    v_ref[0] = acc[:, :STATE_W]
    v_ref[1] = acc[:, STATE_W:]
    ar = lre_ref[...]
    ai = lim_ref[...]
    if nc == 1:
        sr = s0re_ref[...]
        si = s0im_ref[...]
        s_ref[0] = sr
        s_ref[1] = si
        fre_ref[...] = ar * sr - ai * si + v_ref[0]
        fim_ref[...] = ar * si + ai * sr + v_ref[1]
    else:
        def body(k, carry):
            new = []
            for b in range(nseq):
                sr, si = carry[2 * b], carry[2 * b + 1]
                row = pl.ds(b * nc + k, 1)
                s_ref[0, row, :] = sr
                s_ref[1, row, :] = si
                new.append(ar * sr - ai * si + v_ref[0, row, :])
                new.append(ar * si + ai * sr + v_ref[1, row, :])
            return tuple(new)

        init = []
        for b in range(nseq):
            init += [s0re_ref[b:b + 1, :], s0im_ref[b:b + 1, :]]
        fin = lax.fori_loop(0, nc, body, tuple(init))
        for b in range(nseq):
            fre_ref[b:b + 1, :] = fin[2 * b]
            fim_ref[b:b + 1, :] = fin[2 * b + 1]
    s_re = s_ref[0].astype(BF16)
    s_im = s_ref[1].astype(BF16)
    for t in range(t_chunk):
        rows = slice((t + 1) * c, (t + 2) * c)
        q_re = jnp.where(mask, jnp.concatenate([qre_ref[rows, :]] * GROUPS_PER_BLOCK, axis=0), 0.0).astype(BF16)
        q_im = jnp.where(mask, jnp.concatenate([qim_ref[rows, :]] * GROUPS_PER_BLOCK, axis=0), 0.0).astype(BF16)
        y = (_dot(ucat_ref[:, :(t + 1) * BLOCK_W], w_ref[(t_max - 1 - t) * BLOCK_W:, :])
             + _dot_nt(s_re, q_re) + _dot_nt(s_im, q_im))
        for s in range(BLOCK_W // LANES):
            y_ref[s, pl.ds(t, rb, stride=t_chunk), :] = y[:, s * LANES:(s + 1) * LANES]


def _ssm(u_slabs, w_t, pd, q_re, q_im, s0_re, s0_im, lam_re, lam_im, t_chunk, nb, nc):
    gb = w_t.shape[0]
    c = SSM_GROUP
    nseq = max(1, min(nb, 256 // nc))
    nblk = nb // nseq
    rb = nseq * nc
    per = BLOCK_W // LANES
    gp = gb * STATE_W
    uspec = pl.BlockSpec((per, rb * t_chunk, LANES), lambda g, i: (g, i, 0))
    qspec = pl.BlockSpec(((t_chunk + 1) * c, STATE_W), lambda g, i: (0, g))
    sspec = pl.BlockSpec((None, nseq, STATE_W), lambda g, i: (i, 0, g))
    lspec = pl.BlockSpec((1, STATE_W), lambda g, i: (0, g))
    fshape = jax.ShapeDtypeStruct((nblk, nseq, gp), F32)
    y, f_re, f_im = pl.pallas_call(
        functools.partial(_ssm_kernel, t_chunk=t_chunk, nseq=nseq, nc=nc),
        grid=(gb, nblk),
        in_specs=[uspec,
                  pl.BlockSpec((None,) + w_t.shape[1:], lambda g, i: (g, 0, 0)),
                  pl.BlockSpec((t_chunk, BLOCK_W, 2 * LANES), lambda g, i: (0, g, 0)),
                  qspec, qspec, sspec, sspec, lspec, lspec],
        out_specs=[uspec, sspec, sspec],
        out_shape=[jax.ShapeDtypeStruct(u_slabs.shape, F32), fshape, fshape],
        scratch_shapes=[pltpu.VMEM((rb, t_chunk * BLOCK_W), BF16), pltpu.VMEM((2, rb, STATE_W), F32),
                        pltpu.VMEM((2, rb, STATE_W), F32)],
        compiler_params=_params("arbitrary", "arbitrary"),
        name="ssm_chunked",
    )(u_slabs, w_t, pd, q_re, q_im, s0_re.reshape(nblk, nseq, gp), s0_im.reshape(nblk, nseq, gp), lam_re, lam_im)
    return y, f_re.reshape(nb, gp), f_im.reshape(nb, gp)


def _lambda(lq1_ref, lk1_ref, lq2_ref, lk2_ref, lambda_init):
    return (jnp.exp(jnp.sum(lq1_ref[...] * lk1_ref[...], axis=-1, keepdims=True))
            - jnp.exp(jnp.sum(lq2_ref[...] * lk2_ref[...], axis=-1, keepdims=True)) + lambda_init)


def _bias_columns(n, off, pieces, key_side):
    hw = 2 * HEAD_DIM
    lane = lax.broadcasted_iota(jnp.int32, (1, hw), 1) - off
    valid = (lane >= 0) & (lane < 4 * len(pieces))
    a = lane // 4
    kap = lane % 4
    s_lane = pieces[-1]
    for i in range(len(pieces) - 2, -1, -1):
        s_lane = jnp.where(a == i, pieces[i], s_lane)
    pos = lax.broadcasted_iota(jnp.int32, (n, 1), 0)
    hi = (pos // 64 * 64).astype(F32)
    lo = (pos % 64).astype(F32)
    if key_side:
        val = jnp.where(kap < 2, s_lane, jnp.where(kap == 2, hi, lo))
    else:
        val = jnp.where(kap == 0, -hi, jnp.where(kap == 1, -lo, s_lane))
    return jnp.where(valid, val, 0.0)


def _attn_kernel(slope_ref, q_ref, k_ref, v_ref, lq1_ref, lk1_ref, lq2_ref, lk2_ref, g_ref, o_ref,
                 rel_ref, ka_ref, qb_ref, m_ref, l_ref, acc_ref, *, tq, tk, nq, lambda_init):
    h = pl.program_id(1)
    qi = pl.program_id(2)
    hd = HEAD_DIM
    hw = 2 * hd
    slope2 = slope_ref[h] * LOG2E
    first = lax.broadcasted_iota(jnp.int32, (1, hw), 1) < hd

    @pl.when(qi == 0)
    def _():
        sv = jnp.full((1, hw), slope2, F32)
        s1 = sv.astype(BF16).astype(F32)
        s2 = (sv - s1).astype(BF16).astype(F32)
        s3 = (sv - s1 - s2).astype(BF16).astype(F32)
        pieces = (s1, s2, s3)
        qb_ref[0] = _bias_columns(tq, hd, pieces, False)
        qb_ref[1] = _bias_columns(tq, 0, pieces, False)
        ka_ref[0] = _bias_columns(tk, hd, pieces, True)
        ka_ref[1] = _bias_columns(tk, 0, pieces, True)
        rel_ref[...] = (lax.broadcasted_iota(jnp.int32, (tk, tq), 1)
                        - lax.broadcasted_iota(jnp.int32, (tk, tq), 0)).astype(F32)

    q = q_ref[...] * (hd ** -0.5 * LOG2E)
    qm = (jnp.where(first, q, qb_ref[0]).astype(BF16), jnp.where(first, qb_ref[1], q).astype(BF16))
    m_ref[...] = jnp.full_like(m_ref, NEG_BIG)
    l_ref[...] = jnp.zeros_like(l_ref)
    acc_ref[...] = jnp.zeros_like(acc_ref)

    def scores(k0, nk, q0, nq_, masked):
        kp = k0 % tk
        k = k_ref[pl.ds(k0, nk), :]
        km = (jnp.where(first, k, ka_ref[0, kp:kp + nk]).astype(BF16),
              jnp.where(first, ka_ref[1, kp:kp + nk], k).astype(BF16))
        ts = [_dot_nt(km[mp], qm[mp][q0:q0 + nq_]) for mp in range(2)]
        if masked:
            keep = rel_ref[:nk, :nq_] >= 0.0
            ts = [jnp.where(keep, t, -jnp.inf) for t in ts]
        return ts

    def update(k0, nk, q0, nq_, ts, c):
        v = v_ref[pl.ds(k0, nk), :].astype(BF16)
        ones = jnp.ones((16, nk), BF16)
        lanes = slice(q0, q0 + nq_)
        m_old = [m_ref[mp, :, lanes] for mp in range(2)]
        m_new = [jnp.maximum(m_old[mp], jnp.max(ts[mp], axis=0, keepdims=True) + c) for mp in range(2)]
        ps = [jnp.exp2(ts[mp] - (m_new[mp] - c)).astype(BF16) for mp in range(2)]
        for mp in range(2):
            alpha = jnp.exp2(m_old[mp] - m_new[mp])
            l_ref[mp, :, lanes] = alpha * l_ref[mp, :, lanes] + _dot(ones, ps[mp])[0:1]
            acc_ref[mp, :, lanes] = alpha * acc_ref[mp, :, lanes] + _dot_tn(v, ps[mp])
            m_ref[mp, :, lanes] = m_new[mp]

    half = tk // 2
    for q_blk in range(nq):
        @pl.when(qi == q_blk)
        def _(q_blk=q_blk):
            pieces_ = [(kj * tk, tk, 0, tq, False, -slope2 * float((q_blk - kj) * tq)) for kj in range(q_blk)]
            pieces_ += [(q_blk * tk, half, 0, tq, True, 0.0), (q_blk * tk + half, half, half, tq - half, True, 0.0)]
            nxt = scores(*pieces_[0][:5])
            for n, pc in enumerate(pieces_):
                cur = nxt
                if n + 1 < len(pieces_):
                    nxt = scores(*pieces_[n + 1][:5])
                update(*pc[:4], cur, pc[5])

    lam = _lambda(lq1_ref, lk1_ref, lq2_ref, lk2_ref, lambda_init)
    o = acc_ref[0] / l_ref[0] - lam * (acc_ref[1] / l_ref[1])
    o = o * lax.rsqrt(jnp.mean(o * o, axis=0, keepdims=True) + LN_EPS) * g_ref[...] * (1.0 - lambda_init)
    o_ref[...] = o.T.astype(o_ref.dtype)


def _attention(q, k, v, slopes, lam_vecs, subln_g, nbatch, seq, lambda_init, tq, tk):
    n, da = q.shape
    nh = da // (2 * HEAD_DIM)
    nq = seq // tq
    hw = 2 * HEAD_DIM
    vec = pl.BlockSpec((1, HEAD_DIM), lambda b, h, i: (0, 0))
    return pl.pallas_call(
        functools.partial(_attn_kernel, tq=tq, tk=tk, nq=nq, lambda_init=lambda_init),
        grid=(nbatch, nh, nq),
        in_specs=[pl.BlockSpec(memory_space=pltpu.SMEM),
                  pl.BlockSpec((tq, hw), lambda b, h, i: (b * nq + i, h)),
                  pl.BlockSpec((seq, hw), lambda b, h, i: (b, h)),
                  pl.BlockSpec((seq, hw), lambda b, h, i: (b, h)),
                  vec, vec, vec, vec,
                  pl.BlockSpec((hw, 1), lambda b, h, i: (0, 0))],
        out_specs=pl.BlockSpec((tq, hw), lambda b, h, i: (b * nq + i, h)),
        out_shape=jax.ShapeDtypeStruct((n, da), BF16),
        scratch_shapes=[pltpu.VMEM((tk, tq), F32), pltpu.VMEM((2, tk, hw), F32), pltpu.VMEM((2, tq, hw), F32),
                        pltpu.VMEM((2, 1, tq), F32),
                        pltpu.VMEM((2, 1, tq), F32), pltpu.VMEM((2, hw, tq), F32)],
        compiler_params=_params("arbitrary", "arbitrary", "arbitrary"),
        name="prompt_attention",
    )(slopes, q, k, v, *lam_vecs, subln_g.reshape(hw, 1))


def _dec_attn_kernel(pt_ref, q_ref, kn_ref, vn_ref, *rest, n_pages, page, tdec, nh, lambda_init):
    del pt_ref
    k_refs = rest[:n_pages] + (kn_ref,)
    v_refs = rest[n_pages:2 * n_pages] + (vn_ref,)
    lq1_ref, lk1_ref, lq2_ref, lk2_ref, g_ref, o_ref = rest[2 * n_pages:]
    hd = HEAD_DIM
    hw = 2 * hd
    ncol = 2 * nh * tdec
    past = n_pages * page
    q = q_ref[...] * (hd ** -0.5 * LOG2E)
    rid = lax.broadcasted_iota(jnp.int32, (ncol, hw), 0)
    cid = lax.broadcasted_iota(jnp.int32, (ncol, hw), 1)
    qt = jnp.zeros((ncol, hw), F32)
    for h in range(nh):
        for i in range(tdec):
            sel = (rid // (2 * tdec) == h) & (rid % tdec == i)
            qt = jnp.where(sel, jnp.broadcast_to(q[i:i + 1, h * hw:(h + 1) * hw], (ncol, hw)), qt)
    qt = jnp.where((rid // tdec) % 2 == cid // hd, qt, 0.0)
    zq = jnp.zeros_like(qt)
    qt2 = jnp.concatenate([jnp.concatenate([qt, zq], axis=1), jnp.concatenate([zq, qt], axis=1)],
                          axis=0).astype(BF16)
    col2 = lax.broadcasted_iota(jnp.int32, (1, 2 * ncol), 1)
    second = col2 // ncol
    col = col2 % ncol
    qidx = col % tdec
    hcol = col // (2 * tdec)
    slope = jnp.exp2((hcol + 1).astype(F32) * (-8.0 / nh)) * LOG2E

    def rows(n):
        r = lax.broadcasted_iota(jnp.int32, (n, 1), 0)
        return r // nh, r % nh

    tok, hrow = rows(page * nh)
    base = jnp.where(hrow == hcol, -slope * (past + qidx - tok - second * page).astype(F32), -jnp.inf)
    tokn, hrown = rows(tdec * nh)
    basen = jnp.where((hrown == hcol) & (qidx >= tokn) & (second == 0),
                      -slope * (qidx - tokn).astype(F32), -jnp.inf)
    blocks = [(j, j + 1, base, slope * float(j * page)) for j in range(0, n_pages - 1, 2)]
    if n_pages % 2:
        blocks.append((n_pages - 1, None, jnp.where(second == 0, base, -jnp.inf), slope * float((n_pages - 1) * page)))
    blocks.append((n_pages, None, basen, jnp.zeros_like(slope)))

    def pair(refs, a, b):
        xa = refs[a][...]
        xb = refs[b][...] if b is not None else jnp.zeros_like(xa)
        return jnp.concatenate([xa, xb], axis=1).astype(BF16)

    t_list = []
    m2 = jnp.full((1, 2 * ncol), NEG_BIG, F32)
    for a, b, bias, c in blocks:
        t = _dot_nt(pair(k_refs, a, b), qt2) + bias
        m2 = jnp.maximum(m2, jnp.max(t, axis=0, keepdims=True) + c)
        t_list.append(t)
    m = jnp.maximum(m2[:, :ncol], m2[:, ncol:])
    m2 = jnp.concatenate([m, m], axis=1)
    l2 = jnp.zeros((1, 2 * ncol), F32)
    acc2 = jnp.zeros((2 * ncol, 2 * hw), F32)
    for (a, b, bias, c), t in zip(blocks, t_list):
        p = jnp.exp2(t - (m2 - c))
        l2 = l2 + jnp.sum(p, axis=0, keepdims=True)
        p = p.astype(BF16)
        v = pair(v_refs, a, b)
        if p.shape[0] < LANES:
            pad = LANES - p.shape[0]
            p = jnp.concatenate([p, jnp.zeros((pad, 2 * ncol), BF16)], axis=0)
            v = jnp.concatenate([v, jnp.zeros((pad, 2 * hw), BF16)], axis=0)
        acc2 = acc2 + _dot_tn(p, v)
    l = l2[:, :ncol] + l2[:, ncol:]
    acc = acc2[:ncol, :hw] + acc2[ncol:, hw:]
    eye = lax.broadcasted_iota(jnp.int32, (ncol, ncol), 0) == lax.broadcasted_iota(jnp.int32, (ncol, ncol), 1)
    lcol = jnp.sum(jnp.where(eye, jnp.broadcast_to(l, (ncol, ncol)), 0.0), axis=1, keepdims=True)
    acc = acc / lcol
    lam = _lambda(lq1_ref, lk1_ref, lq2_ref, lk2_ref, lambda_init)
    g = g_ref[...]
    for h in range(nh):
        blk = acc[h * 2 * tdec:(h + 1) * 2 * tdec]
        o = blk[:tdec] - lam * blk[tdec:]
        o = o * lax.rsqrt(jnp.mean(o * o, axis=-1, keepdims=True) + LN_EPS) * g * (1.0 - lambda_init)
        o_ref[:, h * hw:(h + 1) * hw] = o.astype(o_ref.dtype)


def _decode_attention(q3, k3, v3, cache_k, cache_v, page_table, lam_vecs, subln_g, lambda_init):
    nb, tdec, da = q3.shape
    n_pages = page_table.shape[1]
    rows, hw = cache_k.shape[1:]
    nh = da // hw
    page = rows // nh
    new = pl.BlockSpec((None, tdec, da), lambda b, pt: (b, 0, 0))
    newp = pl.BlockSpec((None, tdec * nh, hw), lambda b, pt: (b, 0, 0))
    pages = [pl.BlockSpec((None, rows, hw), lambda b, pt, j=j: (pt[b, j], 0, 0)) for j in range(n_pages)]
    vec = pl.BlockSpec((1, HEAD_DIM), lambda b, pt: (0, 0))
    return pl.pallas_call(
        functools.partial(_dec_attn_kernel, n_pages=n_pages, page=page, tdec=tdec, nh=nh, lambda_init=lambda_init),
        grid_spec=pltpu.PrefetchScalarGridSpec(
            num_scalar_prefetch=1,
            grid=(nb,),
            in_specs=[new, newp, newp] + pages + pages + [vec, vec, vec, vec,
                                                          pl.BlockSpec((1, hw), lambda b, pt: (0, 0))],
            out_specs=new),
        out_shape=jax.ShapeDtypeStruct((nb, tdec, da), BF16),
        compiler_params=_params("arbitrary"),
        name="decode_attention",
    )(page_table, q3, k3.reshape(nb, tdec * nh, hw), v3.reshape(nb, tdec * nh, hw),
      *([cache_k] * n_pages), *([cache_v] * n_pages), *lam_vecs, subln_g)


def _mix_in_kernel(y_ref, o_ref, gs_ref, ga_ref, wglu_ref, wus_ref, wua_ref, out_ref):
    z = jax.nn.gelu(jnp.concatenate([y_ref[i] for i in range(y_ref.shape[0])], axis=1))
    zz = z * jax.nn.sigmoid(_dot(z.astype(BF16), wglu_ref[...]))
    y_ssm = _dot(zz.astype(BF16), wus_ref[...])
    y_att = _dot(o_ref[...], wua_ref[...])
    out_ref[...] = (jax.nn.sigmoid(gs_ref[...]) * y_ssm + jax.nn.sigmoid(ga_ref[...]) * y_att).astype(out_ref.dtype)


def _mix_in(y, o, gates, w_glu, w_up_ssm, w_up_att, tm):
    nslab, n, _ = y.shape
    da = o.shape[1]
    d = w_up_ssm.shape[1]
    full = lambda a: pl.BlockSpec(a.shape, lambda i: (0, 0))
    return pl.pallas_call(
        _mix_in_kernel,
        grid=(n // tm,),
        in_specs=[pl.BlockSpec((nslab, tm, LANES), lambda i: (0, i, 0)),
                  pl.BlockSpec((tm, da), lambda i: (i, 0)),
                  pl.BlockSpec((tm, d), lambda i: (i, 0)),
                  pl.BlockSpec((tm, d), lambda i: (i, 1)),
                  full(w_glu), full(w_up_ssm), full(w_up_att)],
        out_specs=pl.BlockSpec((tm, d), lambda i: (i, 0)),
        out_shape=jax.ShapeDtypeStruct((n, d), BF16),
        compiler_params=_params("arbitrary"),
        name="branch_mix",
    )(y, o, gates, gates, w_glu, w_up_ssm, w_up_att)


def _out_proj_kernel(mix_ref, x_ref, g1_ref, wo_ref, lg_ref, lb_ref, out_ref, *, alpha):
    r = alpha * x_ref[...] + _mod_val(g1_ref) * _dot(mix_ref[...], wo_ref[...])
    out_ref[...] = _ln(r) * lg_ref[...] + lb_ref[...]


def _out_proj(mix, x, mod3, rows_per_batch, w_o, ln_g, ln_b, alpha, tm):
    n, d = x.shape
    row = pl.BlockSpec((tm, d), lambda i: (i, 0))
    vec = pl.BlockSpec((1, d), lambda i: (0, 0))
    return pl.pallas_call(
        functools.partial(_out_proj_kernel, alpha=alpha),
        grid=(n // tm,),
        in_specs=[row, row, _mod_spec(mod3, tm, rows_per_batch, 2, 1),
                  pl.BlockSpec(w_o.shape, lambda i: (0, 0)), vec, vec],
        out_specs=row,
        out_shape=jax.ShapeDtypeStruct((n, d), F32),
        compiler_params=_params("arbitrary"),
        name="out_proj_ln1",
    )(mix, x, mod3, w_o, ln_g, ln_b)


def _ffn_kernel(x_ref, sh_ref, sc_ref, g2_ref, w1_ref, w2_ref, lg_ref, lb_ref, out_ref, *rest, alpha, nf):
    f = pl.program_id(1)
    h_ref, acc_ref = rest[-2:]
    if len(rest) == 4:
        w1b_ref, w2b_ref = rest[:2]
        w1b_ref[...] = w1_ref[...].astype(w1b_ref.dtype)
        w2b_ref[...] = w2_ref[...].astype(w2b_ref.dtype)
    else:
        w1b_ref, w2b_ref = w1_ref, w2_ref

    @pl.when(f == 0)
    def _():
        h_ref[...] = (_ln(x_ref[...]) * (1.0 + _mod_val(sc_ref)) + _mod_val(sh_ref)).astype(h_ref.dtype)
        acc_ref[...] = jnp.zeros_like(acc_ref)

    a = jnp.maximum(_dot(h_ref[...], w1b_ref[...]), 0.0)
    acc_ref[...] += _dot((a * a).astype(BF16), w2b_ref[...])

    @pl.when(f == nf - 1)
    def _():
        r = alpha * x_ref[...] + _mod_val(g2_ref) * acc_ref[...]
        out_ref[...] = _ln(r) * lg_ref[...] + lb_ref[...]


def _ffn(x, mod3, rows_per_batch, w1, w2, ln_g, ln_b, alpha, tm, tf):
    n, d = x.shape
    dff = w1.shape[1]
    nf = dff // tf
    emit = w1.dtype != BF16
    assert not emit or n == tm
    once = dict(pipeline_mode=pl.Buffered(1)) if emit else {}
    row = pl.BlockSpec((tm, d), lambda i, f: (i, 0), **once)
    vec = pl.BlockSpec((1, d), lambda i, f: (0, 0))
    w1spec = pl.BlockSpec((d, tf), lambda i, f: (0, f))
    w2spec = pl.BlockSpec((tf, d), lambda i, f: (f, 0))
    res = pl.pallas_call(
        functools.partial(_ffn_kernel, alpha=alpha, nf=nf),
        grid=(n // tm, nf),
        in_specs=[row,
                  _mod_spec(mod3, tm, rows_per_batch, 3, 2),
                  _mod_spec(mod3, tm, rows_per_batch, 4, 2),
                  _mod_spec(mod3, tm, rows_per_batch, 5, 2),
                  w1spec, w2spec, vec, vec],
        out_specs=[row] + ([w1spec, w2spec] if emit else []),
        out_shape=[jax.ShapeDtypeStruct((n, d), F32)]
        + ([jax.ShapeDtypeStruct(w1.shape, BF16), jax.ShapeDtypeStruct(w2.shape, BF16)] if emit else []),
        scratch_shapes=[pltpu.VMEM((tm, d), BF16), pltpu.VMEM((tm, d), F32)],
        compiler_params=_params("arbitrary", "arbitrary"),
        name="ffn_ln2",
    )(x, mod3, mod3, mod3, w1, w2, ln_g, ln_b)
    return res


def _trunk_layer(x3, mod3, s0, cache, page_table, wts, tabs, lambda_init, alpha):
    nbatch, seq, d = x3.shape
    n = nbatch * seq
    x = x3.reshape(n, d)
    ds = wts["w_glu"].shape[0]
    da = wts["w_up_att"].shape[0]
    nh = da // (2 * HEAD_DIM)
    tm = min(512, seq) if mod3.shape[1] == 1 else min(512, n)
    assert ds == da
    tm_in = min(1024, seq) if wts["w_in"].dtype == BF16 and mod3.shape[1] == 1 else tm
    u, q, k, v, gates, *w_in_b = _in_proj(x, mod3, seq, wts["w_in"], ds, tm_in)

    pd, q_re, q_im, w_t, lp_re, lp_im = tabs
    t_chunk = PROMPT_CHUNK if seq % PROMPT_CHUNK == 0 else seq
    gp = lp_re.shape[1] * SSM_STATE
    lam_re = lp_re[t_chunk, :, :SSM_STATE].reshape(1, gp)
    lam_im = lp_im[t_chunk, :, :SSM_STATE].reshape(1, gp)
    y, f_re, f_im = _ssm(u, w_t, pd, q_re, q_im, s0[0], s0[1], lam_re, lam_im, t_chunk, nbatch, seq // t_chunk)

    lam_vecs = wts["lam_vecs"]
    if cache is None:
        slopes = jnp.asarray([2.0 ** (-8.0 * (i + 1) / nh) for i in range(nh)], F32)
        tq = min(512, seq)
        o = _attention(q, k, v, slopes, lam_vecs, wts["subln_g"], nbatch, seq, lambda_init, tq, tq)
    else:
        o = _decode_attention(q.reshape(nbatch, seq, da), k.reshape(nbatch, seq, da), v.reshape(nbatch, seq, da),
                              cache[0], cache[1], page_table, lam_vecs, wts["subln_g"], lambda_init)
        o = o.reshape(n, da)

    mix = _mix_in(y, o, gates, wts["w_glu"], wts["w_up_ssm"], wts["w_up_att"], tm)
    x1 = _out_proj(mix, x, mod3, seq, wts["w_o"], wts["ln1_g"], wts["ln1_b"], alpha, tm)
    tf = 1024 if wts["w_ff1"].dtype == BF16 else 512
    x2, *w_ff_b = _ffn(x1, mod3, seq, wts["w_ff1"], wts["w_ff2"], wts["ln2_g"], wts["ln2_b"], alpha, tm,
                       min(tf, wts["w_ff1"].shape[1]))
    return (x2.reshape(nbatch, seq, d), k.reshape(nbatch, seq, nh, 2 * HEAD_DIM),
            v.reshape(nbatch, seq, nh, 2 * HEAD_DIM), f_re, f_im, tuple(w_in_b) + tuple(w_ff_b))


def kernel(x_prompt, x_sample, c_prompt, c_sample, cache_k, cache_v, state_ssm_re, state_ssm_im, page_table,
           w_ada, b_ada, w_in, ssm_a_re, ssm_a_im, ssm_log_dt, ssm_b_re, ssm_b_im, ssm_c_re, ssm_c_im, ssm_d,
           w_glu, w_up_ssm, lam_q1, lam_k1, lam_q2, lam_k2, subln_g, w_up_att, w_o, ln1_g, ln1_b, w_ff1, w_ff2,
           ln2_g, ln2_b):
    depth = w_in.shape[0]
    bp, seq_p, d = x_prompt.shape
    bs, seq_s, _ = x_sample.shape
    g, p = ssm_a_re.shape[1:]
    alpha = (2 * depth) ** 0.25
    npow = PROMPT_CHUNK + 1
    xp, xs = x_prompt, x_sample
    outs_p, outs_s = [], []
    c_rows = jnp.concatenate([c_sample, c_prompt], axis=0)
    c_rows = jnp.pad(c_rows, ((0, (-(bp + bs)) % 8), (0, 0)))
    n_pool, page, nh, hw = cache_k.shape[1:]
    for l in range(depth):
        lambda_init = 0.8 - 0.6 * math.exp(-0.3 * l)
        mod, mod_s = _ada(c_rows, w_ada[l], b_ada[l].reshape(1, 6 * d), min(1024, d), bs, seq_s)
        mod_p = mod[bs:bs + bp].reshape(bp, 1, 6 * d)
        wts = {
            "w_in": w_in[l], "w_ff1": w_ff1[l], "w_ff2": w_ff2[l],
            "w_glu": w_glu[l].astype(BF16), "w_up_ssm": w_up_ssm[l].astype(BF16),
            "w_up_att": w_up_att[l].astype(BF16), "w_o": w_o[l].astype(BF16),
            "ln1_g": ln1_g[l].reshape(1, d), "ln1_b": ln1_b[l].reshape(1, d),
            "ln2_g": ln2_g[l].reshape(1, d), "ln2_b": ln2_b[l].reshape(1, d),
            "subln_g": subln_g[l].reshape(1, 2 * HEAD_DIM),
            "lam_vecs": tuple(a[l].reshape(1, HEAD_DIM) for a in (lam_q1, lam_k1, lam_q2, lam_k2)),
        }
        tabs = _ssm_prep(ssm_a_re[l], ssm_a_im[l], ssm_log_dt[l], ssm_b_re[l], ssm_b_im[l], ssm_c_re[l],
                         ssm_c_im[l], ssm_d[l], npow)
        tabs = list(tabs)
        tabs[3] = _toeplitz_tiles(tabs[3], PROMPT_CHUNK)
        cache = (cache_k[l].reshape(n_pool, page * nh, hw), cache_v[l].reshape(n_pool, page * nh, hw))
        s0 = (state_ssm_re[l].reshape(bs, g * p), state_ssm_im[l].reshape(bs, g * p))
        res_s = _trunk_layer(xs, mod_s, s0, cache, page_table, wts, tabs, lambda_init, alpha)
        wts["w_in"], wts["w_ff1"], wts["w_ff2"] = res_s[5]
        zeros = jnp.zeros((bp, g * p), F32)
        res_p = _trunk_layer(xp, mod_p, (zeros, zeros), None, None, wts, tabs, lambda_init, alpha)
        xp, xs = res_p[0], res_s[0]
        outs_p.append(res_p[1:5])
        outs_s.append(res_s[1:5])

    def stack(outs, i, shape=None):
        a = jnp.stack([o[i] for o in outs])
        return a if shape is None else a.reshape((depth,) + shape)

    return (xp, xs,
            stack(outs_p, 0), stack(outs_p, 1), stack(outs_p, 2, (bp, g, p)), stack(outs_p, 3, (bp, g, p)),
            stack(outs_s, 0), stack(outs_s, 1), stack(outs_s, 2, (bs, g, p)), stack(outs_s, 3, (bs, g, p)))
```

```python
import functools
import math

import jax
import jax.numpy as jnp
from jax import lax
from jax.experimental import pallas as pl
from jax.experimental.pallas import tpu as pltpu

F32 = jnp.float32
BF16 = jnp.bfloat16

SSM_GROUP = 16
SSM_STATE = 64
HEAD_DIM = 64
LANES = 128
BLOCK_W = 256
GROUPS_PER_BLOCK = BLOCK_W // SSM_GROUP
STATE_W = GROUPS_PER_BLOCK * SSM_STATE
PROMPT_CHUNK = 16
LN_EPS = 1e-5
NEG_BIG = -1e30
LOG2E = 1.4426950408889634
VMEM_LIMIT = 56 * 1024 * 1024


def _params(*sem):
    return pltpu.CompilerParams(dimension_semantics=sem, vmem_limit_bytes=VMEM_LIMIT)


def _ln(x):
    mu = jnp.mean(x, axis=-1, keepdims=True)
    xc = x - mu
    var = jnp.mean(xc * xc, axis=-1, keepdims=True)
    return xc * lax.rsqrt(var + LN_EPS)


def _dot(a, b):
    return jnp.dot(a, b, preferred_element_type=F32)


def _dot_nt(a, b):
    return lax.dot_general(a, b, (((1,), (1,)), ((), ())), preferred_element_type=F32)


def _dot_tn(a, b):
    return lax.dot_general(a, b, (((0,), (0,)), ((), ())), preferred_element_type=F32)


def _mod_spec(mod3, tm, rows_per_batch, col, ngrid):
    if mod3.shape[1] == 1:
        d = mod3.shape[2] // 6
        per = rows_per_batch // tm
        if ngrid == 1:
            return pl.BlockSpec((None, 1, d), lambda i: (i // per, 0, col))
        return pl.BlockSpec((None, 1, d), lambda i, j: (i // per, 0, col))
    nslab = mod3.shape[0] // 6
    once = pl.Buffered(1)
    if ngrid == 1:
        return pl.BlockSpec((nslab, tm, LANES), lambda i: (col, i, 0), pipeline_mode=once)
    return pl.BlockSpec((nslab, tm, LANES), lambda i, j: (col, i, 0), pipeline_mode=once)


def _mod_val(ref):
    if len(ref.shape) == 2:
        return ref[...]
    return jnp.concatenate([ref[s] for s in range(ref.shape[0])], axis=1)


def _ada_kernel(c_ref, w_ref, b_ref, o_ref, os_ref, *, nrep, seq):
    c = c_ref[...]
    a = (c * jax.nn.sigmoid(c)).astype(BF16)
    res = _dot(a, w_ref[...].astype(BF16)) + b_ref[...]
    o_ref[...] = res
    for s in range(os_ref.shape[0]):
        for t in range(seq):
            os_ref[s, pl.ds(t, nrep, stride=seq), :] = res[:nrep, s * LANES:(s + 1) * LANES]


def _ada(c, w, b, tn, nrep, seq):
    m, d = c.shape
    n = w.shape[1]
    return pl.pallas_call(
        functools.partial(_ada_kernel, nrep=nrep, seq=seq),
        grid=(n // tn,),
        in_specs=[pl.BlockSpec((m, d), lambda j: (0, 0)),
                  pl.BlockSpec((d, tn), lambda j: (0, j)),
                  pl.BlockSpec((1, tn), lambda j: (0, j))],
        out_specs=[pl.BlockSpec((m, tn), lambda j: (0, j)),
                   pl.BlockSpec((tn // LANES, nrep * seq, LANES), lambda j: (j, 0, 0))],
        out_shape=[jax.ShapeDtypeStruct((m, n), F32),
                   jax.ShapeDtypeStruct((n // LANES, nrep * seq, LANES), F32)],
        compiler_params=_params("arbitrary"),
        name="ada_mod",
    )(c, w, b)


def _in_proj_kernel(x_ref, sh_ref, sc_ref, w_ref, u_ref, q_ref, k_ref, v_ref, g_ref, *rest):
    j = pl.program_id(1)
    h_ref = rest[-1]
    if len(rest) == 2:
        wb_ref = rest[0]
        wb_ref[...] = w_ref[...].astype(wb_ref.dtype)
    else:
        wb_ref = w_ref

    @pl.when(j == 0)
    def _():
        h_ref[...] = (_ln(x_ref[...]) * (1.0 + _mod_val(sc_ref)) + _mod_val(sh_ref)).astype(h_ref.dtype)

    @pl.when(j == 0)
    def _():
        acc = _dot(h_ref[...], wb_ref[...])
        for s in range(u_ref.shape[0]):
            u_ref[s] = acc[:, s * LANES:(s + 1) * LANES]

    for idx, ref in ((1, q_ref), (2, k_ref), (3, v_ref)):
        @pl.when(j == idx)
        def _(ref=ref):
            ref[...] = _dot(h_ref[...], wb_ref[...])

    @pl.when(j >= 4)
    def _():
        g_ref[...] = _dot(h_ref[...], wb_ref[...])


WEIGHT_RING = 3


def _in_proj_ring_kernel(x_ref, sh_ref, sc_ref, w_hbm, u_ref, q_ref, k_ref, v_ref, g_ref, h_ref, wbuf, sem, *, nt):
    i = pl.program_id(0)
    j = pl.program_id(1)
    tn = wbuf.shape[2]
    total = pl.num_programs(0) * nt
    s = i * nt + j

    def copy(step):
        col = pl.multiple_of((step % nt) * tn, tn)
        slot = step % WEIGHT_RING
        return pltpu.make_async_copy(w_hbm.at[:, pl.ds(col, tn)], wbuf.at[slot], sem.at[slot])

    @pl.when(s == 0)
    def _():
        for first in range(WEIGHT_RING - 1):
            copy(first).start()

    @pl.when(s + WEIGHT_RING - 1 < total)
    def _():
        copy(s + WEIGHT_RING - 1).start()

    @pl.when(j == 0)
    def _():
        h_ref[...] = (_ln(x_ref[...]) * (1.0 + _mod_val(sc_ref)) + _mod_val(sh_ref)).astype(h_ref.dtype)

    copy(s).wait()
    slot = s % WEIGHT_RING

    @pl.when(j == 0)
    def _():
        acc = _dot(h_ref[...], wbuf[slot])
        for t in range(u_ref.shape[0]):
            u_ref[t] = acc[:, t * LANES:(t + 1) * LANES]

    for idx, ref in ((1, q_ref), (2, k_ref), (3, v_ref)):
        @pl.when(j == idx)
        def _(ref=ref):
            ref[...] = _dot(h_ref[...], wbuf[slot])

    @pl.when(j >= 4)
    def _():
        g_ref[...] = _dot(h_ref[...], wbuf[slot])


def _in_proj(x, mod3, rows_per_batch, w, width, tm):
    n, d = x.shape
    tn = width
    nt = w.shape[1] // tn
    emit = w.dtype != BF16
    assert not emit or n == tm
    once = dict(pipeline_mode=pl.Buffered(1)) if emit else {}
    row = pl.BlockSpec((tm, tn), lambda i, j: (i, 0), **once)
    wspec = pl.BlockSpec((d, tn), lambda i, j: (0, j))
    mat = jax.ShapeDtypeStruct((n, width), F32)
    ring = not emit and (n // tm) * nt >= WEIGHT_RING
    scratch = [pltpu.VMEM((tm, d), BF16)]
    if ring:
        scratch += [pltpu.VMEM((WEIGHT_RING, d, tn), BF16), pltpu.SemaphoreType.DMA((WEIGHT_RING,))]
    return pl.pallas_call(
        functools.partial(_in_proj_ring_kernel, nt=nt) if ring else _in_proj_kernel,
        grid=(n // tm, nt),
        in_specs=[pl.BlockSpec((tm, d), lambda i, j: (i, 0), **once),
                  _mod_spec(mod3, tm, rows_per_batch, 0, 2),
                  _mod_spec(mod3, tm, rows_per_batch, 1, 2),
                  pl.BlockSpec(memory_space=pl.ANY) if ring else wspec],
        out_specs=[pl.BlockSpec((tn // LANES, tm, LANES), lambda i, j: (0, i, 0), **once), row, row, row,
                   pl.BlockSpec((tm, tn), lambda i, j: (i, jnp.maximum(j - 4, 0)))] + ([wspec] if emit else []),
        out_shape=[jax.ShapeDtypeStruct((width // LANES, n, LANES), F32), mat, mat, mat,
                   jax.ShapeDtypeStruct((n, (nt - 4) * tn), F32)]
        + ([jax.ShapeDtypeStruct(w.shape, BF16)] if emit else []),
        scratch_shapes=scratch,
        compiler_params=_params("arbitrary", "arbitrary"),
        name="in_proj",
    )(x, mod3, mod3, w)


def _ssm_prep_kernel(are_ref, aim_ref, ldt_ref, bre_ref, bim_ref, cre_ref, cim_ref, d_ref,
                     pd_ref, qre_ref, qim_ref, kt_ref, lpre_ref, lpim_ref, *, npow, gt):
    c = SSM_GROUP
    p = SSM_STATE
    a_re = are_ref[...]
    a_im = aim_ref[...]
    dt = jnp.exp(ldt_ref[...])
    mag = jnp.exp(dt * a_re)
    ab_re = mag * jnp.cos(dt * a_im)
    ab_im = mag * jnp.sin(dt * a_im)
    den = a_re * a_re + a_im * a_im
    f_re = ((ab_re - 1.0) * a_re + ab_im * a_im) / den
    f_im = (ab_im * a_re - (ab_re - 1.0) * a_im) / den
    b_re = bre_ref[...]
    b_im = bim_ref[...]
    bb_re = f_re[:, None, :] * b_re - f_im[:, None, :] * b_im
    bb_im = f_re[:, None, :] * b_im + f_im[:, None, :] * b_re
    c_re = cre_ref[...]
    c_im = cim_ref[...]
    low = lax.broadcasted_iota(jnp.int32, (1, 1, 2 * p), 2) < p
    p_re = jnp.ones_like(a_re)
    p_im = jnp.zeros_like(a_re)
    g_re_all, g_im_all = [], []
    for n in range(npow):
        pr = p_re[:, None, :]
        pi = p_im[:, None, :]
        pw_re = pr * bb_re - pi * bb_im
        pw_im = pr * bb_im + pi * bb_re
        pd_ref[n] = jnp.concatenate([pw_re, pw_im], axis=-1).reshape(gt * c, 4 * p)
        g_re = c_re * pr - c_im * pi
        g_im = -(c_re * pi + c_im * pr)
        g_re_all.append(g_re)
        g_im_all.append(g_im)
        lpre_ref[n] = p_re
        lpim_ref[n] = p_im
        p_re, p_im = p_re * ab_re - p_im * ab_im, p_re * ab_im + p_im * ab_re
    g_re_all = jnp.concatenate(g_re_all, axis=1)
    g_im_all = jnp.concatenate(g_im_all, axis=1)
    for gg in range(0, gt, 2):
        sl = slice(gg // 2 * 2 * p, (gg // 2 + 1) * 2 * p)
        qre_ref[:, sl] = jnp.where(low[0], g_re_all[gg], g_re_all[gg + 1])
        qim_ref[:, sl] = jnp.where(low[0], g_im_all[gg], g_im_all[gg + 1])
    rows = kt_ref.shape[2]
    zpad = jnp.zeros((gt, rows - npow * c, 2 * p), F32)
    dn = (((2,), (2,)), ((0,), (0,)))
    kt = (lax.dot_general(jnp.where(low, bb_re, 0.0), jnp.concatenate([g_re_all, zpad], axis=1), dn,
                          precision=lax.Precision.HIGHEST, preferred_element_type=F32)
          + lax.dot_general(jnp.where(low, bb_im, 0.0), jnp.concatenate([g_im_all, zpad], axis=1), dn,
                            precision=lax.Precision.HIGHEST, preferred_element_type=F32))
    eye = (lax.broadcasted_iota(jnp.int32, (c, rows), 0) == lax.broadcasted_iota(jnp.int32, (c, rows), 1))
    kt_ref[...] = kt + jnp.where(eye[None], d_ref[...], 0.0)


def _ssm_prep(a_re, a_im, log_dt, b_re, b_im, c_re, c_im, d_skip, npow):
    g, p = a_re.shape
    c = SSM_GROUP
    gt = 8
    kt_w = -(-npow * c // LANES) * LANES
    dup = lambda a: jnp.concatenate([a, a], axis=-1)
    gp = pl.BlockSpec((gt, 2 * p), lambda i: (i, 0))
    gcp = pl.BlockSpec((gt, c, 2 * p), lambda i: (i, 0, 0))
    qspec = pl.BlockSpec((npow * c, gt * p), lambda i: (0, i))
    lspec = pl.BlockSpec((npow, gt, 2 * p), lambda i: (0, i, 0))
    return pl.pallas_call(
        functools.partial(_ssm_prep_kernel, npow=npow, gt=gt),
        grid=(g // gt,),
        in_specs=[gp, gp, pl.BlockSpec((gt, 1), lambda i: (i, 0)), gcp, gcp, gcp, gcp,
                  pl.BlockSpec((gt, c, 1), lambda i: (i, 0, 0))],
        out_specs=[pl.BlockSpec((npow, gt * c, 4 * p), lambda i: (0, i, 0)), qspec, qspec,
                   pl.BlockSpec((gt, c, kt_w), lambda i: (i, 0, 0)), lspec, lspec],
        out_shape=[jax.ShapeDtypeStruct((npow, g * c, 4 * p), F32),
                   jax.ShapeDtypeStruct((npow * c, g * p), F32),
                   jax.ShapeDtypeStruct((npow * c, g * p), F32),
                   jax.ShapeDtypeStruct((g, c, kt_w), F32),
                   jax.ShapeDtypeStruct((npow, g, 2 * p), F32),
                   jax.ShapeDtypeStruct((npow, g, 2 * p), F32)],
        compiler_params=_params("arbitrary"),
        name="ssm_prep",
    )(dup(a_re), dup(a_im), log_dt.reshape(g, 1), dup(jnp.swapaxes(b_re, 1, 2)), dup(jnp.swapaxes(b_im, 1, 2)),
      dup(c_re), dup(c_im), d_skip.reshape(g, c, 1))


def _toeplitz_kernel(kt_ref, w_ref, *, t_max):
    c = SSM_GROUP
    kw = kt_ref.shape[2]
    x = kt_ref[...].reshape(BLOCK_W, kw).astype(BF16)
    mask = _group_mask(BLOCK_W, BLOCK_W, c, c)
    sel_r = lax.broadcasted_iota(jnp.int32, (kw, BLOCK_W), 0)
    sel_c = lax.broadcasted_iota(jnp.int32, (kw, BLOCK_W), 1)
    for n in range(t_max):
        e = jnp.where((sel_r // c == n) & (sel_r % c == sel_c % c), 1.0, 0.0).astype(BF16)
        tile = jnp.where(mask, _dot(x, e), 0.0)
        w_ref[pl.ds((t_max - 1 - n) * BLOCK_W, BLOCK_W), :] = tile.astype(w_ref.dtype)


def _toeplitz_tiles(kt, t_max):
    g, c, kw = kt.shape
    gb = g // GROUPS_PER_BLOCK
    rows = t_max * BLOCK_W
    return pl.pallas_call(
        functools.partial(_toeplitz_kernel, t_max=t_max),
        grid=(gb,),
        in_specs=[pl.BlockSpec((GROUPS_PER_BLOCK, c, kw), lambda b: (b, 0, 0))],
        out_specs=pl.BlockSpec((None, rows, BLOCK_W), lambda b: (b, 0, 0)),
        out_shape=jax.ShapeDtypeStruct((gb, rows, BLOCK_W), BF16),
        compiler_params=_params("arbitrary"),
        name="ssm_toeplitz",
    )(kt)


def _group_mask(rows, cols, row_div, col_div):
    return (lax.broadcasted_iota(jnp.int32, (rows, cols), 0) // row_div
            == lax.broadcasted_iota(jnp.int32, (rows, cols), 1) // col_div)


def _ssm_kernel(u_ref, w_ref, pd_ref, qre_ref, qim_ref, s0re_ref, s0im_ref, lre_ref, lim_ref,
                y_ref, fre_ref, fim_ref, ucat_ref, v_ref, s_ref, *, t_chunk, nseq, nc):
    c = SSM_GROUP
    rb = nseq * nc
    t_max = w_ref.shape[0] // BLOCK_W
    for t in range(t_chunk):
        for s in range(BLOCK_W // LANES):
            lo = t * BLOCK_W + s * LANES
            ucat_ref[:, lo:lo + LANES] = u_ref[s, pl.ds(t, rb, stride=t_chunk), :].astype(BF16)
    mask = _group_mask(BLOCK_W, STATE_W, c, SSM_STATE)
    reps = STATE_W // LANES
    acc = jnp.zeros((rb, 2 * STATE_W), F32)
    for t in range(t_chunk):
        blk = pd_ref[t_chunk - 1 - t]
        w_re = jnp.where(mask, jnp.concatenate([blk[:, :LANES]] * reps, axis=1), 0.0)
        w_im = jnp.where(mask, jnp.concatenate([blk[:, LANES:]] * reps, axis=1), 0.0)
        w = jnp.concatenate([w_re, w_im], axis=1).astype(BF16)
        acc = acc + _dot(ucat_ref[:, t * BLOCK_W:(t + 1) * BLOCK_W], w)
    v_ref[0] = acc[:, :STATE_W]
    v_ref[1] = acc[:, STATE_W:]
    ar = lre_ref[...]
    ai = lim_ref[...]
    if nc == 1:
        sr = s0re_ref[...]
        si = s0im_ref[...]
        s_ref[0] = sr
        s_ref[1] = si
        fre_ref[...] = ar * sr - ai * si + v_ref[0]
        fim_ref[...] = ar * si + ai * sr + v_ref[1]
    else:
        def body(k, carry):
            new = []
            for b in range(nseq):
                sr, si = carry[2 * b], carry[2 * b + 1]
                row = pl.ds(b * nc + k, 1)
                s_ref[0, row, :] = sr
                s_ref[1, row, :] = si
                new.append(ar * sr - ai * si + v_ref[0, row, :])
                new.append(ar * si + ai * sr + v_ref[1, row, :])
            return tuple(new)

        init = []
        for b in range(nseq):
            init += [s0re_ref[b:b + 1, :], s0im_ref[b:b + 1, :]]
        fin = lax.fori_loop(0, nc, body, tuple(init))
        for b in range(nseq):
            fre_ref[b:b + 1, :] = fin[2 * b]
            fim_ref[b:b + 1, :] = fin[2 * b + 1]
    s_re = s_ref[0].astype(BF16)
    s_im = s_ref[1].astype(BF16)
    for t in range(t_chunk):
        rows = slice((t + 1) * c, (t + 2) * c)
        q_re = jnp.where(mask, jnp.concatenate([qre_ref[rows, :]] * GROUPS_PER_BLOCK, axis=0), 0.0).astype(BF16)
        q_im = jnp.where(mask, jnp.concatenate([qim_ref[rows, :]] * GROUPS_PER_BLOCK, axis=0), 0.0).astype(BF16)
        y = (_dot(ucat_ref[:, :(t + 1) * BLOCK_W], w_ref[(t_max - 1 - t) * BLOCK_W:, :])
             + _dot_nt(s_re, q_re) + _dot_nt(s_im, q_im))
        for s in range(BLOCK_W // LANES):
            y_ref[s, pl.ds(t, rb, stride=t_chunk), :] = y[:, s * LANES:(s + 1) * LANES]


def _ssm(u_slabs, w_t, pd, q_re, q_im, s0_re, s0_im, lam_re, lam_im, t_chunk, nb, nc):
    gb = w_t.shape[0]
    c = SSM_GROUP
    nseq = max(1, min(nb, 256 // nc))
    nblk = nb // nseq
    rb = nseq * nc
    per = BLOCK_W // LANES
    gp = gb * STATE_W
    uspec = pl.BlockSpec((per, rb * t_chunk, LANES), lambda g, i: (g, i, 0))
    qspec = pl.BlockSpec(((t_chunk + 1) * c, STATE_W), lambda g, i: (0, g))
    sspec = pl.BlockSpec((None, nseq, STATE_W), lambda g, i: (i, 0, g))
    lspec = pl.BlockSpec((1, STATE_W), lambda g, i: (0, g))
    fshape = jax.ShapeDtypeStruct((nblk, nseq, gp), F32)
    y, f_re, f_im = pl.pallas_call(
        functools.partial(_ssm_kernel, t_chunk=t_chunk, nseq=nseq, nc=nc),
        grid=(gb, nblk),
        in_specs=[uspec,
                  pl.BlockSpec((None,) + w_t.shape[1:], lambda g, i: (g, 0, 0)),
                  pl.BlockSpec((t_chunk, BLOCK_W, 2 * LANES), lambda g, i: (0, g, 0)),
                  qspec, qspec, sspec, sspec, lspec, lspec],
        out_specs=[uspec, sspec, sspec],
        out_shape=[jax.ShapeDtypeStruct(u_slabs.shape, F32), fshape, fshape],
        scratch_shapes=[pltpu.VMEM((rb, t_chunk * BLOCK_W), BF16), pltpu.VMEM((2, rb, STATE_W), F32),
                        pltpu.VMEM((2, rb, STATE_W), F32)],
        compiler_params=_params("arbitrary", "arbitrary"),
        name="ssm_chunked",
    )(u_slabs, w_t, pd, q_re, q_im, s0_re.reshape(nblk, nseq, gp), s0_im.reshape(nblk, nseq, gp), lam_re, lam_im)
    return y, f_re.reshape(nb, gp), f_im.reshape(nb, gp)


def _lambda(lq1_ref, lk1_ref, lq2_ref, lk2_ref, lambda_init):
    return (jnp.exp(jnp.sum(lq1_ref[...] * lk1_ref[...], axis=-1, keepdims=True))
            - jnp.exp(jnp.sum(lq2_ref[...] * lk2_ref[...], axis=-1, keepdims=True)) + lambda_init)


def _bias_columns(n, off, pieces, key_side):
    hw = 2 * HEAD_DIM
    lane = lax.broadcasted_iota(jnp.int32, (1, hw), 1) - off
    valid = (lane >= 0) & (lane < 4 * len(pieces))
    a = lane // 4
    kap = lane % 4
    s_lane = pieces[-1]
    for i in range(len(pieces) - 2, -1, -1):
        s_lane = jnp.where(a == i, pieces[i], s_lane)
    pos = lax.broadcasted_iota(jnp.int32, (n, 1), 0)
    hi = (pos // 64 * 64).astype(F32)
    lo = (pos % 64).astype(F32)
    if key_side:
        val = jnp.where(kap < 2, s_lane, jnp.where(kap == 2, hi, lo))
    else:
        val = jnp.where(kap == 0, -hi, jnp.where(kap == 1, -lo, s_lane))
    return jnp.where(valid, val, 0.0)


def _attn_kernel(slope_ref, q_ref, k_ref, v_ref, lq1_ref, lk1_ref, lq2_ref, lk2_ref, g_ref, o_ref,
                 rel_ref, ka_ref, qb_ref, m_ref, l_ref, acc_ref, *, tq, tk, nq, lambda_init):
    h = pl.program_id(1)
    qi = pl.program_id(2)
    hd = HEAD_DIM
    hw = 2 * hd
    slope2 = slope_ref[h] * LOG2E
    first = lax.broadcasted_iota(jnp.int32, (1, hw), 1) < hd

    @pl.when(qi == 0)
    def _():
        sv = jnp.full((1, hw), slope2, F32)
        s1 = sv.astype(BF16).astype(F32)
        s2 = (sv - s1).astype(BF16).astype(F32)
        s3 = (sv - s1 - s2).astype(BF16).astype(F32)
        pieces = (s1, s2, s3)
        qb_ref[0] = _bias_columns(tq, hd, pieces, False)
        qb_ref[1] = _bias_columns(tq, 0, pieces, False)
        ka_ref[0] = _bias_columns(tk, hd, pieces, True)
        ka_ref[1] = _bias_columns(tk, 0, pieces, True)
        rel_ref[...] = (lax.broadcasted_iota(jnp.int32, (tk, tq), 1)
                        - lax.broadcasted_iota(jnp.int32, (tk, tq), 0)).astype(F32)

    q = q_ref[...] * (hd ** -0.5 * LOG2E)
    qm = (jnp.where(first, q, qb_ref[0]).astype(BF16), jnp.where(first, qb_ref[1], q).astype(BF16))
    m_ref[...] = jnp.full_like(m_ref, NEG_BIG)
    l_ref[...] = jnp.zeros_like(l_ref)
    acc_ref[...] = jnp.zeros_like(acc_ref)

    def scores(k0, nk, q0, nq_, masked):
        kp = k0 % tk
        k = k_ref[pl.ds(k0, nk), :]
        km = (jnp.where(first, k, ka_ref[0, kp:kp + nk]).astype(BF16),
              jnp.where(first, ka_ref[1, kp:kp + nk], k).astype(BF16))
        ts = [_dot_nt(km[mp], qm[mp][q0:q0 + nq_]) for mp in range(2)]
        if masked:
            keep = rel_ref[:nk, :nq_] >= 0.0
            ts = [jnp.where(keep, t, -jnp.inf) for t in ts]
        return ts

    def update(k0, nk, q0, nq_, ts, c):
        v = v_ref[pl.ds(k0, nk), :].astype(BF16)
        ones = jnp.ones((16, nk), BF16)
        lanes = slice(q0, q0 + nq_)
        m_old = [m_ref[mp, :, lanes] for mp in range(2)]
        m_new = [jnp.maximum(m_old[mp], jnp.max(ts[mp], axis=0, keepdims=True) + c) for mp in range(2)]
        ps = [jnp.exp2(ts[mp] - (m_new[mp] - c)).astype(BF16) for mp in range(2)]
        for mp in range(2):
            alpha = jnp.exp2(m_old[mp] - m_new[mp])
            l_ref[mp, :, lanes] = alpha * l_ref[mp, :, lanes] + _dot(ones, ps[mp])[0:1]
            acc_ref[mp, :, lanes] = alpha * acc_ref[mp, :, lanes] + _dot_tn(v, ps[mp])
            m_ref[mp, :, lanes] = m_new[mp]

    half = tk // 2
    for q_blk in range(nq):
        @pl.when(qi == q_blk)
        def _(q_blk=q_blk):
            pieces_ = [(kj * tk, tk, 0, tq, False, -slope2 * float((q_blk - kj) * tq)) for kj in range(q_blk)]
            pieces_ += [(q_blk * tk, half, 0, tq, True, 0.0), (q_blk * tk + half, half, half, tq - half, True, 0.0)]
            nxt = scores(*pieces_[0][:5])
            for n, pc in enumerate(pieces_):
                cur = nxt
                if n + 1 < len(pieces_):
                    nxt = scores(*pieces_[n + 1][:5])
                update(*pc[:4], cur, pc[5])

    lam = _lambda(lq1_ref, lk1_ref, lq2_ref, lk2_ref, lambda_init)
    o = acc_ref[0] / l_ref[0] - lam * (acc_ref[1] / l_ref[1])
    o = o * lax.rsqrt(jnp.mean(o * o, axis=0, keepdims=True) + LN_EPS) * g_ref[...] * (1.0 - lambda_init)
    o_ref[...] = o.T.astype(o_ref.dtype)


def _attention(q, k, v, slopes, lam_vecs, subln_g, nbatch, seq, lambda_init, tq, tk):
    n, da = q.shape
    nh = da // (2 * HEAD_DIM)
    nq = seq // tq
    hw = 2 * HEAD_DIM
    vec = pl.BlockSpec((1, HEAD_DIM), lambda b, h, i: (0, 0))
    return pl.pallas_call(
        functools.partial(_attn_kernel, tq=tq, tk=tk, nq=nq, lambda_init=lambda_init),
        grid=(nbatch, nh, nq),
        in_specs=[pl.BlockSpec(memory_space=pltpu.SMEM),
                  pl.BlockSpec((tq, hw), lambda b, h, i: (b * nq + i, h)),
                  pl.BlockSpec((seq, hw), lambda b, h, i: (b, h)),
                  pl.BlockSpec((seq, hw), lambda b, h, i: (b, h)),
                  vec, vec, vec, vec,
                  pl.BlockSpec((hw, 1), lambda b, h, i: (0, 0))],
        out_specs=pl.BlockSpec((tq, hw), lambda b, h, i: (b * nq + i, h)),
        out_shape=jax.ShapeDtypeStruct((n, da), BF16),
        scratch_shapes=[pltpu.VMEM((tk, tq), F32), pltpu.VMEM((2, tk, hw), F32), pltpu.VMEM((2, tq, hw), F32),
                        pltpu.VMEM((2, 1, tq), F32),
                        pltpu.VMEM((2, 1, tq), F32), pltpu.VMEM((2, hw, tq), F32)],
        compiler_params=_params("arbitrary", "arbitrary", "arbitrary"),
        name="prompt_attention",
    )(slopes, q, k, v, *lam_vecs, subln_g.reshape(hw, 1))


def _dec_attn_kernel(pt_ref, q_ref, kn_ref, vn_ref, *rest, n_pages, page, tdec, nh, lambda_init):
    del pt_ref
    k_refs = rest[:n_pages] + (kn_ref,)
    v_refs = rest[n_pages:2 * n_pages] + (vn_ref,)
    lq1_ref, lk1_ref, lq2_ref, lk2_ref, g_ref, o_ref = rest[2 * n_pages:]
    hd = HEAD_DIM
    hw = 2 * hd
    ncol = 2 * nh * tdec
    past = n_pages * page
    q = q_ref[...] * (hd ** -0.5 * LOG2E)
    rid = lax.broadcasted_iota(jnp.int32, (ncol, hw), 0)
    cid = lax.broadcasted_iota(jnp.int32, (ncol, hw), 1)
    qt = jnp.zeros((ncol, hw), F32)
    for h in range(nh):
        for i in range(tdec):
            sel = (rid // (2 * tdec) == h) & (rid % tdec == i)
            qt = jnp.where(sel, jnp.broadcast_to(q[i:i + 1, h * hw:(h + 1) * hw], (ncol, hw)), qt)
    qt = jnp.where((rid // tdec) % 2 == cid // hd, qt, 0.0)
    zq = jnp.zeros_like(qt)
    qt2 = jnp.concatenate([jnp.concatenate([qt, zq], axis=1), jnp.concatenate([zq, qt], axis=1)],
                          axis=0).astype(BF16)
    col2 = lax.broadcasted_iota(jnp.int32, (1, 2 * ncol), 1)
    second = col2 // ncol
    col = col2 % ncol
    qidx = col % tdec
    hcol = col // (2 * tdec)
    slope = jnp.exp2((hcol + 1).astype(F32) * (-8.0 / nh)) * LOG2E

    def rows(n):
        r = lax.broadcasted_iota(jnp.int32, (n, 1), 0)
        return r // nh, r % nh

    tok, hrow = rows(page * nh)
    base = jnp.where(hrow == hcol, -slope * (past + qidx - tok - second * page).astype(F32), -jnp.inf)
    tokn, hrown = rows(tdec * nh)
    basen = jnp.where((hrown == hcol) & (qidx >= tokn) & (second == 0),
                      -slope * (qidx - tokn).astype(F32), -jnp.inf)
    blocks = [(j, j + 1, base, slope * float(j * page)) for j in range(0, n_pages - 1, 2)]
    if n_pages % 2:
        blocks.append((n_pages - 1, None, jnp.where(second == 0, base, -jnp.inf), slope * float((n_pages - 1) * page)))
    blocks.append((n_pages, None, basen, jnp.zeros_like(slope)))

    def pair(refs, a, b):
        xa = refs[a][...]
        xb = refs[b][...] if b is not None else jnp.zeros_like(xa)
        return jnp.concatenate([xa, xb], axis=1).astype(BF16)

    t_list = []
    m2 = jnp.full((1, 2 * ncol), NEG_BIG, F32)
    for a, b, bias, c in blocks:
        t = _dot_nt(pair(k_refs, a, b), qt2) + bias
        m2 = jnp.maximum(m2, jnp.max(t, axis=0, keepdims=True) + c)
        t_list.append(t)
    m = jnp.maximum(m2[:, :ncol], m2[:, ncol:])
    m2 = jnp.concatenate([m, m], axis=1)
    l2 = jnp.zeros((1, 2 * ncol), F32)
    acc2 = jnp.zeros((2 * ncol, 2 * hw), F32)
    for (a, b, bias, c), t in zip(blocks, t_list):
        p = jnp.exp2(t - (m2 - c))
        l2 = l2 + jnp.sum(p, axis=0, keepdims=True)
        p = p.astype(BF16)
        v = pair(v_refs, a, b)
        if p.shape[0] < LANES:
            pad = LANES - p.shape[0]
            p = jnp.concatenate([p, jnp.zeros((pad, 2 * ncol), BF16)], axis=0)
            v = jnp.concatenate([v, jnp.zeros((pad, 2 * hw), BF16)], axis=0)
        acc2 = acc2 + _dot_tn(p, v)
    l = l2[:, :ncol] + l2[:, ncol:]
    acc = acc2[:ncol, :hw] + acc2[ncol:, hw:]
    eye = lax.broadcasted_iota(jnp.int32, (ncol, ncol), 0) == lax.broadcasted_iota(jnp.int32, (ncol, ncol), 1)
    lcol = jnp.sum(jnp.where(eye, jnp.broadcast_to(l, (ncol, ncol)), 0.0), axis=1, keepdims=True)
    acc = acc / lcol
    lam = _lambda(lq1_ref, lk1_ref, lq2_ref, lk2_ref, lambda_init)
    g = g_ref[...]
    for h in range(nh):
        blk = acc[h * 2 * tdec:(h + 1) * 2 * tdec]
        o = blk[:tdec] - lam * blk[tdec:]
        o = o * lax.rsqrt(jnp.mean(o * o, axis=-1, keepdims=True) + LN_EPS) * g * (1.0 - lambda_init)
        o_ref[:, h * hw:(h + 1) * hw] = o.astype(o_ref.dtype)


def _decode_attention(q3, k3, v3, cache_k, cache_v, page_table, lam_vecs, subln_g, lambda_init):
    nb, tdec, da = q3.shape
    n_pages = page_table.shape[1]
    rows, hw = cache_k.shape[1:]
    nh = da // hw
    page = rows // nh
    new = pl.BlockSpec((None, tdec, da), lambda b, pt: (b, 0, 0))
    newp = pl.BlockSpec((None, tdec * nh, hw), lambda b, pt: (b, 0, 0))
    pages = [pl.BlockSpec((None, rows, hw), lambda b, pt, j=j: (pt[b, j], 0, 0)) for j in range(n_pages)]
    vec = pl.BlockSpec((1, HEAD_DIM), lambda b, pt: (0, 0))
    return pl.pallas_call(
        functools.partial(_dec_attn_kernel, n_pages=n_pages, page=page, tdec=tdec, nh=nh, lambda_init=lambda_init),
        grid_spec=pltpu.PrefetchScalarGridSpec(
            num_scalar_prefetch=1,
            grid=(nb,),
            in_specs=[new, newp, newp] + pages + pages + [vec, vec, vec, vec,
                                                          pl.BlockSpec((1, hw), lambda b, pt: (0, 0))],
            out_specs=new),
        out_shape=jax.ShapeDtypeStruct((nb, tdec, da), BF16),
        compiler_params=_params("arbitrary"),
        name="decode_attention",
    )(page_table, q3, k3.reshape(nb, tdec * nh, hw), v3.reshape(nb, tdec * nh, hw),
      *([cache_k] * n_pages), *([cache_v] * n_pages), *lam_vecs, subln_g)


def _mix_in_kernel(y_ref, o_ref, gs_ref, ga_ref, wglu_ref, wus_ref, wua_ref, out_ref):
    z = jax.nn.gelu(jnp.concatenate([y_ref[i] for i in range(y_ref.shape[0])], axis=1))
    zz = z * jax.nn.sigmoid(_dot(z.astype(BF16), wglu_ref[...]))
    y_ssm = _dot(zz.astype(BF16), wus_ref[...])
    y_att = _dot(o_ref[...], wua_ref[...])
    out_ref[...] = (jax.nn.sigmoid(gs_ref[...]) * y_ssm + jax.nn.sigmoid(ga_ref[...]) * y_att).astype(out_ref.dtype)


def _mix_in(y, o, gates, w_glu, w_up_ssm, w_up_att, tm):
    nslab, n, _ = y.shape
    da = o.shape[1]
    d = w_up_ssm.shape[1]
    full = lambda a: pl.BlockSpec(a.shape, lambda i: (0, 0))
    return pl.pallas_call(
        _mix_in_kernel,
        grid=(n // tm,),
        in_specs=[pl.BlockSpec((nslab, tm, LANES), lambda i: (0, i, 0)),
                  pl.BlockSpec((tm, da), lambda i: (i, 0)),
                  pl.BlockSpec((tm, d), lambda i: (i, 0)),
                  pl.BlockSpec((tm, d), lambda i: (i, 1)),
                  full(w_glu), full(w_up_ssm), full(w_up_att)],
        out_specs=pl.BlockSpec((tm, d), lambda i: (i, 0)),
        out_shape=jax.ShapeDtypeStruct((n, d), BF16),
        compiler_params=_params("arbitrary"),
        name="branch_mix",
    )(y, o, gates, gates, w_glu, w_up_ssm, w_up_att)


def _out_proj_kernel(mix_ref, x_ref, g1_ref, wo_ref, lg_ref, lb_ref, out_ref, *, alpha):
    r = alpha * x_ref[...] + _mod_val(g1_ref) * _dot(mix_ref[...], wo_ref[...])
    out_ref[...] = _ln(r) * lg_ref[...] + lb_ref[...]


def _out_proj(mix, x, mod3, rows_per_batch, w_o, ln_g, ln_b, alpha, tm):
    n, d = x.shape
    row = pl.BlockSpec((tm, d), lambda i: (i, 0))
    vec = pl.BlockSpec((1, d), lambda i: (0, 0))
    return pl.pallas_call(
        functools.partial(_out_proj_kernel, alpha=alpha),
        grid=(n // tm,),
        in_specs=[row, row, _mod_spec(mod3, tm, rows_per_batch, 2, 1),
                  pl.BlockSpec(w_o.shape, lambda i: (0, 0)), vec, vec],
        out_specs=row,
        out_shape=jax.ShapeDtypeStruct((n, d), F32),
        compiler_params=_params("arbitrary"),
        name="out_proj_ln1",
    )(mix, x, mod3, w_o, ln_g, ln_b)


def _ffn_kernel(x_ref, sh_ref, sc_ref, g2_ref, w1_ref, w2_ref, lg_ref, lb_ref, out_ref, *rest, alpha, nf):
    f = pl.program_id(1)
    h_ref, acc_ref = rest[-2:]
    if len(rest) == 4:
        w1b_ref, w2b_ref = rest[:2]
        w1b_ref[...] = w1_ref[...].astype(w1b_ref.dtype)
        w2b_ref[...] = w2_ref[...].astype(w2b_ref.dtype)
    else:
        w1b_ref, w2b_ref = w1_ref, w2_ref

    @pl.when(f == 0)
    def _():
        h_ref[...] = (_ln(x_ref[...]) * (1.0 + _mod_val(sc_ref)) + _mod_val(sh_ref)).astype(h_ref.dtype)
        acc_ref[...] = jnp.zeros_like(acc_ref)

    a = jnp.maximum(_dot(h_ref[...], w1b_ref[...]), 0.0)
    acc_ref[...] += _dot((a * a).astype(BF16), w2b_ref[...])

    @pl.when(f == nf - 1)
    def _():
        r = alpha * x_ref[...] + _mod_val(g2_ref) * acc_ref[...]
        out_ref[...] = _ln(r) * lg_ref[...] + lb_ref[...]


def _ffn(x, mod3, rows_per_batch, w1, w2, ln_g, ln_b, alpha, tm, tf):
    n, d = x.shape
    dff = w1.shape[1]
    nf = dff // tf
    emit = w1.dtype != BF16
    assert not emit or n == tm
    once = dict(pipeline_mode=pl.Buffered(1)) if emit else {}
    row = pl.BlockSpec((tm, d), lambda i, f: (i, 0), **once)
    vec = pl.BlockSpec((1, d), lambda i, f: (0, 0))
    w1spec = pl.BlockSpec((d, tf), lambda i, f: (0, f))
    w2spec = pl.BlockSpec((tf, d), lambda i, f: (f, 0))
    res = pl.pallas_call(
        functools.partial(_ffn_kernel, alpha=alpha, nf=nf),
        grid=(n // tm, nf),
        in_specs=[row,
                  _mod_spec(mod3, tm, rows_per_batch, 3, 2),
                  _mod_spec(mod3, tm, rows_per_batch, 4, 2),
                  _mod_spec(mod3, tm, rows_per_batch, 5, 2),
                  w1spec, w2spec, vec, vec],
        out_specs=[row] + ([w1spec, w2spec] if emit else []),
        out_shape=[jax.ShapeDtypeStruct((n, d), F32)]
        + ([jax.ShapeDtypeStruct(w1.shape, BF16), jax.ShapeDtypeStruct(w2.shape, BF16)] if emit else []),
        scratch_shapes=[pltpu.VMEM((tm, d), BF16), pltpu.VMEM((tm, d), F32)],
        compiler_params=_params("arbitrary", "arbitrary"),
        name="ffn_ln2",
    )(x, mod3, mod3, mod3, w1, w2, ln_g, ln_b)
    return res


def _trunk_layer(x3, mod3, s0, cache, page_table, wts, tabs, lambda_init, alpha):
    nbatch, seq, d = x3.shape
    n = nbatch * seq
    x = x3.reshape(n, d)
    ds = wts["w_glu"].shape[0]
    da = wts["w_up_att"].shape[0]
    nh = da // (2 * HEAD_DIM)
    tm = min(512, seq) if mod3.shape[1] == 1 else min(512, n)
    assert ds == da
    u, q, k, v, gates, *w_in_b = _in_proj(x, mod3, seq, wts["w_in"], ds, tm)

    pd, q_re, q_im, w_t, lp_re, lp_im = tabs
    t_chunk = PROMPT_CHUNK if seq % PROMPT_CHUNK == 0 else seq
    gp = lp_re.shape[1] * SSM_STATE
    lam_re = lp_re[t_chunk, :, :SSM_STATE].reshape(1, gp)
    lam_im = lp_im[t_chunk, :, :SSM_STATE].reshape(1, gp)
    y, f_re, f_im = _ssm(u, w_t, pd, q_re, q_im, s0[0], s0[1], lam_re, lam_im, t_chunk, nbatch, seq // t_chunk)

    lam_vecs = wts["lam_vecs"]
    if cache is None:
        slopes = jnp.asarray([2.0 ** (-8.0 * (i + 1) / nh) for i in range(nh)], F32)
        tq = min(512, seq)
        o = _attention(q, k, v, slopes, lam_vecs, wts["subln_g"], nbatch, seq, lambda_init, tq, tq)
    else:
        o = _decode_attention(q.reshape(nbatch, seq, da), k.reshape(nbatch, seq, da), v.reshape(nbatch, seq, da),
                              cache[0], cache[1], page_table, lam_vecs, wts["subln_g"], lambda_init)
        o = o.reshape(n, da)

    mix = _mix_in(y, o, gates, wts["w_glu"], wts["w_up_ssm"], wts["w_up_att"], tm)
    x1 = _out_proj(mix, x, mod3, seq, wts["w_o"], wts["ln1_g"], wts["ln1_b"], alpha, tm)
    tf = 1024 if wts["w_ff1"].dtype == BF16 else 512
    x2, *w_ff_b = _ffn(x1, mod3, seq, wts["w_ff1"], wts["w_ff2"], wts["ln2_g"], wts["ln2_b"], alpha, tm,
                       min(tf, wts["w_ff1"].shape[1]))
    return (x2.reshape(nbatch, seq, d), k.reshape(nbatch, seq, nh, 2 * HEAD_DIM),
            v.reshape(nbatch, seq, nh, 2 * HEAD_DIM), f_re, f_im, tuple(w_in_b) + tuple(w_ff_b))


def kernel(x_prompt, x_sample, c_prompt, c_sample, cache_k, cache_v, state_ssm_re, state_ssm_im, page_table,
           w_ada, b_ada, w_in, ssm_a_re, ssm_a_im, ssm_log_dt, ssm_b_re, ssm_b_im, ssm_c_re, ssm_c_im, ssm_d,
           w_glu, w_up_ssm, lam_q1, lam_k1, lam_q2, lam_k2, subln_g, w_up_att, w_o, ln1_g, ln1_b, w_ff1, w_ff2,
           ln2_g, ln2_b):
    depth = w_in.shape[0]
    bp, seq_p, d = x_prompt.shape
    bs, seq_s, _ = x_sample.shape
    g, p = ssm_a_re.shape[1:]
    alpha = (2 * depth) ** 0.25
    npow = PROMPT_CHUNK + 1
    xp, xs = x_prompt, x_sample
    outs_p, outs_s = [], []
    c_rows = jnp.concatenate([c_sample, c_prompt], axis=0)
    c_rows = jnp.pad(c_rows, ((0, (-(bp + bs)) % 8), (0, 0)))
    n_pool, page, nh, hw = cache_k.shape[1:]
    for l in range(depth):
        lambda_init = 0.8 - 0.6 * math.exp(-0.3 * l)
        mod, mod_s = _ada(c_rows, w_ada[l], b_ada[l].reshape(1, 6 * d), min(1024, d), bs, seq_s)
        mod_p = mod[bs:bs + bp].reshape(bp, 1, 6 * d)
        wts = {
            "w_in": w_in[l], "w_ff1": w_ff1[l], "w_ff2": w_ff2[l],
            "w_glu": w_glu[l].astype(BF16), "w_up_ssm": w_up_ssm[l].astype(BF16),
            "w_up_att": w_up_att[l].astype(BF16), "w_o": w_o[l].astype(BF16),
            "ln1_g": ln1_g[l].reshape(1, d), "ln1_b": ln1_b[l].reshape(1, d),
            "ln2_g": ln2_g[l].reshape(1, d), "ln2_b": ln2_b[l].reshape(1, d),
            "subln_g": subln_g[l].reshape(1, 2 * HEAD_DIM),
            "lam_vecs": tuple(a[l].reshape(1, HEAD_DIM) for a in (lam_q1, lam_k1, lam_q2, lam_k2)),
        }
        tabs = _ssm_prep(ssm_a_re[l], ssm_a_im[l], ssm_log_dt[l], ssm_b_re[l], ssm_b_im[l], ssm_c_re[l],
                         ssm_c_im[l], ssm_d[l], npow)
        tabs = list(tabs)
        tabs[3] = _toeplitz_tiles(tabs[3], PROMPT_CHUNK)
        cache = (cache_k[l].reshape(n_pool, page * nh, hw), cache_v[l].reshape(n_pool, page * nh, hw))
        s0 = (state_ssm_re[l].reshape(bs, g * p), state_ssm_im[l].reshape(bs, g * p))
        res_s = _trunk_layer(xs, mod_s, s0, cache, page_table, wts, tabs, lambda_init, alpha)
        wts["w_in"], wts["w_ff1"], wts["w_ff2"] = res_s[5]
        zeros = jnp.zeros((bp, g * p), F32)
        res_p = _trunk_layer(xp, mod_p, (zeros, zeros), None, None, wts, tabs, lambda_init, alpha)
        xp, xs = res_p[0], res_s[0]
        outs_p.append(res_p[1:5])
        outs_s.append(res_s[1:5])

    def stack(outs, i, shape=None):
        a = jnp.stack([o[i] for o in outs])
        return a if shape is None else a.reshape((depth,) + shape)

    return (xp, xs,
            stack(outs_p, 0), stack(outs_p, 1), stack(outs_p, 2, (bp, g, p)), stack(outs_p, 3, (bp, g, p)),
            stack(outs_s, 0), stack(outs_s, 1), stack(outs_s, 2, (bs, g, p)), stack(outs_s, 3, (bs, g, p)))
```

```python
import functools
import math

import jax
import jax.numpy as jnp
from jax import lax
from jax.experimental import pallas as pl
from jax.experimental.pallas import tpu as pltpu

F32 = jnp.float32
BF16 = jnp.bfloat16

SSM_GROUP = 16
SSM_STATE = 64
HEAD_DIM = 64
LANES = 128
BLOCK_W = 256
GROUPS_PER_BLOCK = BLOCK_W // SSM_GROUP
STATE_W = GROUPS_PER_BLOCK * SSM_STATE
PROMPT_CHUNK = 16
LN_EPS = 1e-5
NEG_BIG = -1e30
LOG2E = 1.4426950408889634
VMEM_LIMIT = 56 * 1024 * 1024


def _params(*sem):
    return pltpu.CompilerParams(dimension_semantics=sem, vmem_limit_bytes=VMEM_LIMIT)


def _ln(x):
    mu = jnp.mean(x, axis=-1, keepdims=True)
    xc = x - mu
    var = jnp.mean(xc * xc, axis=-1, keepdims=True)
    return xc * lax.rsqrt(var + LN_EPS)


def _dot(a, b):
    return jnp.dot(a, b, preferred_element_type=F32)


def _dot_nt(a, b):
    return lax.dot_general(a, b, (((1,), (1,)), ((), ())), preferred_element_type=F32)


def _dot_tn(a, b):
    return lax.dot_general(a, b, (((0,), (0,)), ((), ())), preferred_element_type=F32)


def _mod_spec(mod3, tm, rows_per_batch, col, ngrid):
    if mod3.shape[1] == 1:
        d = mod3.shape[2] // 6
        per = rows_per_batch // tm
        if ngrid == 1:
            return pl.BlockSpec((None, 1, d), lambda i: (i // per, 0, col))
        return pl.BlockSpec((None, 1, d), lambda i, j: (i // per, 0, col))
    nslab = mod3.shape[0] // 6
    once = pl.Buffered(1)
    if ngrid == 1:
        return pl.BlockSpec((nslab, tm, LANES), lambda i: (col, i, 0), pipeline_mode=once)
    return pl.BlockSpec((nslab, tm, LANES), lambda i, j: (col, i, 0), pipeline_mode=once)


def _mod_val(ref):
    if len(ref.shape) == 2:
        return ref[...]
    return jnp.concatenate([ref[s] for s in range(ref.shape[0])], axis=1)


def _ada_kernel(c_ref, w_ref, b_ref, o_ref, os_ref, *, nrep, seq):
    c = c_ref[...]
    a = (c * jax.nn.sigmoid(c)).astype(BF16)
    res = _dot(a, w_ref[...].astype(BF16)) + b_ref[...]
    o_ref[...] = res
    for s in range(os_ref.shape[0]):
        for t in range(seq):
            os_ref[s, pl.ds(t, nrep, stride=seq), :] = res[:nrep, s * LANES:(s + 1) * LANES]


def _ada(c, w, b, tn, nrep, seq):
    m, d = c.shape
    n = w.shape[1]
    return pl.pallas_call(
        functools.partial(_ada_kernel, nrep=nrep, seq=seq),
        grid=(n // tn,),
        in_specs=[pl.BlockSpec((m, d), lambda j: (0, 0)),
                  pl.BlockSpec((d, tn), lambda j: (0, j)),
                  pl.BlockSpec((1, tn), lambda j: (0, j))],
        out_specs=[pl.BlockSpec((m, tn), lambda j: (0, j)),
                   pl.BlockSpec((tn // LANES, nrep * seq, LANES), lambda j: (j, 0, 0))],
        out_shape=[jax.ShapeDtypeStruct((m, n), F32),
                   jax.ShapeDtypeStruct((n // LANES, nrep * seq, LANES), F32)],
        compiler_params=_params("arbitrary"),
        name="ada_mod",
    )(c, w, b)


def _in_proj_kernel(x_ref, sh_ref, sc_ref, w_ref, u_ref, q_ref, k_ref, v_ref, g_ref, *rest):
    j = pl.program_id(1)
    h_ref = rest[-1]
    if len(rest) == 2:
        wb_ref = rest[0]
        wb_ref[...] = w_ref[...].astype(wb_ref.dtype)
    else:
        wb_ref = w_ref

    @pl.when(j == 0)
    def _():
        h_ref[...] = (_ln(x_ref[...]) * (1.0 + _mod_val(sc_ref)) + _mod_val(sh_ref)).astype(h_ref.dtype)

    @pl.when(j == 0)
    def _():
        acc = _dot(h_ref[...], wb_ref[...])
        for s in range(u_ref.shape[0]):
            u_ref[s] = acc[:, s * LANES:(s + 1) * LANES]

    for idx, ref in ((1, q_ref), (2, k_ref), (3, v_ref)):
        @pl.when(j == idx)
        def _(ref=ref):
            ref[...] = _dot(h_ref[...], wb_ref[...])

    @pl.when(j >= 4)
    def _():
        g_ref[...] = _dot(h_ref[...], wb_ref[...])


WEIGHT_RING = 3


def _in_proj_ring_kernel(x_ref, sh_ref, sc_ref, w_hbm, u_ref, q_ref, k_ref, v_ref, g_ref, h_ref, wbuf, sem, *, nt):
    i = pl.program_id(0)
    j = pl.program_id(1)
    tn = wbuf.shape[2]
    total = pl.num_programs(0) * nt
    s = i * nt + j

    def copy(step):
        col = pl.multiple_of((step % nt) * tn, tn)
        slot = step % WEIGHT_RING
        return pltpu.make_async_copy(w_hbm.at[:, pl.ds(col, tn)], wbuf.at[slot], sem.at[slot])

    @pl.when(s == 0)
    def _():
        for first in range(WEIGHT_RING - 1):
            copy(first).start()

    @pl.when(s + WEIGHT_RING - 1 < total)
    def _():
        copy(s + WEIGHT_RING - 1).start()

    @pl.when(j == 0)
    def _():
        h_ref[...] = (_ln(x_ref[...]) * (1.0 + _mod_val(sc_ref)) + _mod_val(sh_ref)).astype(h_ref.dtype)

    copy(s).wait()
    slot = s % WEIGHT_RING

    @pl.when(j == 0)
    def _():
        acc = _dot(h_ref[...], wbuf[slot])
        for t in range(u_ref.shape[0]):
            u_ref[t] = acc[:, t * LANES:(t + 1) * LANES]

    for idx, ref in ((1, q_ref), (2, k_ref), (3, v_ref)):
        @pl.when(j == idx)
        def _(ref=ref):
            ref[...] = _dot(h_ref[...], wbuf[slot])

    @pl.when(j >= 4)
    def _():
        g_ref[...] = _dot(h_ref[...], wbuf[slot])


def _in_proj(x, mod3, rows_per_batch, w, width, tm):
    n, d = x.shape
    tn = width
    nt = w.shape[1] // tn
    emit = w.dtype != BF16
    assert not emit or n == tm
    once = dict(pipeline_mode=pl.Buffered(1)) if emit else {}
    row = pl.BlockSpec((tm, tn), lambda i, j: (i, 0), **once)
    wspec = pl.BlockSpec((d, tn), lambda i, j: (0, j))
    mat = jax.ShapeDtypeStruct((n, width), F32)
    ring = not emit and (n // tm) * nt >= WEIGHT_RING
    scratch = [pltpu.VMEM((tm, d), BF16)]
    if ring:
        scratch += [pltpu.VMEM((WEIGHT_RING, d, tn), BF16), pltpu.SemaphoreType.DMA((WEIGHT_RING,))]
    return pl.pallas_call(
        functools.partial(_in_proj_ring_kernel, nt=nt) if ring else _in_proj_kernel,
        grid=(n // tm, nt),
        in_specs=[pl.BlockSpec((tm, d), lambda i, j: (i, 0), **once),
                  _mod_spec(mod3, tm, rows_per_batch, 0, 2),
                  _mod_spec(mod3, tm, rows_per_batch, 1, 2),
                  pl.BlockSpec(memory_space=pl.ANY) if ring else wspec],
        out_specs=[pl.BlockSpec((tn // LANES, tm, LANES), lambda i, j: (0, i, 0), **once), row, row, row,
                   pl.BlockSpec((tm, tn), lambda i, j: (i, jnp.maximum(j - 4, 0)))] + ([wspec] if emit else []),
        out_shape=[jax.ShapeDtypeStruct((width // LANES, n, LANES), F32), mat, mat, mat,
                   jax.ShapeDtypeStruct((n, (nt - 4) * tn), F32)]
        + ([jax.ShapeDtypeStruct(w.shape, BF16)] if emit else []),
        scratch_shapes=scratch,
        compiler_params=_params("arbitrary", "arbitrary"),
        name="in_proj",
    )(x, mod3, mod3, w)


def _ssm_prep_kernel(are_ref, aim_ref, ldt_ref, bre_ref, bim_ref, cre_ref, cim_ref, d_ref,
                     pd_ref, qre_ref, qim_ref, kt_ref, lpre_ref, lpim_ref, *, npow, gt):
    c = SSM_GROUP
    p = SSM_STATE
    a_re = are_ref[...]
    a_im = aim_ref[...]
    dt = jnp.exp(ldt_ref[...])
    mag = jnp.exp(dt * a_re)
    ab_re = mag * jnp.cos(dt * a_im)
    ab_im = mag * jnp.sin(dt * a_im)
    den = a_re * a_re + a_im * a_im
    f_re = ((ab_re - 1.0) * a_re + ab_im * a_im) / den
    f_im = (ab_im * a_re - (ab_re - 1.0) * a_im) / den
    b_re = bre_ref[...]
    b_im = bim_ref[...]
    bb_re = f_re[:, None, :] * b_re - f_im[:, None, :] * b_im
    bb_im = f_re[:, None, :] * b_im + f_im[:, None, :] * b_re
    c_re = cre_ref[...]
    c_im = cim_ref[...]
    low = lax.broadcasted_iota(jnp.int32, (1, 1, 2 * p), 2) < p
    p_re = jnp.ones_like(a_re)
    p_im = jnp.zeros_like(a_re)
    g_re_all, g_im_all = [], []
    for n in range(npow):
        pr = p_re[:, None, :]
        pi = p_im[:, None, :]
        pw_re = pr * bb_re - pi * bb_im
        pw_im = pr * bb_im + pi * bb_re
        pd_ref[n] = jnp.concatenate([pw_re, pw_im], axis=-1).reshape(gt * c, 4 * p)
        g_re = c_re * pr - c_im * pi
        g_im = -(c_re * pi + c_im * pr)
        g_re_all.append(g_re)
        g_im_all.append(g_im)
        lpre_ref[n] = p_re
        lpim_ref[n] = p_im
        p_re, p_im = p_re * ab_re - p_im * ab_im, p_re * ab_im + p_im * ab_re
    g_re_all = jnp.concatenate(g_re_all, axis=1)
    g_im_all = jnp.concatenate(g_im_all, axis=1)
    for gg in range(0, gt, 2):
        sl = slice(gg // 2 * 2 * p, (gg // 2 + 1) * 2 * p)
        qre_ref[:, sl] = jnp.where(low[0], g_re_all[gg], g_re_all[gg + 1])
        qim_ref[:, sl] = jnp.where(low[0], g_im_all[gg], g_im_all[gg + 1])
    rows = kt_ref.shape[2]
    zpad = jnp.zeros((gt, rows - npow * c, 2 * p), F32)
    dn = (((2,), (2,)), ((0,), (0,)))
    kt = (lax.dot_general(jnp.where(low, bb_re, 0.0), jnp.concatenate([g_re_all, zpad], axis=1), dn,
                          precision=lax.Precision.HIGHEST, preferred_element_type=F32)
          + lax.dot_general(jnp.where(low, bb_im, 0.0), jnp.concatenate([g_im_all, zpad], axis=1), dn,
                            precision=lax.Precision.HIGHEST, preferred_element_type=F32))
    eye = (lax.broadcasted_iota(jnp.int32, (c, rows), 0) == lax.broadcasted_iota(jnp.int32, (c, rows), 1))
    kt_ref[...] = kt + jnp.where(eye[None], d_ref[...], 0.0)


def _ssm_prep(a_re, a_im, log_dt, b_re, b_im, c_re, c_im, d_skip, npow):
    g, p = a_re.shape
    c = SSM_GROUP
    gt = 8
    kt_w = -(-npow * c // LANES) * LANES
    dup = lambda a: jnp.concatenate([a, a], axis=-1)
    gp = pl.BlockSpec((gt, 2 * p), lambda i: (i, 0))
    gcp = pl.BlockSpec((gt, c, 2 * p), lambda i: (i, 0, 0))
    qspec = pl.BlockSpec((npow * c, gt * p), lambda i: (0, i))
    lspec = pl.BlockSpec((npow, gt, 2 * p), lambda i: (0, i, 0))
    return pl.pallas_call(
        functools.partial(_ssm_prep_kernel, npow=npow, gt=gt),
        grid=(g // gt,),
        in_specs=[gp, gp, pl.BlockSpec((gt, 1), lambda i: (i, 0)), gcp, gcp, gcp, gcp,
                  pl.BlockSpec((gt, c, 1), lambda i: (i, 0, 0))],
        out_specs=[pl.BlockSpec((npow, gt * c, 4 * p), lambda i: (0, i, 0)), qspec, qspec,
                   pl.BlockSpec((gt, c, kt_w), lambda i: (i, 0, 0)), lspec, lspec],
        out_shape=[jax.ShapeDtypeStruct((npow, g * c, 4 * p), F32),
                   jax.ShapeDtypeStruct((npow * c, g * p), F32),
                   jax.ShapeDtypeStruct((npow * c, g * p), F32),
                   jax.ShapeDtypeStruct((g, c, kt_w), F32),
                   jax.ShapeDtypeStruct((npow, g, 2 * p), F32),
                   jax.ShapeDtypeStruct((npow, g, 2 * p), F32)],
        compiler_params=_params("arbitrary"),
        name="ssm_prep",
    )(dup(a_re), dup(a_im), log_dt.reshape(g, 1), dup(jnp.swapaxes(b_re, 1, 2)), dup(jnp.swapaxes(b_im, 1, 2)),
      dup(c_re), dup(c_im), d_skip.reshape(g, c, 1))


def _toeplitz_kernel(kt_ref, w_ref, *, t_max):
    c = SSM_GROUP
    kw = kt_ref.shape[2]
    x = kt_ref[...].reshape(BLOCK_W, kw).astype(BF16)
    mask = _group_mask(BLOCK_W, BLOCK_W, c, c)
    sel_r = lax.broadcasted_iota(jnp.int32, (kw, BLOCK_W), 0)
    sel_c = lax.broadcasted_iota(jnp.int32, (kw, BLOCK_W), 1)
    for n in range(t_max):
        e = jnp.where((sel_r // c == n) & (sel_r % c == sel_c % c), 1.0, 0.0).astype(BF16)
        tile = jnp.where(mask, _dot(x, e), 0.0)
        w_ref[pl.ds((t_max - 1 - n) * BLOCK_W, BLOCK_W), :] = tile.astype(w_ref.dtype)


def _toeplitz_tiles(kt, t_max):
    g, c, kw = kt.shape
    gb = g // GROUPS_PER_BLOCK
    rows = t_max * BLOCK_W
    return pl.pallas_call(
        functools.partial(_toeplitz_kernel, t_max=t_max),
        grid=(gb,),
        in_specs=[pl.BlockSpec((GROUPS_PER_BLOCK, c, kw), lambda b: (b, 0, 0))],
        out_specs=pl.BlockSpec((None, rows, BLOCK_W), lambda b: (b, 0, 0)),
        out_shape=jax.ShapeDtypeStruct((gb, rows, BLOCK_W), BF16),
        compiler_params=_params("arbitrary"),
        name="ssm_toeplitz",
    )(kt)


def _group_mask(rows, cols, row_div, col_div):
    return (lax.broadcasted_iota(jnp.int32, (rows, cols), 0) // row_div
            == lax.broadcasted_iota(jnp.int32, (rows, cols), 1) // col_div)


def _ssm_kernel(u_ref, w_ref, pd_ref, qre_ref, qim_ref, s0re_ref, s0im_ref, lre_ref, lim_ref,
                y_ref, fre_ref, fim_ref, ucat_ref, v_ref, s_ref, *, t_chunk, nseq, nc):
    c = SSM_GROUP
    rb = nseq * nc
    t_max = w_ref.shape[0] // BLOCK_W
    for t in range(t_chunk):
        for s in range(BLOCK_W // LANES):
            lo = t * BLOCK_W + s * LANES
            ucat_ref[:, lo:lo + LANES] = u_ref[s, pl.ds(t, rb, stride=t_chunk), :].astype(BF16)
    mask = _group_mask(BLOCK_W, STATE_W, c, SSM_STATE)
    reps = STATE_W // LANES
    acc = jnp.zeros((rb, 2 * STATE_W), F32)
    for t in range(t_chunk):
        blk = pd_ref[t_chunk - 1 - t]
        w_re = jnp.where(mask, jnp.concatenate([blk[:, :LANES]] * reps, axis=1), 0.0)
        w_im = jnp.where(mask, jnp.concatenate([blk[:, LANES:]] * reps, axis=1), 0.0)
        w = jnp.concatenate([w_re, w_im], axis=1).astype(BF16)
        acc = acc + _dot(ucat_ref[:, t * BLOCK_W:(t + 1) * BLOCK_W], w)
    v_ref[0] = acc[:, :STATE_W]
    v_ref[1] = acc[:, STATE_W:]
    ar = lre_ref[...]
    ai = lim_ref[...]
    if nc == 1:
        sr = s0re_ref[...]
        si = s0im_ref[...]
        s_ref[0] = sr
        s_ref[1] = si
        fre_ref[...] = ar * sr - ai * si + v_ref[0]
        fim_ref[...] = ar * si + ai * sr + v_ref[1]
    else:
        def body(k, carry):
            new = []
            for b in range(nseq):
                sr, si = carry[2 * b], carry[2 * b + 1]
                row = pl.ds(b * nc + k, 1)
                s_ref[0, row, :] = sr
                s_ref[1, row, :] = si
                new.append(ar * sr - ai * si + v_ref[0, row, :])
                new.append(ar * si + ai * sr + v_ref[1, row, :])
            return tuple(new)

        init = []
        for b in range(nseq):
            init += [s0re_ref[b:b + 1, :], s0im_ref[b:b + 1, :]]
        fin = lax.fori_loop(0, nc, body, tuple(init))
        for b in range(nseq):
            fre_ref[b:b + 1, :] = fin[2 * b]
            fim_ref[b:b + 1, :] = fin[2 * b + 1]
    s_re = s_ref[0].astype(BF16)
    s_im = s_ref[1].astype(BF16)
    for t in range(t_chunk):
        rows = slice((t + 1) * c, (t + 2) * c)
        q_re = jnp.where(mask, jnp.concatenate([qre_ref[rows, :]] * GROUPS_PER_BLOCK, axis=0), 0.0).astype(BF16)
        q_im = jnp.where(mask, jnp.concatenate([qim_ref[rows, :]] * GROUPS_PER_BLOCK, axis=0), 0.0).astype(BF16)
        y = (_dot(ucat_ref[:, :(t + 1) * BLOCK_W], w_ref[(t_max - 1 - t) * BLOCK_W:, :])
             + _dot_nt(s_re, q_re) + _dot_nt(s_im, q_im))
        for s in range(BLOCK_W // LANES):
            y_ref[s, pl.ds(t, rb, stride=t_chunk), :] = y[:, s * LANES:(s + 1) * LANES]


def _ssm(u_slabs, w_t, pd, q_re, q_im, s0_re, s0_im, lam_re, lam_im, t_chunk, nb, nc):
    gb = w_t.shape[0]
    c = SSM_GROUP
    nseq = max(1, min(nb, 256 // nc))
    nblk = nb // nseq
    rb = nseq * nc
    per = BLOCK_W // LANES
    gp = gb * STATE_W
    uspec = pl.BlockSpec((per, rb * t_chunk, LANES), lambda g, i: (g, i, 0))
    qspec = pl.BlockSpec(((t_chunk + 1) * c, STATE_W), lambda g, i: (0, g))
    sspec = pl.BlockSpec((None, nseq, STATE_W), lambda g, i: (i, 0, g))
    lspec = pl.BlockSpec((1, STATE_W), lambda g, i: (0, g))
    fshape = jax.ShapeDtypeStruct((nblk, nseq, gp), F32)
    y, f_re, f_im = pl.pallas_call(
        functools.partial(_ssm_kernel, t_chunk=t_chunk, nseq=nseq, nc=nc),
        grid=(gb, nblk),
        in_specs=[uspec,
                  pl.BlockSpec((None,) + w_t.shape[1:], lambda g, i: (g, 0, 0)),
                  pl.BlockSpec((t_chunk, BLOCK_W, 2 * LANES), lambda g, i: (0, g, 0)),
                  qspec, qspec, sspec, sspec, lspec, lspec],
        out_specs=[uspec, sspec, sspec],
        out_shape=[jax.ShapeDtypeStruct(u_slabs.shape, F32), fshape, fshape],
        scratch_shapes=[pltpu.VMEM((rb, t_chunk * BLOCK_W), BF16), pltpu.VMEM((2, rb, STATE_W), F32),
                        pltpu.VMEM((2, rb, STATE_W), F32)],
        compiler_params=_params("arbitrary", "arbitrary"),
        name="ssm_chunked",
    )(u_slabs, w_t, pd, q_re, q_im, s0_re.reshape(nblk, nseq, gp), s0_im.reshape(nblk, nseq, gp), lam_re, lam_im)
    return y, f_re.reshape(nb, gp), f_im.reshape(nb, gp)


def _lambda(lq1_ref, lk1_ref, lq2_ref, lk2_ref, lambda_init):
    return (jnp.exp(jnp.sum(lq1_ref[...] * lk1_ref[...], axis=-1, keepdims=True))
            - jnp.exp(jnp.sum(lq2_ref[...] * lk2_ref[...], axis=-1, keepdims=True)) + lambda_init)


def _bias_columns(n, off, pieces, key_side):
    hw = 2 * HEAD_DIM
    lane = lax.broadcasted_iota(jnp.int32, (1, hw), 1) - off
    valid = (lane >= 0) & (lane < 4 * len(pieces))
    a = lane // 4
    kap = lane % 4
    s_lane = pieces[-1]
    for i in range(len(pieces) - 2, -1, -1):
        s_lane = jnp.where(a == i, pieces[i], s_lane)
    pos = lax.broadcasted_iota(jnp.int32, (n, 1), 0)
    hi = (pos // 64 * 64).astype(F32)
    lo = (pos % 64).astype(F32)
    if key_side:
        val = jnp.where(kap < 2, s_lane, jnp.where(kap == 2, hi, lo))
    else:
        val = jnp.where(kap == 0, -hi, jnp.where(kap == 1, -lo, s_lane))
    return jnp.where(valid, val, 0.0)


def _attn_kernel(slope_ref, q_ref, k_ref, v_ref, lq1_ref, lk1_ref, lq2_ref, lk2_ref, g_ref, o_ref,
                 rel_ref, ka_ref, qb_ref, m_ref, l_ref, acc_ref, *, tq, tk, nq, lambda_init):
    h = pl.program_id(1)
    qi = pl.program_id(2)
    hd = HEAD_DIM
    hw = 2 * hd
    slope2 = slope_ref[h] * LOG2E
    first = lax.broadcasted_iota(jnp.int32, (1, hw), 1) < hd

    @pl.when(qi == 0)
    def _():
        sv = jnp.full((1, hw), slope2, F32)
        s1 = sv.astype(BF16).astype(F32)
        s2 = (sv - s1).astype(BF16).astype(F32)
        s3 = (sv - s1 - s2).astype(BF16).astype(F32)
        pieces = (s1, s2, s3)
        qb_ref[0] = _bias_columns(tq, hd, pieces, False)
        qb_ref[1] = _bias_columns(tq, 0, pieces, False)
        ka_ref[0] = _bias_columns(tk, hd, pieces, True)
        ka_ref[1] = _bias_columns(tk, 0, pieces, True)
        rel_ref[...] = (lax.broadcasted_iota(jnp.int32, (tk, tq), 1)
                        - lax.broadcasted_iota(jnp.int32, (tk, tq), 0)).astype(F32)

    q = q_ref[...] * (hd ** -0.5 * LOG2E)
    qm = (jnp.where(first, q, qb_ref[0]).astype(BF16), jnp.where(first, qb_ref[1], q).astype(BF16))
    m_ref[...] = jnp.full_like(m_ref, NEG_BIG)
    l_ref[...] = jnp.zeros_like(l_ref)
    acc_ref[...] = jnp.zeros_like(acc_ref)

    def scores(k0, nk, q0, nq_, masked):
        kp = k0 % tk
        k = k_ref[pl.ds(k0, nk), :]
        km = (jnp.where(first, k, ka_ref[0, kp:kp + nk]).astype(BF16),
              jnp.where(first, ka_ref[1, kp:kp + nk], k).astype(BF16))
        ts = [_dot_nt(km[mp], qm[mp][q0:q0 + nq_]) for mp in range(2)]
        if masked:
            keep = rel_ref[:nk, :nq_] >= 0.0
            ts = [jnp.where(keep, t, -jnp.inf) for t in ts]
        return ts

    def update(k0, nk, q0, nq_, ts, c):
        v = v_ref[pl.ds(k0, nk), :].astype(BF16)
        ones = jnp.ones((16, nk), BF16)
        lanes = slice(q0, q0 + nq_)
        m_old = [m_ref[mp, :, lanes] for mp in range(2)]
        m_new = [jnp.maximum(m_old[mp], jnp.max(ts[mp], axis=0, keepdims=True) + c) for mp in range(2)]
        ps = [jnp.exp2(ts[mp] - (m_new[mp] - c)).astype(BF16) for mp in range(2)]
        for mp in range(2):
            alpha = jnp.exp2(m_old[mp] - m_new[mp])
            l_ref[mp, :, lanes] = alpha * l_ref[mp, :, lanes] + _dot(ones, ps[mp])[0:1]
            acc_ref[mp, :, lanes] = alpha * acc_ref[mp, :, lanes] + _dot_tn(v, ps[mp])
            m_ref[mp, :, lanes] = m_new[mp]

    half = tk // 2
    for q_blk in range(nq):
        @pl.when(qi == q_blk)
        def _(q_blk=q_blk):
            pieces_ = [(kj * tk, tk, 0, tq, False, -slope2 * float((q_blk - kj) * tq)) for kj in range(q_blk)]
            pieces_ += [(q_blk * tk, half, 0, tq, True, 0.0), (q_blk * tk + half, half, half, tq - half, True, 0.0)]
            nxt = scores(*pieces_[0][:5])
            for n, pc in enumerate(pieces_):
                cur = nxt
                if n + 1 < len(pieces_):
                    nxt = scores(*pieces_[n + 1][:5])
                update(*pc[:4], cur, pc[5])

    lam = _lambda(lq1_ref, lk1_ref, lq2_ref, lk2_ref, lambda_init)
    o = acc_ref[0] / l_ref[0] - lam * (acc_ref[1] / l_ref[1])
    o = o * lax.rsqrt(jnp.mean(o * o, axis=0, keepdims=True) + LN_EPS) * g_ref[...] * (1.0 - lambda_init)
    o_ref[...] = o.T.astype(o_ref.dtype)


def _attention(q, k, v, slopes, lam_vecs, subln_g, nbatch, seq, lambda_init, tq, tk):
    n, da = q.shape
    nh = da // (2 * HEAD_DIM)
    nq = seq // tq
    hw = 2 * HEAD_DIM
    vec = pl.BlockSpec((1, HEAD_DIM), lambda b, h, i: (0, 0))
    return pl.pallas_call(
        functools.partial(_attn_kernel, tq=tq, tk=tk, nq=nq, lambda_init=lambda_init),
        grid=(nbatch, nh, nq),
        in_specs=[pl.BlockSpec(memory_space=pltpu.SMEM),
                  pl.BlockSpec((tq, hw), lambda b, h, i: (b * nq + i, h)),
                  pl.BlockSpec((seq, hw), lambda b, h, i: (b, h)),
                  pl.BlockSpec((seq, hw), lambda b, h, i: (b, h)),
                  vec, vec, vec, vec,
                  pl.BlockSpec((hw, 1), lambda b, h, i: (0, 0))],
        out_specs=pl.BlockSpec((tq, hw), lambda b, h, i: (b * nq + i, h)),
        out_shape=jax.ShapeDtypeStruct((n, da), BF16),
        scratch_shapes=[pltpu.VMEM((tk, tq), F32), pltpu.VMEM((2, tk, hw), F32), pltpu.VMEM((2, tq, hw), F32),
                        pltpu.VMEM((2, 1, tq), F32),
                        pltpu.VMEM((2, 1, tq), F32), pltpu.VMEM((2, hw, tq), F32)],
        compiler_params=_params("arbitrary", "arbitrary", "arbitrary"),
        name="prompt_attention",
    )(slopes, q, k, v, *lam_vecs, subln_g.reshape(hw, 1))


def _dec_attn_kernel(pt_ref, q_ref, kn_ref, vn_ref, *rest, n_pages, page, tdec, nh, lambda_init):
    k_refs = rest[:n_pages] + (kn_ref,)
    cv_hbm, lq1_ref, lk1_ref, lq2_ref, lk2_ref, g_ref, o_ref, vbuf, vsem = rest[n_pages:]
    bi = pl.program_id(0)

    def v_copy(batch, j):
        slot = batch % 2
        return pltpu.make_async_copy(cv_hbm.at[pt_ref[batch, j]], vbuf.at[slot, j], vsem.at[slot])

    @pl.when(bi == 0)
    def _():
        for j in range(n_pages):
            v_copy(0, j).start()

    @pl.when(bi + 1 < pl.num_programs(0))
    def _():
        for j in range(n_pages):
            v_copy(bi + 1, j).start()

    hd = HEAD_DIM
    hw = 2 * hd
    ncol = 2 * nh * tdec
    past = n_pages * page
    q = q_ref[...] * (hd ** -0.5 * LOG2E)
    rid = lax.broadcasted_iota(jnp.int32, (ncol, hw), 0)
    cid = lax.broadcasted_iota(jnp.int32, (ncol, hw), 1)
    qt = jnp.zeros((ncol, hw), F32)
    for h in range(nh):
        for i in range(tdec):
            sel = (rid // (2 * tdec) == h) & (rid % tdec == i)
            qt = jnp.where(sel, jnp.broadcast_to(q[i:i + 1, h * hw:(h + 1) * hw], (ncol, hw)), qt)
    qt = jnp.where((rid // tdec) % 2 == cid // hd, qt, 0.0)
    zq = jnp.zeros_like(qt)
    qt2 = jnp.concatenate([jnp.concatenate([qt, zq], axis=1), jnp.concatenate([zq, qt], axis=1)],
                          axis=0).astype(BF16)
    col2 = lax.broadcasted_iota(jnp.int32, (1, 2 * ncol), 1)
    second = col2 // ncol
    col = col2 % ncol
    qidx = col % tdec
    hcol = col // (2 * tdec)
    slope = jnp.exp2((hcol + 1).astype(F32) * (-8.0 / nh)) * LOG2E

    def rows(n):
        r = lax.broadcasted_iota(jnp.int32, (n, 1), 0)
        return r // nh, r % nh

    tok, hrow = rows(page * nh)
    base = jnp.where(hrow == hcol, -slope * (past + qidx - tok - second * page).astype(F32), -jnp.inf)
    tokn, hrown = rows(tdec * nh)
    basen = jnp.where((hrown == hcol) & (qidx >= tokn) & (second == 0),
                      -slope * (qidx - tokn).astype(F32), -jnp.inf)
    blocks = [(j, j + 1, base, slope * float(j * page)) for j in range(0, n_pages - 1, 2)]
    if n_pages % 2:
        blocks.append((n_pages - 1, None, jnp.where(second == 0, base, -jnp.inf), slope * float((n_pages - 1) * page)))
    blocks.append((n_pages, None, basen, jnp.zeros_like(slope)))

    def pair(load, a, b):
        xa = load(a)
        xb = load(b) if b is not None else jnp.zeros_like(xa)
        return jnp.concatenate([xa, xb], axis=1).astype(BF16)

    k_load = lambda j: k_refs[j][...]
    v_load = lambda j: vn_ref[...] if j == n_pages else vbuf[bi % 2, j]

    t_list = []
    m2 = jnp.full((1, 2 * ncol), NEG_BIG, F32)
    for a, b, bias, c in blocks:
        t = _dot_nt(pair(k_load, a, b), qt2) + bias
        m2 = jnp.maximum(m2, jnp.max(t, axis=0, keepdims=True) + c)
        t_list.append(t)
    m = jnp.maximum(m2[:, :ncol], m2[:, ncol:])
    m2 = jnp.concatenate([m, m], axis=1)
    for j in range(n_pages):
        v_copy(bi, j).wait()
    l2 = jnp.zeros((1, 2 * ncol), F32)
    acc2 = jnp.zeros((2 * ncol, 2 * hw), F32)
    for (a, b, bias, c), t in zip(blocks, t_list):
        p = jnp.exp2(t - (m2 - c))
        l2 = l2 + jnp.sum(p, axis=0, keepdims=True)
        p = p.astype(BF16)
        v = pair(v_load, a, b)
        if p.shape[0] < LANES:
            pad = LANES - p.shape[0]
            p = jnp.concatenate([p, jnp.zeros((pad, 2 * ncol), BF16)], axis=0)
            v = jnp.concatenate([v, jnp.zeros((pad, 2 * hw), BF16)], axis=0)
        acc2 = acc2 + _dot_tn(p, v)
    l = l2[:, :ncol] + l2[:, ncol:]
    acc = acc2[:ncol, :hw] + acc2[ncol:, hw:]
    eye = lax.broadcasted_iota(jnp.int32, (ncol, ncol), 0) == lax.broadcasted_iota(jnp.int32, (ncol, ncol), 1)
    lcol = jnp.sum(jnp.where(eye, jnp.broadcast_to(l, (ncol, ncol)), 0.0), axis=1, keepdims=True)
    acc = acc / lcol
    lam = _lambda(lq1_ref, lk1_ref, lq2_ref, lk2_ref, lambda_init)
    g = g_ref[...]
    for h in range(nh):
        blk = acc[h * 2 * tdec:(h + 1) * 2 * tdec]
        o = blk[:tdec] - lam * blk[tdec:]
        o = o * lax.rsqrt(jnp.mean(o * o, axis=-1, keepdims=True) + LN_EPS) * g * (1.0 - lambda_init)
        o_ref[:, h * hw:(h + 1) * hw] = o.astype(o_ref.dtype)


def _decode_attention(q3, k3, v3, cache_k, cache_v, page_table, lam_vecs, subln_g, lambda_init):
    nb, tdec, da = q3.shape
    n_pages = page_table.shape[1]
    rows, hw = cache_k.shape[1:]
    nh = da // hw
    page = rows // nh
    new = pl.BlockSpec((None, tdec, da), lambda b, pt: (b, 0, 0))
    newp = pl.BlockSpec((None, tdec * nh, hw), lambda b, pt: (b, 0, 0))
    pages = [pl.BlockSpec((None, rows, hw), lambda b, pt, j=j: (pt[b, j], 0, 0)) for j in range(n_pages)]
    vec = pl.BlockSpec((1, HEAD_DIM), lambda b, pt: (0, 0))
    return pl.pallas_call(
        functools.partial(_dec_attn_kernel, n_pages=n_pages, page=page, tdec=tdec, nh=nh, lambda_init=lambda_init),
        grid_spec=pltpu.PrefetchScalarGridSpec(
            num_scalar_prefetch=1,
            grid=(nb,),
            in_specs=[new, newp, newp] + pages + [pl.BlockSpec(memory_space=pl.ANY), vec, vec, vec, vec,
                                                  pl.BlockSpec((1, hw), lambda b, pt: (0, 0))],
            out_specs=new,
            scratch_shapes=[pltpu.VMEM((2, n_pages, rows, hw), cache_v.dtype), pltpu.SemaphoreType.DMA((2,))]),
        out_shape=jax.ShapeDtypeStruct((nb, tdec, da), BF16),
        compiler_params=_params("arbitrary"),
        name="decode_attention",
    )(page_table, q3, k3.reshape(nb, tdec * nh, hw), v3.reshape(nb, tdec * nh, hw),
      *([cache_k] * n_pages), cache_v, *lam_vecs, subln_g)


def _mix_in_kernel(y_ref, o_ref, gs_ref, ga_ref, wglu_ref, wus_ref, wua_ref, out_ref):
    z = jax.nn.gelu(jnp.concatenate([y_ref[i] for i in range(y_ref.shape[0])], axis=1))
    zz = z * jax.nn.sigmoid(_dot(z.astype(BF16), wglu_ref[...]))
    y_ssm = _dot(zz.astype(BF16), wus_ref[...])
    y_att = _dot(o_ref[...], wua_ref[...])
    out_ref[...] = (jax.nn.sigmoid(gs_ref[...]) * y_ssm + jax.nn.sigmoid(ga_ref[...]) * y_att).astype(out_ref.dtype)


def _mix_in(y, o, gates, w_glu, w_up_ssm, w_up_att, tm):
    nslab, n, _ = y.shape
    da = o.shape[1]
    d = w_up_ssm.shape[1]
    full = lambda a: pl.BlockSpec(a.shape, lambda i: (0, 0))
    return pl.pallas_call(
        _mix_in_kernel,
        grid=(n // tm,),
        in_specs=[pl.BlockSpec((nslab, tm, LANES), lambda i: (0, i, 0)),
                  pl.BlockSpec((tm, da), lambda i: (i, 0)),
                  pl.BlockSpec((tm, d), lambda i: (i, 0)),
                  pl.BlockSpec((tm, d), lambda i: (i, 1)),
                  full(w_glu), full(w_up_ssm), full(w_up_att)],
        out_specs=pl.BlockSpec((tm, d), lambda i: (i, 0)),
        out_shape=jax.ShapeDtypeStruct((n, d), BF16),
        compiler_params=_params("arbitrary"),
        name="branch_mix",
    )(y, o, gates, gates, w_glu, w_up_ssm, w_up_att)


def _out_proj_kernel(mix_ref, x_ref, g1_ref, wo_ref, lg_ref, lb_ref, out_ref, *, alpha):
    r = alpha * x_ref[...] + _mod_val(g1_ref) * _dot(mix_ref[...], wo_ref[...])
    out_ref[...] = _ln(r) * lg_ref[...] + lb_ref[...]


def _out_proj(mix, x, mod3, rows_per_batch, w_o, ln_g, ln_b, alpha, tm):
    n, d = x.shape
    row = pl.BlockSpec((tm, d), lambda i: (i, 0))
    vec = pl.BlockSpec((1, d), lambda i: (0, 0))
    return pl.pallas_call(
        functools.partial(_out_proj_kernel, alpha=alpha),
        grid=(n // tm,),
        in_specs=[row, row, _mod_spec(mod3, tm, rows_per_batch, 2, 1),
                  pl.BlockSpec(w_o.shape, lambda i: (0, 0)), vec, vec],
        out_specs=row,
        out_shape=jax.ShapeDtypeStruct((n, d), F32),
        compiler_params=_params("arbitrary"),
        name="out_proj_ln1",
    )(mix, x, mod3, w_o, ln_g, ln_b)


def _ffn_kernel(x_ref, sh_ref, sc_ref, g2_ref, w1_ref, w2_ref, lg_ref, lb_ref, out_ref, *rest, alpha, nf):
    f = pl.program_id(1)
    h_ref, acc_ref = rest[-2:]
    if len(rest) == 4:
        w1b_ref, w2b_ref = rest[:2]
        w1b_ref[...] = w1_ref[...].astype(w1b_ref.dtype)
        w2b_ref[...] = w2_ref[...].astype(w2b_ref.dtype)
    else:
        w1b_ref, w2b_ref = w1_ref, w2_ref

    @pl.when(f == 0)
    def _():
        h_ref[...] = (_ln(x_ref[...]) * (1.0 + _mod_val(sc_ref)) + _mod_val(sh_ref)).astype(h_ref.dtype)
        acc_ref[...] = jnp.zeros_like(acc_ref)

    a = jnp.maximum(_dot(h_ref[...], w1b_ref[...]), 0.0)
    acc_ref[...] += _dot((a * a).astype(BF16), w2b_ref[...])

    @pl.when(f == nf - 1)
    def _():
        r = alpha * x_ref[...] + _mod_val(g2_ref) * acc_ref[...]
        out_ref[...] = _ln(r) * lg_ref[...] + lb_ref[...]


def _ffn(x, mod3, rows_per_batch, w1, w2, ln_g, ln_b, alpha, tm, tf):
    n, d = x.shape
    dff = w1.shape[1]
    nf = dff // tf
    emit = w1.dtype != BF16
    assert not emit or n == tm
    once = dict(pipeline_mode=pl.Buffered(1)) if emit else {}
    row = pl.BlockSpec((tm, d), lambda i, f: (i, 0), **once)
    vec = pl.BlockSpec((1, d), lambda i, f: (0, 0))
    w1spec = pl.BlockSpec((d, tf), lambda i, f: (0, f))
    w2spec = pl.BlockSpec((tf, d), lambda i, f: (f, 0))
    res = pl.pallas_call(
        functools.partial(_ffn_kernel, alpha=alpha, nf=nf),
        grid=(n // tm, nf),
        in_specs=[row,
                  _mod_spec(mod3, tm, rows_per_batch, 3, 2),
                  _mod_spec(mod3, tm, rows_per_batch, 4, 2),
                  _mod_spec(mod3, tm, rows_per_batch, 5, 2),
                  w1spec, w2spec, vec, vec],
        out_specs=[row] + ([w1spec, w2spec] if emit else []),
        out_shape=[jax.ShapeDtypeStruct((n, d), F32)]
        + ([jax.ShapeDtypeStruct(w1.shape, BF16), jax.ShapeDtypeStruct(w2.shape, BF16)] if emit else []),
        scratch_shapes=[pltpu.VMEM((tm, d), BF16), pltpu.VMEM((tm, d), F32)],
        compiler_params=_params("arbitrary", "arbitrary"),
        name="ffn_ln2",
    )(x, mod3, mod3, mod3, w1, w2, ln_g, ln_b)
    return res


def _trunk_layer(x3, mod3, s0, cache, page_table, wts, tabs, lambda_init, alpha):
    nbatch, seq, d = x3.shape
    n = nbatch * seq
    x = x3.reshape(n, d)
    ds = wts["w_glu"].shape[0]
    da = wts["w_up_att"].shape[0]
    nh = da // (2 * HEAD_DIM)
    tm = min(512, seq) if mod3.shape[1] == 1 else min(512, n)
    assert ds == da
    u, q, k, v, gates, *w_in_b = _in_proj(x, mod3, seq, wts["w_in"], ds, tm)

    pd, q_re, q_im, w_t, lp_re, lp_im = tabs
    t_chunk = PROMPT_CHUNK if seq % PROMPT_CHUNK == 0 else seq
    gp = lp_re.shape[1] * SSM_STATE
    lam_re = lp_re[t_chunk, :, :SSM_STATE].reshape(1, gp)
    lam_im = lp_im[t_chunk, :, :SSM_STATE].reshape(1, gp)
    y, f_re, f_im = _ssm(u, w_t, pd, q_re, q_im, s0[0], s0[1], lam_re, lam_im, t_chunk, nbatch, seq // t_chunk)

    lam_vecs = wts["lam_vecs"]
    if cache is None:
        slopes = jnp.asarray([2.0 ** (-8.0 * (i + 1) / nh) for i in range(nh)], F32)
        tq = min(512, seq)
        o = _attention(q, k, v, slopes, lam_vecs, wts["subln_g"], nbatch, seq, lambda_init, tq, tq)
    else:
        o = _decode_attention(q.reshape(nbatch, seq, da), k.reshape(nbatch, seq, da), v.reshape(nbatch, seq, da),
                              cache[0], cache[1], page_table, lam_vecs, wts["subln_g"], lambda_init)
        o = o.reshape(n, da)

    mix = _mix_in(y, o, gates, wts["w_glu"], wts["w_up_ssm"], wts["w_up_att"], tm)
    x1 = _out_proj(mix, x, mod3, seq, wts["w_o"], wts["ln1_g"], wts["ln1_b"], alpha, tm)
    tf = 1024 if wts["w_ff1"].dtype == BF16 else 512
    x2, *w_ff_b = _ffn(x1, mod3, seq, wts["w_ff1"], wts["w_ff2"], wts["ln2_g"], wts["ln2_b"], alpha, tm,
                       min(tf, wts["w_ff1"].shape[1]))
    return (x2.reshape(nbatch, seq, d), k.reshape(nbatch, seq, nh, 2 * HEAD_DIM),
            v.reshape(nbatch, seq, nh, 2 * HEAD_DIM), f_re, f_im, tuple(w_in_b) + tuple(w_ff_b))


def kernel(x_prompt, x_sample, c_prompt, c_sample, cache_k, cache_v, state_ssm_re, state_ssm_im, page_table,
           w_ada, b_ada, w_in, ssm_a_re, ssm_a_im, ssm_log_dt, ssm_b_re, ssm_b_im, ssm_c_re, ssm_c_im, ssm_d,
           w_glu, w_up_ssm, lam_q1, lam_k1, lam_q2, lam_k2, subln_g, w_up_att, w_o, ln1_g, ln1_b, w_ff1, w_ff2,
           ln2_g, ln2_b):
    depth = w_in.shape[0]
    bp, seq_p, d = x_prompt.shape
    bs, seq_s, _ = x_sample.shape
    g, p = ssm_a_re.shape[1:]
    alpha = (2 * depth) ** 0.25
    npow = PROMPT_CHUNK + 1
    xp, xs = x_prompt, x_sample
    outs_p, outs_s = [], []
    c_rows = jnp.concatenate([c_sample, c_prompt], axis=0)
    c_rows = jnp.pad(c_rows, ((0, (-(bp + bs)) % 8), (0, 0)))
    n_pool, page, nh, hw = cache_k.shape[1:]
    for l in range(depth):
        lambda_init = 0.8 - 0.6 * math.exp(-0.3 * l)
        mod, mod_s = _ada(c_rows, w_ada[l], b_ada[l].reshape(1, 6 * d), min(1024, d), bs, seq_s)
        mod_p = mod[bs:bs + bp].reshape(bp, 1, 6 * d)
        wts = {
            "w_in": w_in[l], "w_ff1": w_ff1[l], "w_ff2": w_ff2[l],
            "w_glu": w_glu[l].astype(BF16), "w_up_ssm": w_up_ssm[l].astype(BF16),
            "w_up_att": w_up_att[l].astype(BF16), "w_o": w_o[l].astype(BF16),
            "ln1_g": ln1_g[l].reshape(1, d), "ln1_b": ln1_b[l].reshape(1, d),
            "ln2_g": ln2_g[l].reshape(1, d), "ln2_b": ln2_b[l].reshape(1, d),
            "subln_g": subln_g[l].reshape(1, 2 * HEAD_DIM),
            "lam_vecs": tuple(a[l].reshape(1, HEAD_DIM) for a in (lam_q1, lam_k1, lam_q2, lam_k2)),
        }
        tabs = _ssm_prep(ssm_a_re[l], ssm_a_im[l], ssm_log_dt[l], ssm_b_re[l], ssm_b_im[l], ssm_c_re[l],
                         ssm_c_im[l], ssm_d[l], npow)
        tabs = list(tabs)
        tabs[3] = _toeplitz_tiles(tabs[3], PROMPT_CHUNK)
        cache = (cache_k[l].reshape(n_pool, page * nh, hw), cache_v[l].reshape(n_pool, page * nh, hw))
        s0 = (state_ssm_re[l].reshape(bs, g * p), state_ssm_im[l].reshape(bs, g * p))
        res_s = _trunk_layer(xs, mod_s, s0, cache, page_table, wts, tabs, lambda_init, alpha)
        wts["w_in"], wts["w_ff1"], wts["w_ff2"] = res_s[5]
        zeros = jnp.zeros((bp, g * p), F32)
        res_p = _trunk_layer(xp, mod_p, (zeros, zeros), None, None, wts, tabs, lambda_init, alpha)
        xp, xs = res_p[0], res_s[0]
        outs_p.append(res_p[1:5])
        outs_s.append(res_s[1:5])

    def stack(outs, i, shape=None):
        a = jnp.stack([o[i] for o in outs])
        return a if shape is None else a.reshape((depth,) + shape)

    return (xp, xs,
            stack(outs_p, 0), stack(outs_p, 1), stack(outs_p, 2, (bp, g, p)), stack(outs_p, 3, (bp, g, p)),
            stack(outs_s, 0), stack(outs_s, 1), stack(outs_s, 2, (bs, g, p)), stack(outs_s, 3, (bs, g, p)))
```

```python
import functools
import math

import jax
import jax.numpy as jnp
from jax import lax
from jax.experimental import pallas as pl
from jax.experimental.pallas import tpu as pltpu

F32 = jnp.float32
BF16 = jnp.bfloat16

SSM_GROUP = 16
SSM_STATE = 64
HEAD_DIM = 64
LANES = 128
BLOCK_W = 256
GROUPS_PER_BLOCK = BLOCK_W // SSM_GROUP
STATE_W = GROUPS_PER_BLOCK * SSM_STATE
PROMPT_CHUNK = 16
LN_EPS = 1e-5
NEG_BIG = -1e30
LOG2E = 1.4426950408889634
VMEM_LIMIT = 56 * 1024 * 1024


def _params(*sem):
    return pltpu.CompilerParams(dimension_semantics=sem, vmem_limit_bytes=VMEM_LIMIT)


def _ln(x):
    mu = jnp.mean(x, axis=-1, keepdims=True)
    xc = x - mu
    var = jnp.mean(xc * xc, axis=-1, keepdims=True)
    return xc * lax.rsqrt(var + LN_EPS)


def _dot(a, b):
    return jnp.dot(a, b, preferred_element_type=F32)


def _dot_nt(a, b):
    return lax.dot_general(a, b, (((1,), (1,)), ((), ())), preferred_element_type=F32)


def _dot_tn(a, b):
    return lax.dot_general(a, b, (((0,), (0,)), ((), ())), preferred_element_type=F32)


def _mod_spec(mod3, tm, rows_per_batch, col, ngrid):
    if mod3.shape[1] == 1:
        d = mod3.shape[2] // 6
        per = rows_per_batch // tm
        if ngrid == 1:
            return pl.BlockSpec((None, 1, d), lambda i: (i // per, 0, col))
        return pl.BlockSpec((None, 1, d), lambda i, j: (i // per, 0, col))
    nslab = mod3.shape[0] // 6
    once = pl.Buffered(1)
    if ngrid == 1:
        return pl.BlockSpec((nslab, tm, LANES), lambda i: (col, i, 0), pipeline_mode=once)
    return pl.BlockSpec((nslab, tm, LANES), lambda i, j: (col, i, 0), pipeline_mode=once)


def _mod_val(ref):
    if len(ref.shape) == 2:
        return ref[...]
    return jnp.concatenate([ref[s] for s in range(ref.shape[0])], axis=1)


def _ada_kernel(c_ref, w_ref, b_ref, o_ref, os_ref, *, nrep, seq):
    c = c_ref[...]
    a = (c * jax.nn.sigmoid(c)).astype(BF16)
    res = _dot(a, w_ref[...].astype(BF16)) + b_ref[...]
    o_ref[...] = res
    for s in range(os_ref.shape[0]):
        for t in range(seq):
            os_ref[s, pl.ds(t, nrep, stride=seq), :] = res[:nrep, s * LANES:(s + 1) * LANES]


def _ada(c, w, b, tn, nrep, seq):
    m, d = c.shape
    n = w.shape[1]
    return pl.pallas_call(
        functools.partial(_ada_kernel, nrep=nrep, seq=seq),
        grid=(n // tn,),
        in_specs=[pl.BlockSpec((m, d), lambda j: (0, 0)),
                  pl.BlockSpec((d, tn), lambda j: (0, j)),
                  pl.BlockSpec((1, tn), lambda j: (0, j))],
        out_specs=[pl.BlockSpec((m, tn), lambda j: (0, j)),
                   pl.BlockSpec((tn // LANES, nrep * seq, LANES), lambda j: (j, 0, 0))],
        out_shape=[jax.ShapeDtypeStruct((m, n), F32),
                   jax.ShapeDtypeStruct((n // LANES, nrep * seq, LANES), F32)],
        compiler_params=_params("arbitrary"),
        name="ada_mod",
    )(c, w, b)


def _in_proj_kernel(x_ref, sh_ref, sc_ref, w_ref, u_ref, q_ref, k_ref, v_ref, g_ref, *rest):
    j = pl.program_id(1)
    h_ref = rest[-1]
    if len(rest) == 2:
        wb_ref = rest[0]
        wb_ref[...] = w_ref[...].astype(wb_ref.dtype)
    else:
        wb_ref = w_ref

    @pl.when(j == 0)
    def _():
        h_ref[...] = (_ln(x_ref[...]) * (1.0 + _mod_val(sc_ref)) + _mod_val(sh_ref)).astype(h_ref.dtype)

    @pl.when(j == 0)
    def _():
        acc = _dot(h_ref[...], wb_ref[...])
        for s in range(u_ref.shape[0]):
            u_ref[s] = acc[:, s * LANES:(s + 1) * LANES]

    for idx, ref in ((1, q_ref), (2, k_ref), (3, v_ref)):
        @pl.when(j == idx)
        def _(ref=ref):
            ref[...] = _dot(h_ref[...], wb_ref[...])

    @pl.when(j >= 4)
    def _():
        g_ref[...] = _dot(h_ref[...], wb_ref[...])


WEIGHT_RING = 3


def _in_proj_ring_kernel(x_ref, sh_ref, sc_ref, w_hbm, u_ref, q_ref, k_ref, v_ref, g_ref, h_ref, wbuf, sem, *, nt):
    i = pl.program_id(0)
    j = pl.program_id(1)
    tn = wbuf.shape[2]
    total = pl.num_programs(0) * nt
    s = i * nt + j

    def copy(step):
        col = pl.multiple_of((step % nt) * tn, tn)
        slot = step % WEIGHT_RING
        return pltpu.make_async_copy(w_hbm.at[:, pl.ds(col, tn)], wbuf.at[slot], sem.at[slot])

    @pl.when(s == 0)
    def _():
        for first in range(WEIGHT_RING - 1):
            copy(first).start()

    @pl.when(s + WEIGHT_RING - 1 < total)
    def _():
        copy(s + WEIGHT_RING - 1).start()

    @pl.when(j == 0)
    def _():
        h_ref[...] = (_ln(x_ref[...]) * (1.0 + _mod_val(sc_ref)) + _mod_val(sh_ref)).astype(h_ref.dtype)

    copy(s).wait()
    slot = s % WEIGHT_RING

    @pl.when(j == 0)
    def _():
        acc = _dot(h_ref[...], wbuf[slot])
        for t in range(u_ref.shape[0]):
            u_ref[t] = acc[:, t * LANES:(t + 1) * LANES]

    for idx, ref in ((1, q_ref), (2, k_ref), (3, v_ref)):
        @pl.when(j == idx)
        def _(ref=ref):
            ref[...] = _dot(h_ref[...], wbuf[slot])

    @pl.when(j >= 4)
    def _():
        g_ref[...] = _dot(h_ref[...], wbuf[slot])


def _in_proj(x, mod3, rows_per_batch, w, width, tm):
    n, d = x.shape
    tn = width
    nt = w.shape[1] // tn
    emit = w.dtype != BF16
    assert not emit or n == tm
    once = dict(pipeline_mode=pl.Buffered(1)) if emit else {}
    row = pl.BlockSpec((tm, tn), lambda i, j: (i, 0), **once)
    wspec = pl.BlockSpec((d, tn), lambda i, j: (0, j))
    mat = jax.ShapeDtypeStruct((n, width), F32)
    ring = not emit and (n // tm) * nt >= WEIGHT_RING
    scratch = [pltpu.VMEM((tm, d), BF16)]
    if ring:
        scratch += [pltpu.VMEM((WEIGHT_RING, d, tn), BF16), pltpu.SemaphoreType.DMA((WEIGHT_RING,))]
    return pl.pallas_call(
        functools.partial(_in_proj_ring_kernel, nt=nt) if ring else _in_proj_kernel,
        grid=(n // tm, nt),
        in_specs=[pl.BlockSpec((tm, d), lambda i, j: (i, 0), **once),
                  _mod_spec(mod3, tm, rows_per_batch, 0, 2),
                  _mod_spec(mod3, tm, rows_per_batch, 1, 2),
                  pl.BlockSpec(memory_space=pl.ANY) if ring else wspec],
        out_specs=[pl.BlockSpec((tn // LANES, tm, LANES), lambda i, j: (0, i, 0), **once), row, row, row,
                   pl.BlockSpec((tm, tn), lambda i, j: (i, jnp.maximum(j - 4, 0)))] + ([wspec] if emit else []),
        out_shape=[jax.ShapeDtypeStruct((width // LANES, n, LANES), F32), mat, mat, mat,
                   jax.ShapeDtypeStruct((n, (nt - 4) * tn), F32)]
        + ([jax.ShapeDtypeStruct(w.shape, BF16)] if emit else []),
        scratch_shapes=scratch,
        compiler_params=_params("arbitrary", "arbitrary"),
        name="in_proj",
    )(x, mod3, mod3, w)


def _ssm_prep_kernel(are_ref, aim_ref, ldt_ref, bre_ref, bim_ref, cre_ref, cim_ref, d_ref,
                     pd_ref, qre_ref, qim_ref, kt_ref, lpre_ref, lpim_ref, *, npow, gt):
    c = SSM_GROUP
    p = SSM_STATE
    a_re = are_ref[...]
    a_im = aim_ref[...]
    dt = jnp.exp(ldt_ref[...])
    mag = jnp.exp(dt * a_re)
    ab_re = mag * jnp.cos(dt * a_im)
    ab_im = mag * jnp.sin(dt * a_im)
    den = a_re * a_re + a_im * a_im
    f_re = ((ab_re - 1.0) * a_re + ab_im * a_im) / den
    f_im = (ab_im * a_re - (ab_re - 1.0) * a_im) / den
    b_re = bre_ref[...]
    b_im = bim_ref[...]
    bb_re = f_re[:, None, :] * b_re - f_im[:, None, :] * b_im
    bb_im = f_re[:, None, :] * b_im + f_im[:, None, :] * b_re
    c_re = cre_ref[...]
    c_im = cim_ref[...]
    low = lax.broadcasted_iota(jnp.int32, (1, 1, 2 * p), 2) < p
    p_re = jnp.ones_like(a_re)
    p_im = jnp.zeros_like(a_re)
    g_re_all, g_im_all = [], []
    for n in range(npow):
        pr = p_re[:, None, :]
        pi = p_im[:, None, :]
        pw_re = pr * bb_re - pi * bb_im
        pw_im = pr * bb_im + pi * bb_re
        pd_ref[n] = jnp.concatenate([pw_re, pw_im], axis=-1).reshape(gt * c, 4 * p)
        g_re = c_re * pr - c_im * pi
        g_im = -(c_re * pi + c_im * pr)
        g_re_all.append(g_re)
        g_im_all.append(g_im)
        lpre_ref[n] = p_re
        lpim_ref[n] = p_im
        p_re, p_im = p_re * ab_re - p_im * ab_im, p_re * ab_im + p_im * ab_re
    g_re_all = jnp.concatenate(g_re_all, axis=1)
    g_im_all = jnp.concatenate(g_im_all, axis=1)
    for gg in range(0, gt, 2):
        sl = slice(gg // 2 * 2 * p, (gg // 2 + 1) * 2 * p)
        qre_ref[:, sl] = jnp.where(low[0], g_re_all[gg], g_re_all[gg + 1])
        qim_ref[:, sl] = jnp.where(low[0], g_im_all[gg], g_im_all[gg + 1])
    rows = kt_ref.shape[2]
    zpad = jnp.zeros((gt, rows - npow * c, 2 * p), F32)
    dn = (((2,), (2,)), ((0,), (0,)))
    kt = (lax.dot_general(jnp.where(low, bb_re, 0.0), jnp.concatenate([g_re_all, zpad], axis=1), dn,
                          precision=lax.Precision.HIGHEST, preferred_element_type=F32)
          + lax.dot_general(jnp.where(low, bb_im, 0.0), jnp.concatenate([g_im_all, zpad], axis=1), dn,
                            precision=lax.Precision.HIGHEST, preferred_element_type=F32))
    eye = (lax.broadcasted_iota(jnp.int32, (c, rows), 0) == lax.broadcasted_iota(jnp.int32, (c, rows), 1))
    kt_ref[...] = kt + jnp.where(eye[None], d_ref[...], 0.0)


def _ssm_prep(a_re, a_im, log_dt, b_re, b_im, c_re, c_im, d_skip, npow):
    g, p = a_re.shape
    c = SSM_GROUP
    gt = 8
    kt_w = -(-npow * c // LANES) * LANES
    dup = lambda a: jnp.concatenate([a, a], axis=-1)
    gp = pl.BlockSpec((gt, 2 * p), lambda i: (i, 0))
    gcp = pl.BlockSpec((gt, c, 2 * p), lambda i: (i, 0, 0))
    qspec = pl.BlockSpec((npow * c, gt * p), lambda i: (0, i))
    lspec = pl.BlockSpec((npow, gt, 2 * p), lambda i: (0, i, 0))
    return pl.pallas_call(
        functools.partial(_ssm_prep_kernel, npow=npow, gt=gt),
        grid=(g // gt,),
        in_specs=[gp, gp, pl.BlockSpec((gt, 1), lambda i: (i, 0)), gcp, gcp, gcp, gcp,
                  pl.BlockSpec((gt, c, 1), lambda i: (i, 0, 0))],
        out_specs=[pl.BlockSpec((npow, gt * c, 4 * p), lambda i: (0, i, 0)), qspec, qspec,
                   pl.BlockSpec((gt, c, kt_w), lambda i: (i, 0, 0)), lspec, lspec],
        out_shape=[jax.ShapeDtypeStruct((npow, g * c, 4 * p), F32),
                   jax.ShapeDtypeStruct((npow * c, g * p), F32),
                   jax.ShapeDtypeStruct((npow * c, g * p), F32),
                   jax.ShapeDtypeStruct((g, c, kt_w), F32),
                   jax.ShapeDtypeStruct((npow, g, 2 * p), F32),
                   jax.ShapeDtypeStruct((npow, g, 2 * p), F32)],
        compiler_params=_params("arbitrary"),
        name="ssm_prep",
    )(dup(a_re), dup(a_im), log_dt.reshape(g, 1), dup(jnp.swapaxes(b_re, 1, 2)), dup(jnp.swapaxes(b_im, 1, 2)),
      dup(c_re), dup(c_im), d_skip.reshape(g, c, 1))


def _toeplitz_kernel(kt_ref, w_ref, *, t_max):
    c = SSM_GROUP
    kw = kt_ref.shape[2]
    x = kt_ref[...].reshape(BLOCK_W, kw).astype(BF16)
    mask = _group_mask(BLOCK_W, BLOCK_W, c, c)
    sel_r = lax.broadcasted_iota(jnp.int32, (kw, BLOCK_W), 0)
    sel_c = lax.broadcasted_iota(jnp.int32, (kw, BLOCK_W), 1)
    for n in range(t_max):
        e = jnp.where((sel_r // c == n) & (sel_r % c == sel_c % c), 1.0, 0.0).astype(BF16)
        tile = jnp.where(mask, _dot(x, e), 0.0)
        w_ref[pl.ds((t_max - 1 - n) * BLOCK_W, BLOCK_W), :] = tile.astype(w_ref.dtype)


def _toeplitz_tiles(kt, t_max):
    g, c, kw = kt.shape
    gb = g // GROUPS_PER_BLOCK
    rows = t_max * BLOCK_W
    return pl.pallas_call(
        functools.partial(_toeplitz_kernel, t_max=t_max),
        grid=(gb,),
        in_specs=[pl.BlockSpec((GROUPS_PER_BLOCK, c, kw), lambda b: (b, 0, 0))],
        out_specs=pl.BlockSpec((None, rows, BLOCK_W), lambda b: (b, 0, 0)),
        out_shape=jax.ShapeDtypeStruct((gb, rows, BLOCK_W), BF16),
        compiler_params=_params("arbitrary"),
        name="ssm_toeplitz",
    )(kt)


def _group_mask(rows, cols, row_div, col_div):
    return (lax.broadcasted_iota(jnp.int32, (rows, cols), 0) // row_div
            == lax.broadcasted_iota(jnp.int32, (rows, cols), 1) // col_div)


def _ssm_kernel(u_ref, w_ref, pd_ref, qre_ref, qim_ref, s0re_ref, s0im_ref, lre_ref, lim_ref,
                y_ref, fre_ref, fim_ref, ucat_ref, v_ref, s_ref, *, t_chunk, nseq, nc):
    c = SSM_GROUP
    rb = nseq * nc
    t_max = w_ref.shape[0] // BLOCK_W
    for t in range(t_chunk):
        for s in range(BLOCK_W // LANES):
            lo = t * BLOCK_W + s * LANES
            ucat_ref[:, lo:lo + LANES] = u_ref[s, pl.ds(t, rb, stride=t_chunk), :].astype(BF16)
    mask = _group_mask(BLOCK_W, STATE_W, c, SSM_STATE)
    reps = STATE_W // LANES
    acc = jnp.zeros((rb, 2 * STATE_W), F32)
    for t in range(t_chunk):
        blk = pd_ref[t_chunk - 1 - t]
        w_re = jnp.where(mask, jnp.concatenate([blk[:, :LANES]] * reps, axis=1), 0.0)
        w_im = jnp.where(mask, jnp.concatenate([blk[:, LANES:]] * reps, axis=1), 0.0)
        w = jnp.concatenate([w_re, w_im], axis=1).astype(BF16)
        acc = acc + _dot(ucat_ref[:, t * BLOCK_W:(t + 1) * BLOCK_W], w)
    v_ref[0] = acc[:, :STATE_W]
    v_ref[1] = acc[:, STATE_W:]
    ar = lre_ref[...]
    ai = lim_ref[...]
    if nc == 1:
        sr = s0re_ref[...]
        si = s0im_ref[...]
        s_ref[0] = sr
        s_ref[1] = si
        fre_ref[...] = ar * sr - ai * si + v_ref[0]
        fim_ref[...] = ar * si + ai * sr + v_ref[1]
    else:
        def body(k, carry):
            new = []
            for b in range(nseq):
                sr, si = carry[2 * b], carry[2 * b + 1]
                row = pl.ds(b * nc + k, 1)
                s_ref[0, row, :] = sr
                s_ref[1, row, :] = si
                new.append(ar * sr - ai * si + v_ref[0, row, :])
                new.append(ar * si + ai * sr + v_ref[1, row, :])
            return tuple(new)

        init = []
        for b in range(nseq):
            init += [s0re_ref[b:b + 1, :], s0im_ref[b:b + 1, :]]
        fin = lax.fori_loop(0, nc, body, tuple(init))
        for b in range(nseq):
            fre_ref[b:b + 1, :] = fin[2 * b]
            fim_ref[b:b + 1, :] = fin[2 * b + 1]
    s_re = s_ref[0].astype(BF16)
    s_im = s_ref[1].astype(BF16)
    for t in range(t_chunk):
        rows = slice((t + 1) * c, (t + 2) * c)
        q_re = jnp.where(mask, jnp.concatenate([qre_ref[rows, :]] * GROUPS_PER_BLOCK, axis=0), 0.0).astype(BF16)
        q_im = jnp.where(mask, jnp.concatenate([qim_ref[rows, :]] * GROUPS_PER_BLOCK, axis=0), 0.0).astype(BF16)
        y = (_dot(ucat_ref[:, :(t + 1) * BLOCK_W], w_ref[(t_max - 1 - t) * BLOCK_W:, :])
             + _dot_nt(s_re, q_re) + _dot_nt(s_im, q_im))
        for s in range(BLOCK_W // LANES):
            y_ref[s, pl.ds(t, rb, stride=t_chunk), :] = y[:, s * LANES:(s + 1) * LANES]


def _ssm(u_slabs, w_t, pd, q_re, q_im, s0_re, s0_im, lam_re, lam_im, t_chunk, nb, nc):
    gb = w_t.shape[0]
    c = SSM_GROUP
    nseq = max(1, min(nb, 256 // nc))
    nblk = nb // nseq
    rb = nseq * nc
    per = BLOCK_W // LANES
    gp = gb * STATE_W
    uspec = pl.BlockSpec((per, rb * t_chunk, LANES), lambda g, i: (g, i, 0))
    qspec = pl.BlockSpec(((t_chunk + 1) * c, STATE_W), lambda g, i: (0, g))
    sspec = pl.BlockSpec((None, nseq, STATE_W), lambda g, i: (i, 0, g))
    lspec = pl.BlockSpec((1, STATE_W), lambda g, i: (0, g))
    fshape = jax.ShapeDtypeStruct((nblk, nseq, gp), F32)
    y, f_re, f_im = pl.pallas_call(
        functools.partial(_ssm_kernel, t_chunk=t_chunk, nseq=nseq, nc=nc),
        grid=(gb, nblk),
        in_specs=[uspec,
                  pl.BlockSpec((None,) + w_t.shape[1:], lambda g, i: (g, 0, 0)),
                  pl.BlockSpec((t_chunk, BLOCK_W, 2 * LANES), lambda g, i: (0, g, 0)),
                  qspec, qspec, sspec, sspec, lspec, lspec],
        out_specs=[uspec, sspec, sspec],
        out_shape=[jax.ShapeDtypeStruct(u_slabs.shape, F32), fshape, fshape],
        scratch_shapes=[pltpu.VMEM((rb, t_chunk * BLOCK_W), BF16), pltpu.VMEM((2, rb, STATE_W), F32),
                        pltpu.VMEM((2, rb, STATE_W), F32)],
        compiler_params=_params("arbitrary", "arbitrary"),
        name="ssm_chunked",
    )(u_slabs, w_t, pd, q_re, q_im, s0_re.reshape(nblk, nseq, gp), s0_im.reshape(nblk, nseq, gp), lam_re, lam_im)
    return y, f_re.reshape(nb, gp), f_im.reshape(nb, gp)


def _lambda(lq1_ref, lk1_ref, lq2_ref, lk2_ref, lambda_init):
    return (jnp.exp(jnp.sum(lq1_ref[...] * lk1_ref[...], axis=-1, keepdims=True))
            - jnp.exp(jnp.sum(lq2_ref[...] * lk2_ref[...], axis=-1, keepdims=True)) + lambda_init)


def _bias_columns(n, off, pieces, key_side):
    hw = 2 * HEAD_DIM
    lane = lax.broadcasted_iota(jnp.int32, (1, hw), 1) - off
    valid = (lane >= 0) & (lane < 4 * len(pieces))
    a = lane // 4
    kap = lane % 4
    s_lane = pieces[-1]
    for i in range(len(pieces) - 2, -1, -1):
        s_lane = jnp.where(a == i, pieces[i], s_lane)
    pos = lax.broadcasted_iota(jnp.int32, (n, 1), 0)
    hi = (pos // 64 * 64).astype(F32)
    lo = (pos % 64).astype(F32)
    if key_side:
        val = jnp.where(kap < 2, s_lane, jnp.where(kap == 2, hi, lo))
    else:
        val = jnp.where(kap == 0, -hi, jnp.where(kap == 1, -lo, s_lane))
    return jnp.where(valid, val, 0.0)


def _attn_kernel(slope_ref, q_ref, k_ref, v_ref, lq1_ref, lk1_ref, lq2_ref, lk2_ref, g_ref, o_ref,
                 rel_ref, ka_ref, qb_ref, m_ref, l_ref, acc_ref, *, tq, tk, nq, lambda_init):
    h = pl.program_id(1)
    qi = pl.program_id(2)
    hd = HEAD_DIM
    hw = 2 * hd
    slope2 = slope_ref[h] * LOG2E
    first = lax.broadcasted_iota(jnp.int32, (1, hw), 1) < hd

    @pl.when(qi == 0)
    def _():
        sv = jnp.full((1, hw), slope2, F32)
        s1 = sv.astype(BF16).astype(F32)
        s2 = (sv - s1).astype(BF16).astype(F32)
        s3 = (sv - s1 - s2).astype(BF16).astype(F32)
        pieces = (s1, s2, s3)
        qb_ref[0] = _bias_columns(tq, hd, pieces, False)
        qb_ref[1] = _bias_columns(tq, 0, pieces, False)
        ka_ref[0] = _bias_columns(tk, hd, pieces, True)
        ka_ref[1] = _bias_columns(tk, 0, pieces, True)
        rel_ref[...] = (lax.broadcasted_iota(jnp.int32, (tk, tq), 1)
                        - lax.broadcasted_iota(jnp.int32, (tk, tq), 0)).astype(F32)

    q = q_ref[...] * (hd ** -0.5 * LOG2E)
    qm = (jnp.where(first, q, qb_ref[0]).astype(BF16), jnp.where(first, qb_ref[1], q).astype(BF16))
    m_ref[...] = jnp.full_like(m_ref, NEG_BIG)
    l_ref[...] = jnp.zeros_like(l_ref)
    acc_ref[...] = jnp.zeros_like(acc_ref)

    def scores(k0, nk, q0, nq_, masked):
        kp = k0 % tk
        k = k_ref[pl.ds(k0, nk), :]
        km = (jnp.where(first, k, ka_ref[0, kp:kp + nk]).astype(BF16),
              jnp.where(first, ka_ref[1, kp:kp + nk], k).astype(BF16))
        ts = [_dot_nt(km[mp], qm[mp][q0:q0 + nq_]) for mp in range(2)]
        if masked:
            keep = rel_ref[:nk, :nq_] >= 0.0
            ts = [jnp.where(keep, t, -jnp.inf) for t in ts]
        return ts

    def update(k0, nk, q0, nq_, ts, c):
        v = v_ref[pl.ds(k0, nk), :].astype(BF16)
        ones = jnp.ones((16, nk), BF16)
        lanes = slice(q0, q0 + nq_)
        m_old = [m_ref[mp, :, lanes] for mp in range(2)]
        m_new = [jnp.maximum(m_old[mp], jnp.max(ts[mp], axis=0, keepdims=True) + c) for mp in range(2)]
        ps = [jnp.exp2(ts[mp] - (m_new[mp] - c)).astype(BF16) for mp in range(2)]
        for mp in range(2):
            alpha = jnp.exp2(m_old[mp] - m_new[mp])
            l_ref[mp, :, lanes] = alpha * l_ref[mp, :, lanes] + _dot(ones, ps[mp])[0:1]
            acc_ref[mp, :, lanes] = alpha * acc_ref[mp, :, lanes] + _dot_tn(v, ps[mp])
            m_ref[mp, :, lanes] = m_new[mp]

    half = tk // 2
    for q_blk in range(nq):
        @pl.when(qi == q_blk)
        def _(q_blk=q_blk):
            pieces_ = [(kj * tk, tk, 0, tq, False, -slope2 * float((q_blk - kj) * tq)) for kj in range(q_blk)]
            pieces_ += [(q_blk * tk, half, 0, tq, True, 0.0), (q_blk * tk + half, half, half, tq - half, True, 0.0)]
            nxt = scores(*pieces_[0][:5])
            for n, pc in enumerate(pieces_):
                cur = nxt
                if n + 1 < len(pieces_):
                    nxt = scores(*pieces_[n + 1][:5])
                update(*pc[:4], cur, pc[5])

    lam = _lambda(lq1_ref, lk1_ref, lq2_ref, lk2_ref, lambda_init)
    o = acc_ref[0] / l_ref[0] - lam * (acc_ref[1] / l_ref[1])
    o = o * lax.rsqrt(jnp.mean(o * o, axis=0, keepdims=True) + LN_EPS) * g_ref[...] * (1.0 - lambda_init)
    o_ref[...] = o.T.astype(o_ref.dtype)


def _attention(q, k, v, slopes, lam_vecs, subln_g, nbatch, seq, lambda_init, tq, tk):
    n, da = q.shape
    nh = da // (2 * HEAD_DIM)
    nq = seq // tq
    hw = 2 * HEAD_DIM
    vec = pl.BlockSpec((1, HEAD_DIM), lambda b, h, i: (0, 0))
    return pl.pallas_call(
        functools.partial(_attn_kernel, tq=tq, tk=tk, nq=nq, lambda_init=lambda_init),
        grid=(nbatch, nh, nq),
        in_specs=[pl.BlockSpec(memory_space=pltpu.SMEM),
                  pl.BlockSpec((tq, hw), lambda b, h, i: (b * nq + i, h)),
                  pl.BlockSpec((seq, hw), lambda b, h, i: (b, h)),
                  pl.BlockSpec((seq, hw), lambda b, h, i: (b, h)),
                  vec, vec, vec, vec,
                  pl.BlockSpec((hw, 1), lambda b, h, i: (0, 0))],
        out_specs=pl.BlockSpec((tq, hw), lambda b, h, i: (b * nq + i, h)),
        out_shape=jax.ShapeDtypeStruct((n, da), BF16),
        scratch_shapes=[pltpu.VMEM((tk, tq), F32), pltpu.VMEM((2, tk, hw), F32), pltpu.VMEM((2, tq, hw), F32),
                        pltpu.VMEM((2, 1, tq), F32),
                        pltpu.VMEM((2, 1, tq), F32), pltpu.VMEM((2, hw, tq), F32)],
        compiler_params=_params("arbitrary", "arbitrary", "arbitrary"),
        name="prompt_attention",
    )(slopes, q, k, v, *lam_vecs, subln_g.reshape(hw, 1))


def _dec_attn_kernel(pt_ref, q_ref, kn_ref, vn_ref, *rest, n_pages, page, tdec, nh, lambda_init):
    ck_hbm, cv_hbm, lq1_ref, lk1_ref, lq2_ref, lk2_ref, g_ref, o_ref, kbuf, ksem, vbuf, vsem = rest
    bi = pl.program_id(0)

    def k_copy(batch, j):
        slot = batch % 2
        return pltpu.make_async_copy(ck_hbm.at[pt_ref[batch, j]], kbuf.at[slot, j], ksem.at[slot, j // 2])

    def v_copy(batch, j):
        slot = batch % 2
        return pltpu.make_async_copy(cv_hbm.at[pt_ref[batch, j]], vbuf.at[slot, j], vsem.at[slot])

    def start_all(batch):
        for j in range(n_pages):
            k_copy(batch, j).start()
        for j in range(n_pages):
            v_copy(batch, j).start()

    @pl.when(bi == 0)
    def _():
        start_all(0)

    @pl.when(bi + 1 < pl.num_programs(0))
    def _():
        start_all(bi + 1)

    hd = HEAD_DIM
    hw = 2 * hd
    ncol = 2 * nh * tdec
    past = n_pages * page
    q = q_ref[...] * (hd ** -0.5 * LOG2E)
    rid = lax.broadcasted_iota(jnp.int32, (ncol, hw), 0)
    cid = lax.broadcasted_iota(jnp.int32, (ncol, hw), 1)
    qt = jnp.zeros((ncol, hw), F32)
    for h in range(nh):
        for i in range(tdec):
            sel = (rid // (2 * tdec) == h) & (rid % tdec == i)
            qt = jnp.where(sel, jnp.broadcast_to(q[i:i + 1, h * hw:(h + 1) * hw], (ncol, hw)), qt)
    qt = jnp.where((rid // tdec) % 2 == cid // hd, qt, 0.0)
    zq = jnp.zeros_like(qt)
    qt2 = jnp.concatenate([jnp.concatenate([qt, zq], axis=1), jnp.concatenate([zq, qt], axis=1)],
                          axis=0).astype(BF16)
    col2 = lax.broadcasted_iota(jnp.int32, (1, 2 * ncol), 1)
    second = col2 // ncol
    col = col2 % ncol
    qidx = col % tdec
    hcol = col // (2 * tdec)
    slope = jnp.exp2((hcol + 1).astype(F32) * (-8.0 / nh)) * LOG2E

    def rows(n):
        r = lax.broadcasted_iota(jnp.int32, (n, 1), 0)
        return r // nh, r % nh

    tok, hrow = rows(page * nh)
    base = jnp.where(hrow == hcol, -slope * (past + qidx - tok - second * page).astype(F32), -jnp.inf)
    tokn, hrown = rows(tdec * nh)
    basen = jnp.where((hrown == hcol) & (qidx >= tokn) & (second == 0),
                      -slope * (qidx - tokn).astype(F32), -jnp.inf)
    blocks = [(j, j + 1, base, slope * float(j * page)) for j in range(0, n_pages - 1, 2)]
    if n_pages % 2:
        blocks.append((n_pages - 1, None, jnp.where(second == 0, base, -jnp.inf), slope * float((n_pages - 1) * page)))
    blocks.append((n_pages, None, basen, jnp.zeros_like(slope)))

    def pair(load, a, b):
        xa = load(a)
        xb = load(b) if b is not None else jnp.zeros_like(xa)
        return jnp.concatenate([xa, xb], axis=1).astype(BF16)

    k_load = lambda j: kn_ref[...] if j == n_pages else kbuf[bi % 2, j]
    v_load = lambda j: vn_ref[...] if j == n_pages else vbuf[bi % 2, j]

    t_list = []
    m2 = jnp.full((1, 2 * ncol), NEG_BIG, F32)
    for a, b, bias, c in blocks:
        for j in (a, b):
            if j is not None and j < n_pages:
                k_copy(bi, j).wait()
        t = _dot_nt(pair(k_load, a, b), qt2) + bias
        m2 = jnp.maximum(m2, jnp.max(t, axis=0, keepdims=True) + c)
        t_list.append(t)
    m = jnp.maximum(m2[:, :ncol], m2[:, ncol:])
    m2 = jnp.concatenate([m, m], axis=1)
    for j in range(n_pages):
        v_copy(bi, j).wait()
    l2 = jnp.zeros((1, 2 * ncol), F32)
    acc2 = jnp.zeros((2 * ncol, 2 * hw), F32)
    for (a, b, bias, c), t in zip(blocks, t_list):
        p = jnp.exp2(t - (m2 - c))
        l2 = l2 + jnp.sum(p, axis=0, keepdims=True)
        p = p.astype(BF16)
        v = pair(v_load, a, b)
        if p.shape[0] < LANES:
            pad = LANES - p.shape[0]
            p = jnp.concatenate([p, jnp.zeros((pad, 2 * ncol), BF16)], axis=0)
            v = jnp.concatenate([v, jnp.zeros((pad, 2 * hw), BF16)], axis=0)
        acc2 = acc2 + _dot_tn(p, v)
    l = l2[:, :ncol] + l2[:, ncol:]
    acc = acc2[:ncol, :hw] + acc2[ncol:, hw:]
    eye = lax.broadcasted_iota(jnp.int32, (ncol, ncol), 0) == lax.broadcasted_iota(jnp.int32, (ncol, ncol), 1)
    lcol = jnp.sum(jnp.where(eye, jnp.broadcast_to(l, (ncol, ncol)), 0.0), axis=1, keepdims=True)
    acc = acc / lcol
    lam = _lambda(lq1_ref, lk1_ref, lq2_ref, lk2_ref, lambda_init)
    g = g_ref[...]
    for h in range(nh):
        blk = acc[h * 2 * tdec:(h + 1) * 2 * tdec]
        o = blk[:tdec] - lam * blk[tdec:]
        o = o * lax.rsqrt(jnp.mean(o * o, axis=-1, keepdims=True) + LN_EPS) * g * (1.0 - lambda_init)
        o_ref[:, h * hw:(h + 1) * hw] = o.astype(o_ref.dtype)


def _decode_attention(q3, k3, v3, cache_k, cache_v, page_table, lam_vecs, subln_g, lambda_init):
    nb, tdec, da = q3.shape
    n_pages = page_table.shape[1]
    rows, hw = cache_k.shape[1:]
    nh = da // hw
    page = rows // nh
    new = pl.BlockSpec((None, tdec, da), lambda b, pt: (b, 0, 0))
    newp = pl.BlockSpec((None, tdec * nh, hw), lambda b, pt: (b, 0, 0))
    vec = pl.BlockSpec((1, HEAD_DIM), lambda b, pt: (0, 0))
    hbm = pl.BlockSpec(memory_space=pl.ANY)
    pages = pltpu.VMEM((2, n_pages, rows, hw), cache_k.dtype)
    return pl.pallas_call(
        functools.partial(_dec_attn_kernel, n_pages=n_pages, page=page, tdec=tdec, nh=nh, lambda_init=lambda_init),
        grid_spec=pltpu.PrefetchScalarGridSpec(
            num_scalar_prefetch=1,
            grid=(nb,),
            in_specs=[new, newp, newp, hbm, hbm, vec, vec, vec, vec, pl.BlockSpec((1, hw), lambda b, pt: (0, 0))],
            out_specs=new,
            scratch_shapes=[pages, pltpu.SemaphoreType.DMA((2, (n_pages + 1) // 2)),
                            pages, pltpu.SemaphoreType.DMA((2,))]),
        out_shape=jax.ShapeDtypeStruct((nb, tdec, da), BF16),
        compiler_params=_params("arbitrary"),
        name="decode_attention",
    )(page_table, q3, k3.reshape(nb, tdec * nh, hw), v3.reshape(nb, tdec * nh, hw),
      cache_k, cache_v, *lam_vecs, subln_g)


def _mix_in_kernel(y_ref, o_ref, gs_ref, ga_ref, wglu_ref, wus_ref, wua_ref, out_ref):
    z = jax.nn.gelu(jnp.concatenate([y_ref[i] for i in range(y_ref.shape[0])], axis=1))
    zz = z * jax.nn.sigmoid(_dot(z.astype(BF16), wglu_ref[...]))
    y_ssm = _dot(zz.astype(BF16), wus_ref[...])
    y_att = _dot(o_ref[...], wua_ref[...])
    out_ref[...] = (jax.nn.sigmoid(gs_ref[...]) * y_ssm + jax.nn.sigmoid(ga_ref[...]) * y_att).astype(out_ref.dtype)


def _mix_in(y, o, gates, w_glu, w_up_ssm, w_up_att, tm):
    nslab, n, _ = y.shape
    da = o.shape[1]
    d = w_up_ssm.shape[1]
    full = lambda a: pl.BlockSpec(a.shape, lambda i: (0, 0))
    return pl.pallas_call(
        _mix_in_kernel,
        grid=(n // tm,),
        in_specs=[pl.BlockSpec((nslab, tm, LANES), lambda i: (0, i, 0)),
                  pl.BlockSpec((tm, da), lambda i: (i, 0)),
                  pl.BlockSpec((tm, d), lambda i: (i, 0)),
                  pl.BlockSpec((tm, d), lambda i: (i, 1)),
                  full(w_glu), full(w_up_ssm), full(w_up_att)],
        out_specs=pl.BlockSpec((tm, d), lambda i: (i, 0)),
        out_shape=jax.ShapeDtypeStruct((n, d), BF16),
        compiler_params=_params("arbitrary"),
        name="branch_mix",
    )(y, o, gates, gates, w_glu, w_up_ssm, w_up_att)


def _out_proj_kernel(mix_ref, x_ref, g1_ref, wo_ref, lg_ref, lb_ref, out_ref, *, alpha):
    r = alpha * x_ref[...] + _mod_val(g1_ref) * _dot(mix_ref[...], wo_ref[...])
    out_ref[...] = _ln(r) * lg_ref[...] + lb_ref[...]


def _out_proj(mix, x, mod3, rows_per_batch, w_o, ln_g, ln_b, alpha, tm):
    n, d = x.shape
    row = pl.BlockSpec((tm, d), lambda i: (i, 0))
    vec = pl.BlockSpec((1, d), lambda i: (0, 0))
    return pl.pallas_call(
        functools.partial(_out_proj_kernel, alpha=alpha),
        grid=(n // tm,),
        in_specs=[row, row, _mod_spec(mod3, tm, rows_per_batch, 2, 1),
                  pl.BlockSpec(w_o.shape, lambda i: (0, 0)), vec, vec],
        out_specs=row,
        out_shape=jax.ShapeDtypeStruct((n, d), F32),
        compiler_params=_params("arbitrary"),
        name="out_proj_ln1",
    )(mix, x, mod3, w_o, ln_g, ln_b)


def _ffn_kernel(x_ref, sh_ref, sc_ref, g2_ref, w1_ref, w2_ref, lg_ref, lb_ref, out_ref, *rest, alpha, nf):
    f = pl.program_id(1)
    h_ref, acc_ref = rest[-2:]
    if len(rest) == 4:
        w1b_ref, w2b_ref = rest[:2]
        w1b_ref[...] = w1_ref[...].astype(w1b_ref.dtype)
        w2b_ref[...] = w2_ref[...].astype(w2b_ref.dtype)
    else:
        w1b_ref, w2b_ref = w1_ref, w2_ref

    @pl.when(f == 0)
    def _():
        h_ref[...] = (_ln(x_ref[...]) * (1.0 + _mod_val(sc_ref)) + _mod_val(sh_ref)).astype(h_ref.dtype)
        acc_ref[...] = jnp.zeros_like(acc_ref)

    a = jnp.maximum(_dot(h_ref[...], w1b_ref[...]), 0.0)
    acc_ref[...] += _dot((a * a).astype(BF16), w2b_ref[...])

    @pl.when(f == nf - 1)
    def _():
        r = alpha * x_ref[...] + _mod_val(g2_ref) * acc_ref[...]
        out_ref[...] = _ln(r) * lg_ref[...] + lb_ref[...]


def _ffn(x, mod3, rows_per_batch, w1, w2, ln_g, ln_b, alpha, tm, tf):
    n, d = x.shape
    dff = w1.shape[1]
    nf = dff // tf
    emit = w1.dtype != BF16
    assert not emit or n == tm
    once = dict(pipeline_mode=pl.Buffered(1)) if emit else {}
    row = pl.BlockSpec((tm, d), lambda i, f: (i, 0), **once)
    vec = pl.BlockSpec((1, d), lambda i, f: (0, 0))
    w1spec = pl.BlockSpec((d, tf), lambda i, f: (0, f))
    w2spec = pl.BlockSpec((tf, d), lambda i, f: (f, 0))
    res = pl.pallas_call(
        functools.partial(_ffn_kernel, alpha=alpha, nf=nf),
        grid=(n // tm, nf),
        in_specs=[row,
                  _mod_spec(mod3, tm, rows_per_batch, 3, 2),
                  _mod_spec(mod3, tm, rows_per_batch, 4, 2),
                  _mod_spec(mod3, tm, rows_per_batch, 5, 2),
                  w1spec, w2spec, vec, vec],
        out_specs=[row] + ([w1spec, w2spec] if emit else []),
        out_shape=[jax.ShapeDtypeStruct((n, d), F32)]
        + ([jax.ShapeDtypeStruct(w1.shape, BF16), jax.ShapeDtypeStruct(w2.shape, BF16)] if emit else []),
        scratch_shapes=[pltpu.VMEM((tm, d), BF16), pltpu.VMEM((tm, d), F32)],
        compiler_params=_params("arbitrary", "arbitrary"),
        name="ffn_ln2",
    )(x, mod3, mod3, mod3, w1, w2, ln_g, ln_b)
    return res


def _trunk_layer(x3, mod3, s0, cache, page_table, wts, tabs, lambda_init, alpha):
    nbatch, seq, d = x3.shape
    n = nbatch * seq
    x = x3.reshape(n, d)
    ds = wts["w_glu"].shape[0]
    da = wts["w_up_att"].shape[0]
    nh = da // (2 * HEAD_DIM)
    tm = min(512, seq) if mod3.shape[1] == 1 else min(512, n)
    assert ds == da
    u, q, k, v, gates, *w_in_b = _in_proj(x, mod3, seq, wts["w_in"], ds, tm)

    pd, q_re, q_im, w_t, lp_re, lp_im = tabs
    t_chunk = PROMPT_CHUNK if seq % PROMPT_CHUNK == 0 else seq
    gp = lp_re.shape[1] * SSM_STATE
    lam_re = lp_re[t_chunk, :, :SSM_STATE].reshape(1, gp)
    lam_im = lp_im[t_chunk, :, :SSM_STATE].reshape(1, gp)
    y, f_re, f_im = _ssm(u, w_t, pd, q_re, q_im, s0[0], s0[1], lam_re, lam_im, t_chunk, nbatch, seq // t_chunk)

    lam_vecs = wts["lam_vecs"]
    if cache is None:
        slopes = jnp.asarray([2.0 ** (-8.0 * (i + 1) / nh) for i in range(nh)], F32)
        tq = min(512, seq)
        o = _attention(q, k, v, slopes, lam_vecs, wts["subln_g"], nbatch, seq, lambda_init, tq, tq)
    else:
        o = _decode_attention(q.reshape(nbatch, seq, da), k.reshape(nbatch, seq, da), v.reshape(nbatch, seq, da),
                              cache[0], cache[1], page_table, lam_vecs, wts["subln_g"], lambda_init)
        o = o.reshape(n, da)

    mix = _mix_in(y, o, gates, wts["w_glu"], wts["w_up_ssm"], wts["w_up_att"], tm)
    x1 = _out_proj(mix, x, mod3, seq, wts["w_o"], wts["ln1_g"], wts["ln1_b"], alpha, tm)
    tf = 1024 if wts["w_ff1"].dtype == BF16 else 512
    x2, *w_ff_b = _ffn(x1, mod3, seq, wts["w_ff1"], wts["w_ff2"], wts["ln2_g"], wts["ln2_b"], alpha, tm,
                       min(tf, wts["w_ff1"].shape[1]))
    return (x2.reshape(nbatch, seq, d), k.reshape(nbatch, seq, nh, 2 * HEAD_DIM),
            v.reshape(nbatch, seq, nh, 2 * HEAD_DIM), f_re, f_im, tuple(w_in_b) + tuple(w_ff_b))


def kernel(x_prompt, x_sample, c_prompt, c_sample, cache_k, cache_v, state_ssm_re, state_ssm_im, page_table,
           w_ada, b_ada, w_in, ssm_a_re, ssm_a_im, ssm_log_dt, ssm_b_re, ssm_b_im, ssm_c_re, ssm_c_im, ssm_d,
           w_glu, w_up_ssm, lam_q1, lam_k1, lam_q2, lam_k2, subln_g, w_up_att, w_o, ln1_g, ln1_b, w_ff1, w_ff2,
           ln2_g, ln2_b):
    depth = w_in.shape[0]
    bp, seq_p, d = x_prompt.shape
    bs, seq_s, _ = x_sample.shape
    g, p = ssm_a_re.shape[1:]
    alpha = (2 * depth) ** 0.25
    npow = PROMPT_CHUNK + 1
    xp, xs = x_prompt, x_sample
    outs_p, outs_s = [], []
    c_rows = jnp.concatenate([c_sample, c_prompt], axis=0)
    c_rows = jnp.pad(c_rows, ((0, (-(bp + bs)) % 8), (0, 0)))
    n_pool, page, nh, hw = cache_k.shape[1:]
    for l in range(depth):
        lambda_init = 0.8 - 0.6 * math.exp(-0.3 * l)
        mod, mod_s = _ada(c_rows, w_ada[l], b_ada[l].reshape(1, 6 * d), min(1024, d), bs, seq_s)
        mod_p = mod[bs:bs + bp].reshape(bp, 1, 6 * d)
        wts = {
            "w_in": w_in[l], "w_ff1": w_ff1[l], "w_ff2": w_ff2[l],
            "w_glu": w_glu[l].astype(BF16), "w_up_ssm": w_up_ssm[l].astype(BF16),
            "w_up_att": w_up_att[l].astype(BF16), "w_o": w_o[l].astype(BF16),
            "ln1_g": ln1_g[l].reshape(1, d), "ln1_b": ln1_b[l].reshape(1, d),
            "ln2_g": ln2_g[l].reshape(1, d), "ln2_b": ln2_b[l].reshape(1, d),
            "subln_g": subln_g[l].reshape(1, 2 * HEAD_DIM),
            "lam_vecs": tuple(a[l].reshape(1, HEAD_DIM) for a in (lam_q1, lam_k1, lam_q2, lam_k2)),
        }
        tabs = _ssm_prep(ssm_a_re[l], ssm_a_im[l], ssm_log_dt[l], ssm_b_re[l], ssm_b_im[l], ssm_c_re[l],
                         ssm_c_im[l], ssm_d[l], npow)
        tabs = list(tabs)
        tabs[3] = _toeplitz_tiles(tabs[3], PROMPT_CHUNK)
        cache = (cache_k[l].reshape(n_pool, page * nh, hw), cache_v[l].reshape(n_pool, page * nh, hw))
        s0 = (state_ssm_re[l].reshape(bs, g * p), state_ssm_im[l].reshape(bs, g * p))
        res_s = _trunk_layer(xs, mod_s, s0, cache, page_table, wts, tabs, lambda_init, alpha)
        wts["w_in"], wts["w_ff1"], wts["w_ff2"] = res_s[5]
        zeros = jnp.zeros((bp, g * p), F32)
        res_p = _trunk_layer(xp, mod_p, (zeros, zeros), None, None, wts, tabs, lambda_init, alpha)
        xp, xs = res_p[0], res_s[0]
        outs_p.append(res_p[1:5])
        outs_s.append(res_s[1:5])

    def stack(outs, i, shape=None):
        a = jnp.stack([o[i] for o in outs])
        return a if shape is None else a.reshape((depth,) + shape)

    return (xp, xs,
            stack(outs_p, 0), stack(outs_p, 1), stack(outs_p, 2, (bp, g, p)), stack(outs_p, 3, (bp, g, p)),
            stack(outs_s, 0), stack(outs_s, 1), stack(outs_s, 2, (bs, g, p)), stack(outs_s, 3, (bs, g, p)))
```
